```python
import math
import jax, jax.numpy as jnp
from jax import lax
import numpy as np

D_MODEL = 2048
BATCH = 16
SEQ = 2048
DEPTH = 4

CTX_LEN = 256
GRID_W = 64
N_BRANCH = 3
CONV_K = 3
NORM_EPS = 1e-6

DN_HEADS = 8
DN_DK = 128
DN_DV = 128
DN_CHUNK = 64
DN_QK = DN_HEADS * DN_DK
DN_V = DN_HEADS * DN_DV

RW_HEADS = 16
RW_HEAD = 64
RW_WIDTH = RW_HEADS * RW_HEAD
RW_DECAY_LORA = 96
RW_A_LORA = 96
RW_GATE_LORA = 256
RW_GN_EPS = 64e-5

SSM_HEADS = 16
SSM_HEADDIM = 64
SSM_INNER = SSM_HEADS * SSM_HEADDIM
SSM_GROUPS = 4
SSM_STATE = 128
SSM_CHUNK = 128
SSM_XBC = SSM_INNER + 2 * SSM_GROUPS * SSM_STATE
SSM_NORM_EPS = 1e-5

N_EXPERTS = 32
TOP_K = 4
D_EXPERT = 512
SWIGLU_LIMIT = 7.0
SWIGLU_ALPHA = 1.702

GATE_COLS = N_BRANCH * D_MODEL
DN_COLS = 2 * DN_QK + 2 * DN_V + 4 * DN_HEADS
RW_COLS = 3 * RW_WIDTH + 2 * RW_DECAY_LORA + 2 * RW_A_LORA + RW_GATE_LORA
SSM_COLS = SSM_INNER + SSM_XBC + 2 * SSM_HEADS
N_IN = GATE_COLS + DN_COLS + RW_COLS + SSM_COLS

kernel_name = 'hybrid_bidir_deltanet_rwkv7_mamba2_moe'


def split_cols(z, sizes):
    idx = np.cumsum(sizes)[:-1].tolist()
    return jnp.split(z, idx, axis=-1)


def rmsnorm(x, g, eps=NORM_EPS):
    xf = x.astype(jnp.float32)
    y = xf * lax.rsqrt(jnp.mean(xf * xf, axis=-1, keepdims=True) + eps)
    return (y * g.astype(jnp.float32)).astype(x.dtype)


def l2norm(x, eps=1e-6):
    xf = x.astype(jnp.float32)
    return xf * lax.rsqrt(jnp.sum(xf * xf, axis=-1, keepdims=True) + eps)


def dwconv2d(z, w, rows, cols):
    b, t, ch = z.shape
    zg = z.reshape(b, rows, cols, ch)
    out = lax.conv_general_dilated(zg, w[:, :, None, :].astype(z.dtype), window_strides=(1, 1), padding='SAME',
                                   dimension_numbers=('NHWC', 'HWIO', 'NHWC'), feature_group_count=ch)
    return out.reshape(b, t, ch)


def grid_shift(z, rows, cols):
    b, t, ch = z.shape
    zg = z.reshape(b, rows, cols, ch)
    left = jnp.pad(zg[:, :, :-1], ((0, 0), (0, 0), (1, 0), (0, 0)))
    right = jnp.pad(zg[:, :, 1:], ((0, 0), (0, 0), (0, 1), (0, 0)))
    up = jnp.pad(zg[:, :-1], ((0, 0), (1, 0), (0, 0), (0, 0)))
    down = jnp.pad(zg[:, 1:], ((0, 0), (0, 1), (0, 0), (0, 0)))
    sel = jnp.arange(ch) % 4
    out = jnp.where(sel == 0, left, jnp.where(sel == 1, right, jnp.where(sel == 2, up, down)))
    return out.reshape(b, t, ch)


def seq_shift(z):
    prev = jnp.pad(z[:, :-1], ((0, 0), (1, 0), (0, 0)))
    nxt = jnp.pad(z[:, 1:], ((0, 0), (0, 1), (0, 0)))
    return jnp.where(jnp.arange(z.shape[-1]) % 2 == 0, prev, nxt)


def gated_delta_chunked(q, k, v, g, beta, s0):
    f32 = jnp.float32
    b, t, h, dk = k.shape
    dv = v.shape[-1]
    n = t // DN_CHUNK

    def blocks(z):
        z = z.astype(f32).reshape((b, n, DN_CHUNK, h) + z.shape[3:])
        return jnp.moveaxis(z, 3, 1)

    q, k, v, g, beta = blocks(q) * dk ** -0.5, blocks(k), blocks(v), blocks(g), blocks(beta)
    gc = jnp.cumsum(g, axis=-1)
    tril = jnp.tril(jnp.ones((DN_CHUNK, DN_CHUNK), bool))
    strict = jnp.tril(jnp.ones((DN_CHUNK, DN_CHUNK), bool), -1)
    diff = gc[..., :, None] - gc[..., None, :]
    decay = jnp.where(tril, jnp.exp(jnp.where(tril, diff, 0.0)), 0.0)
    kb = k * beta[..., None]
    lower = jnp.where(strict, jnp.einsum('bhnik,bhnjk->bhnij', kb, k) * decay, 0.0)
    tmat = lower + jnp.eye(DN_CHUNK, dtype=f32)
    u = lax.linalg.triangular_solve(tmat, v * beta[..., None], left_side=True, lower=True, unit_diagonal=True)
    w = lax.linalg.triangular_solve(tmat, kb * jnp.exp(gc)[..., None], left_side=True, lower=True,
                                    unit_diagonal=True)
    att = jnp.where(tril, jnp.einsum('bhnik,bhnjk->bhnij', q, k) * decay, 0.0)
    q_dec = q * jnp.exp(gc)[..., None]
    k_dec = k * jnp.exp(gc[..., -1:] - gc)[..., None]
    g_tot = jnp.exp(gc[..., -1])

    def step(s, inp):
        u_i, w_i, qd_i, kd_i, att_i, gt_i = inp
        v_new = u_i - jnp.einsum('bhck,bhkv->bhcv', w_i, s)
        o_i = jnp.einsum('bhck,bhkv->bhcv', qd_i, s) + jnp.einsum('bhij,bhjv->bhiv', att_i, v_new)
        s = s * gt_i[..., None, None] + jnp.einsum('bhck,bhcv->bhkv', kd_i, v_new)
        return s, o_i

    xs = tuple(jnp.moveaxis(z, 2, 0) for z in (u, w, q_dec, k_dec, att, g_tot))
    s_fin, o = lax.scan(step, s0.astype(f32), xs)
    o = jnp.moveaxis(jnp.moveaxis(o, 0, 2), 1, 3).reshape(b, t, h, dv)
    return o, s_fin


def rwkv7_scan(r, w, k, v, a, bb, s0):
    f32 = jnp.float32
    xs = tuple(jnp.moveaxis(z.astype(f32), 1, 0) for z in (r, w, k, v, a, bb))

    def step(s, inp):
        r_t, w_t, k_t, v_t, a_t, b_t = inp
        sa = jnp.einsum('bhvk,bhk->bhv', s, a_t)
        s = s * w_t[:, :, None, :] + sa[..., :, None] * b_t[..., None, :] + v_t[..., :, None] * k_t[..., None, :]
        return s, jnp.einsum('bhvk,bhk->bhv', s, r_t)

    s_fin, y = lax.scan(step, s0.astype(f32), xs)
    return jnp.moveaxis(y, 0, 1), s_fin


def ssd_chunked(x, a, bm, cm, s0):
    f32 = jnp.float32
    b, t, h, p = x.shape
    ng, ns = bm.shape[2], bm.shape[3]
    hr = h // ng
    nc = t // SSM_CHUNK
    x = x.astype(f32).reshape(b, nc, SSM_CHUNK, ng, hr, p)
    a = a.astype(f32).reshape(b, nc, SSM_CHUNK, ng, hr)
    bm = bm.astype(f32).reshape(b, nc, SSM_CHUNK, ng, ns)
    cm = cm.astype(f32).reshape(b, nc, SSM_CHUNK, ng, ns)
    acum = jnp.cumsum(a, axis=2)
    tril = jnp.tril(jnp.ones((SSM_CHUNK, SSM_CHUNK), bool))[:, :, None, None]
    seg = acum[:, :, :, None] - acum[:, :, None, :]
    lmat = jnp.where(tril, jnp.exp(jnp.where(tril, seg, 0.0)), 0.0)
    cb = jnp.einsum('bclgn,bcsgn->bclsg', cm, bm)
    y_diag = jnp.einsum('bclsg,bclsgr,bcsgrp->bclgrp', cb, lmat, x)
    decay_states = jnp.exp(acum[:, :, -1:] - acum)
    chunk_states = jnp.einsum('bcsgn,bcsgr,bcsgrp->bcgrpn', bm, decay_states, x)
    chunk_decay = jnp.exp(acum[:, :, -1])

    def step(s, inp):
        cs, cd = inp
        return s * cd[..., None, None] + cs, s

    s_fin, s_in = lax.scan(step, s0.astype(f32).reshape(b, ng, hr, p, ns),
                           (jnp.moveaxis(chunk_states, 1, 0), jnp.moveaxis(chunk_decay, 1, 0)))
    s_in = jnp.moveaxis(s_in, 0, 1)
    y_off = jnp.einsum('bclgn,bcgrpn,bclgr->bclgrp', cm, s_in, jnp.exp(acum))
    return (y_diag + y_off).reshape(b, t, h, p), s_fin.reshape(b, h, p, ns)


def bidirectional(scan_fn, ctx_dirs, lat_dirs, s0):
    flip = lambda z: jnp.flip(z, axis=1)
    y_ctx, y_lat = [], []
    for d in range(2):
        cin, lin = ctx_dirs[d], lat_dirs[d]
        if d == 1:
            cin, lin = [flip(z) for z in cin], [flip(z) for z in lin]
        yc, s_ctx = scan_fn(*cin, s0)
        yl, _ = scan_fn(*lin, s_ctx)
        if d == 1:
            yc, yl = flip(yc), flip(yl)
        y_ctx.append(yc)
        y_lat.append(yl)
    return y_ctx[0] + y_ctx[1], y_lat[0] + y_lat[1]


def deltanet_mixer(p_ctx, p_lat, conv_w, a_log, dt_bias, norm_g, rows, with_ctx):
    f32 = jnp.float32

    def prep(p, grid_rows, grid_cols):
        bsz, t, _ = p.shape
        qkv, gate, beta_raw, a_raw = split_cols(p, [2 * DN_QK + DN_V, DN_V, 2 * DN_HEADS, 2 * DN_HEADS])
        qkv = jax.nn.silu(dwconv2d(qkv, conv_w, grid_rows, grid_cols))
        q, k, v = split_cols(qkv, [DN_QK, DN_QK, DN_V])
        q = l2norm(q.reshape(bsz, t, DN_HEADS, DN_DK))
        k = l2norm(k.reshape(bsz, t, DN_HEADS, DN_DK))
        v = v.astype(f32).reshape(bsz, t, DN_HEADS, DN_DV)
        beta = jax.nn.sigmoid(beta_raw.astype(f32)).reshape(bsz, t, 2, DN_HEADS)
        logdec = -jnp.exp(a_log.astype(f32)) * jax.nn.softplus(
            a_raw.astype(f32).reshape(bsz, t, 2, DN_HEADS) + dt_bias.astype(f32))
        return [(q, k, v, logdec[:, :, d], beta[:, :, d]) for d in range(2)], gate

    def readout(o, gate):
        bsz, t = o.shape[:2]
        gate = gate.astype(f32).reshape(bsz, t, DN_HEADS, DN_DV)
        y = rmsnorm(o, norm_g) * jax.nn.silu(gate)
        return y.reshape(bsz, t, DN_V).astype(p_lat.dtype)

    ctx_dirs, gate_c = prep(p_ctx, 1, p_ctx.shape[1])
    lat_dirs, gate_l = prep(p_lat, rows, GRID_W)
    s0 = jnp.zeros((p_lat.shape[0], DN_HEADS, DN_DK, DN_DV), f32)
    o_c, o_l = bidirectional(gated_delta_chunked, ctx_dirs, lat_dirs, s0)
    return (readout(o_c, gate_c) if with_ctx else None), readout(o_l, gate_l)


def rwkv7_mixer(p_ctx, p_lat, mu, w0, w2, a0, a2, g2, k_k, k_a, r_k, ln_g, ln_b, rows, with_ctx):
    f32 = jnp.float32

    def prep(p, shifted):
        bsz, t, _ = p.shape
        p = p + (shifted - p) * mu
        r, k, v, wl, al, gl = split_cols(p, [RW_WIDTH, RW_WIDTH, RW_WIDTH, 2 * RW_DECAY_LORA,
                                             2 * RW_A_LORA, RW_GATE_LORA])
        heads = lambda z: z.astype(f32).reshape(bsz, t, RW_HEADS, RW_HEAD)
        w_raw = w0 + jnp.einsum('btdr,drc->btdc', jnp.tanh(wl.reshape(bsz, t, 2, RW_DECAY_LORA)), w2)
        decay = jnp.exp(-jnp.exp(-jax.nn.softplus(-w_raw.astype(f32)) - 0.5))
        icl = jax.nn.sigmoid((a0 + jnp.einsum('btdr,drc->btdc', al.reshape(bsz, t, 2, RW_A_LORA), a2)).astype(f32))
        gate = jax.nn.sigmoid(gl) @ g2
        rh, kh, vh = heads(r), heads(k), heads(v)
        kk = l2norm(kh * k_k.astype(f32).reshape(RW_HEADS, RW_HEAD))
        ka = k_a.astype(f32).reshape(RW_HEADS, RW_HEAD)
        dirs, keys = [], []
        for d in range(2):
            a_d = heads(icl[:, :, d])
            k_d = kh * (1.0 + (a_d - 1.0) * ka)
            dirs.append((rh, heads(decay[:, :, d]), k_d, vh, -kk, kk * a_d))
            keys.append(k_d)
        return dirs, (gate, rh, vh, keys)

    def readout(y, aux):
        gate, rh, vh, keys = aux
        bsz, t = y.shape[:2]
        mean = jnp.mean(y, axis=-1, keepdims=True)
        var = jnp.mean(jnp.square(y - mean), axis=-1, keepdims=True)
        y = ((y - mean) * lax.rsqrt(var + RW_GN_EPS)).reshape(bsz, t, RW_WIDTH) * ln_g.astype(f32) + ln_b.astype(f32)
        rk = r_k.astype(f32)
        bonus = (jnp.sum(rh * keys[0] * rk, axis=-1, keepdims=True) * vh
                 + jnp.sum(rh * keys[1] * rk, axis=-1, keepdims=True) * vh)
        return ((y + bonus.reshape(bsz, t, RW_WIDTH)) * gate.astype(f32)).astype(p_lat.dtype)

    ctx_dirs, aux_c = prep(p_ctx, seq_shift(p_ctx))
    lat_dirs, aux_l = prep(p_lat, grid_shift(p_lat, rows, GRID_W))
    s0 = jnp.zeros((p_lat.shape[0], RW_HEADS, RW_HEAD, RW_HEAD), f32)
    y_c, y_l = bidirectional(rwkv7_scan, ctx_dirs, lat_dirs, s0)
    return (readout(y_c, aux_c) if with_ctx else None), readout(y_l, aux_l)


def mamba2_mixer(p_ctx, p_lat, conv_w, conv_b, a_log, dt_bias, d_skip, norm_g, rows, with_ctx):
    f32 = jnp.float32

    def prep(p, grid_rows, grid_cols):
        bsz, t, _ = p.shape
        z, xbc, dt_raw = split_cols(p, [SSM_INNER, SSM_XBC, 2 * SSM_HEADS])
        xbc = jax.nn.silu(dwconv2d(xbc, conv_w, grid_rows, grid_cols) + conv_b)
        xs, bm, cm = split_cols(xbc, [SSM_INNER, SSM_GROUPS * SSM_STATE, SSM_GROUPS * SSM_STATE])
        xs = xs.astype(f32).reshape(bsz, t, SSM_HEADS, SSM_HEADDIM)
        bm = bm.reshape(bsz, t, SSM_GROUPS, SSM_STATE)
        cm = cm.reshape(bsz, t, SSM_GROUPS, SSM_STATE)
        dt = jax.nn.softplus(dt_raw.astype(f32).reshape(bsz, t, 2, SSM_HEADS) + dt_bias.astype(f32))
        a = -jnp.exp(a_log.astype(f32)) * dt
        return [(xs * dt[:, :, d, :, None], a[:, :, d], bm, cm) for d in range(2)], (z, xs)

    def readout(y, aux):
        z, xs = aux
        bsz, t = y.shape[:2]
        y = (y + d_skip.astype(f32)[:, None] * xs).reshape(bsz, t, SSM_INNER) * jax.nn.silu(z.astype(f32))
        y = rmsnorm(y.reshape(bsz, t, SSM_GROUPS, SSM_INNER // SSM_GROUPS),
                    norm_g.reshape(SSM_GROUPS, SSM_INNER // SSM_GROUPS), eps=SSM_NORM_EPS)
        return y.reshape(bsz, t, SSM_INNER).astype(p_lat.dtype)

    ctx_dirs, aux_c = prep(p_ctx, 1, p_ctx.shape[1])
    lat_dirs, aux_l = prep(p_lat, rows, GRID_W)
    s0 = jnp.zeros((p_lat.shape[0], SSM_HEADS, SSM_HEADDIM, SSM_STATE), f32)
    y_c, y_l = bidirectional(ssd_chunked, ctx_dirs, lat_dirs, s0)
    return (readout(y_c, aux_c) if with_ctx else None), readout(y_l, aux_l)


def merge_branches(gates, oa, ob, oc, w_branch, w_out):
    gates = jax.nn.sigmoid(gates.reshape(gates.shape[:-1] + (N_BRANCH, D_MODEL)))
    y = (gates[..., 0, :] * (oa @ w_branch[0]) + gates[..., 1, :] * (ob @ w_branch[1])
         + gates[..., 2, :] * (oc @ w_branch[2]))
    return y @ w_out


def moe_ffn(h, w_router, b_router, w_e1, b_e1, w_e2, b_e2):
    bsz, t, d = h.shape
    tok = h.reshape(bsz * t, d)
    logits = (tok @ w_router + b_router).astype(jnp.float32)
    top_val, top_idx = lax.top_k(logits, TOP_K)
    top_w = jax.nn.softmax(top_val, axis=-1)
    combine = jnp.einsum('nk,nke->ne', top_w, jax.nn.one_hot(top_idx, N_EXPERTS, dtype=jnp.float32))
    y = jnp.zeros((bsz * t, d), jnp.float32)
    for e in range(N_EXPERTS):
        hu = tok @ w_e1[e] + b_e1[e]
        gt = jnp.minimum(hu[:, :D_EXPERT], SWIGLU_LIMIT)
        up = jnp.clip(hu[:, D_EXPERT:], -SWIGLU_LIMIT, SWIGLU_LIMIT)
        act = gt * jax.nn.sigmoid(SWIGLU_ALPHA * gt) * (up + 1.0)
        y = y + combine[:, e:e + 1] * (act @ w_e2[e] + b_e2[e])
    return y.astype(h.dtype).reshape(bsz, t, d)


def setup_inputs(seed: int = 0) -> dict:
    key = jax.random.key(seed)
    keys = jax.random.split(key, 64)
    counter = [0]
    f32 = jnp.float32

    def nxt():
        counter[0] += 1
        return keys[counter[0] - 1]

    def normal(shape, scale):
        return jax.random.normal(nxt(), shape, f32) * scale

    def gain(shape):
        return 1.0 + normal(shape, 0.02)

    def a_log(shape):
        return jnp.log(jax.random.uniform(nxt(), shape, f32, 1.0, 16.0))

    def dt_bias(shape):
        dt = jnp.exp(jax.random.uniform(nxt(), shape, f32, math.log(1e-3), math.log(1e-1)))
        return dt + jnp.log(-jnp.expm1(-dt))

    L, D = DEPTH, D_MODEL
    return {
        'x': normal((BATCH, SEQ, D), 1.0),
        'c': normal((BATCH, D), 1.0),
        'ctx': normal((BATCH, CTX_LEN, D), 1.0),
        'c_ctx': normal((D,), 1.0),
        'norm_mix_g': gain((L, D)),
        'norm_ffn_g': gain((L, D)),
        'w_mod': normal((L, D, 6 * D), 0.5 * D ** -0.5),
        'b_mod': normal((L, 6 * D), 0.02),
        'w_in': normal((L, D, N_IN), D ** -0.5),
        'dn_conv': normal((L, CONV_K, CONV_K, 2 * DN_QK + DN_V), 1.0 / CONV_K),
        'dn_a_log': a_log((L, 2, DN_HEADS)),
        'dn_dt_bias': dt_bias((L, 2, DN_HEADS)),
        'dn_norm_g': gain((L, DN_DV)),
        'rw_mu': jax.random.uniform(nxt(), (L, RW_COLS), f32),
        'rw_w0': -2.0 + normal((L, 2, RW_WIDTH), 0.5),
        'rw_w2': normal((L, 2, RW_DECAY_LORA, RW_WIDTH), 0.5 * RW_DECAY_LORA ** -0.5),
        'rw_a0': normal((L, 2, RW_WIDTH), 0.5),
        'rw_a2': normal((L, 2, RW_A_LORA, RW_WIDTH), RW_A_LORA ** -0.5),
        'rw_g2': normal((L, RW_GATE_LORA, RW_WIDTH), RW_GATE_LORA ** -0.5),
        'rw_k_k': 1.0 + normal((L, RW_WIDTH), 0.1),
        'rw_k_a': 1.0 + normal((L, RW_WIDTH), 0.1),
        'rw_r_k': normal((L, RW_HEADS, RW_HEAD), 0.1),
        'rw_ln_g': gain((L, RW_WIDTH)),
        'rw_ln_b': normal((L, RW_WIDTH), 0.02),
        'ssm_conv': normal((L, CONV_K, CONV_K, SSM_XBC), 1.0 / CONV_K),
        'ssm_conv_b': normal((L, SSM_XBC), 0.02),
        'ssm_a_log': a_log((L, 2, SSM_HEADS)),
        'ssm_dt_bias': dt_bias((L, 2, SSM_HEADS)),
        'ssm_d': 1.0 + normal((L, SSM_HEADS), 0.1),
        'ssm_norm_g': gain((L, SSM_INNER)),
        'w_branch': normal((L, N_BRANCH, DN_V, D), DN_V ** -0.5),
        'w_out': normal((L, D, D), D ** -0.5),
        'w_router': normal((L, D, N_EXPERTS), D ** -0.5),
        'b_router': normal((L, N_EXPERTS), 0.01),
        'w_e1': normal((L, N_EXPERTS, D, 2 * D_EXPERT), D ** -0.5),
        'b_e1': normal((L, N_EXPERTS, 2 * D_EXPERT), 0.01),
        'w_e2': normal((L, N_EXPERTS, D_EXPERT, D), D_EXPERT ** -0.5),
        'b_e2': normal((L, N_EXPERTS, D), 0.01),
        'final_norm_g': gain((D,)),
    }


def reference(x, c, ctx, c_ctx, norm_mix_g, norm_ffn_g, w_mod, b_mod, w_in,
              dn_conv, dn_a_log, dn_dt_bias, dn_norm_g,
              rw_mu, rw_w0, rw_w2, rw_a0, rw_a2, rw_g2, rw_k_k, rw_k_a, rw_r_k, rw_ln_g, rw_ln_b,
              ssm_conv, ssm_conv_b, ssm_a_log, ssm_dt_bias, ssm_d, ssm_norm_g,
              w_branch, w_out, w_router, b_router, w_e1, b_e1, w_e2, b_e2, final_norm_g):
    rows = x.shape[1] // GRID_W
    ctx_len = ctx.shape[1]
    cond_lat = jax.nn.silu(c)
    cond_ctx = jax.nn.silu(c_ctx)
    x_lat, x_ctx = x, ctx
    for i in range(DEPTH):
        last = i == DEPTH - 1
        mod_l = jnp.split((cond_lat @ w_mod[i] + b_mod[i])[:, None, :], 6, axis=-1)
        mod_c = jnp.split((cond_ctx @ w_mod[i] + b_mod[i])[None, None, :], 6, axis=-1)

        h_l = rmsnorm(x_lat, norm_mix_g[i]) * (1.0 + mod_l[1]) + mod_l[0]
        h_c = rmsnorm(x_ctx, norm_mix_g[i]) * (1.0 + mod_c[1]) + mod_c[0]
        g_l, dn_l, rw_l, ss_l = split_cols(h_l @ w_in[i], [GATE_COLS, DN_COLS, RW_COLS, SSM_COLS])
        g_c, dn_c, rw_c, ss_c = split_cols(h_c @ w_in[i], [GATE_COLS, DN_COLS, RW_COLS, SSM_COLS])
        with_ctx = not last
        a_c, a_l = deltanet_mixer(dn_c, dn_l, dn_conv[i], dn_a_log[i], dn_dt_bias[i], dn_norm_g[i], rows, with_ctx)
        b_c, b_l = rwkv7_mixer(rw_c, rw_l, rw_mu[i], rw_w0[i], rw_w2[i], rw_a0[i], rw_a2[i], rw_g2[i],
                               rw_k_k[i], rw_k_a[i], rw_r_k[i], rw_ln_g[i], rw_ln_b[i], rows, with_ctx)
        m_c, m_l = mamba2_mixer(ss_c, ss_l, ssm_conv[i], ssm_conv_b[i], ssm_a_log[i], ssm_dt_bias[i],
                                ssm_d[i], ssm_norm_g[i], rows, with_ctx)
        x_lat = x_lat + mod_l[2] * merge_branches(g_l, a_l, b_l, m_l, w_branch[i], w_out[i])

        f_l = rmsnorm(x_lat, norm_ffn_g[i]) * (1.0 + mod_l[4]) + mod_l[3]
        if last:
            x_lat = x_lat + mod_l[5] * moe_ffn(f_l, w_router[i], b_router[i], w_e1[i], b_e1[i], w_e2[i], b_e2[i])
        else:
            x_ctx = x_ctx + mod_c[2] * merge_branches(g_c, a_c, b_c, m_c, w_branch[i], w_out[i])
            f_c = rmsnorm(x_ctx, norm_ffn_g[i]) * (1.0 + mod_c[4]) + mod_c[3]
            f = moe_ffn(jnp.concatenate([f_c, f_l], axis=1), w_router[i], b_router[i],
                        w_e1[i], b_e1[i], w_e2[i], b_e2[i])
            x_ctx = x_ctx + mod_c[5] * f[:, :ctx_len]
            x_lat = x_lat + mod_l[5] * f[:, ctx_len:]
    return rmsnorm(x_lat, final_norm_g)
```

```python
import functools
import math

import jax
import jax.numpy as jnp
from jax import lax
from jax.experimental import pallas as pl
from jax.experimental.pallas import tpu as pltpu

F32 = jnp.float32
BF16 = jnp.bfloat16

GRID_W = 64
NORM_EPS = 1e-6
L2_EPS = 1e-6

DN_HEADS = 8
DN_DK = 128
DN_CHUNK = 64
DN_QK = DN_HEADS * DN_DK
DN_V = DN_QK

RW_HEADS = 16
RW_HEAD = 64
RW_WIDTH = RW_HEADS * RW_HEAD
RW_DECAY_LORA = 96
RW_A_LORA = 96
RW_GATE_LORA = 256
RW_GN_EPS = 64e-5
RW_CHUNK = 64

SSM_HEADS = 16
SSM_HEADDIM = 64
SSM_INNER = SSM_HEADS * SSM_HEADDIM
SSM_GROUPS = 4
SSM_STATE = 128
SSM_CHUNK = 128
SSM_XBC = SSM_INNER + 2 * SSM_GROUPS * SSM_STATE
SSM_NORM_EPS = 1e-5

N_EXPERTS = 32
TOP_K = 4
D_EXPERT = 512
SWIGLU_LIMIT = 7.0
SWIGLU_ALPHA = 1.702

V7X_VMEM_LIMIT = 56 * 1024 * 1024
LANES = 128

P_GATE = 0
P_DNQKV = None


def _p_layout(d_model):
    off = {}
    pos = 0
    for name, width in (("gate", 3 * d_model), ("dn_qkv", 2 * DN_QK + DN_V), ("dn_gate", DN_V),
                        ("rw_rkv", 3 * RW_WIDTH), ("rw_lora", 1024), ("ssm_z", SSM_INNER),
                        ("ssm_xbc", SSM_XBC), ("small", 512)):
        off[name] = pos
        pos += width
    off["total"] = pos
    return off


RWL_W, RWL_A, RWL_G = 0, 256, 512
SM_BETA, SM_A, SM_DT = 0, 16, 32


def _cparams(sem):
    return pltpu.CompilerParams(dimension_semantics=sem, vmem_limit_bytes=V7X_VMEM_LIMIT)


def _tile(n, cap, mult=8):
    best = None
    for t in range(mult, min(n, cap) + 1, mult):
        if n % t == 0:
            best = t
    assert best is not None, (n, cap)
    return best


def _dot(a, b):
    return jnp.dot(a.astype(BF16), b.astype(BF16), preferred_element_type=F32)


def _dot_nt(a, b):
    return lax.dot_general(a.astype(BF16), b.astype(BF16), (((1,), (1,)), ((), ())),
                           preferred_element_type=F32)


def _dot_tn(a, b):
    return lax.dot_general(a.astype(BF16), b.astype(BF16), (((0,), (0,)), ((), ())),
                           preferred_element_type=F32)


def _split2(a):
    hi = a.astype(BF16)
    lo = (a - hi.astype(F32)).astype(BF16)
    return hi, lo


def _dot3(a, b):
    ah, al = _split2(a)
    bh, bl = _split2(b)
    d = functools.partial(jnp.dot, preferred_element_type=F32)
    return d(ah, bh) + d(ah, bl) + d(al, bh)


def _dot_exact_lhs(m_bf16, x):
    x1 = x.astype(BF16)
    r1 = x - x1.astype(F32)
    x2 = r1.astype(BF16)
    x3 = (r1 - x2.astype(F32)).astype(BF16)
    d = functools.partial(jnp.dot, preferred_element_type=F32)
    return d(m_bf16, x1) + d(m_bf16, x2) + d(m_bf16, x3)


def _dot_exact_rhs(x, m_bf16):
    x1 = x.astype(BF16)
    r1 = x - x1.astype(F32)
    x2 = r1.astype(BF16)
    x3 = (r1 - x2.astype(F32)).astype(BF16)
    d = functools.partial(jnp.dot, preferred_element_type=F32)
    return d(x1, m_bf16) + d(x2, m_bf16) + d(x3, m_bf16)


def _sigmoid(x):
    return 1.0 / (1.0 + jnp.exp(-x))


def _silu(x):
    return x * _sigmoid(x)


def _softplus(x):
    return jnp.maximum(x, 0.0) + jnp.log(1.0 + jnp.exp(-jnp.abs(x)))


def _inv_unit_lower(neg_lowers, size):
    shape = neg_lowers[0].shape
    ri = lax.broadcasted_iota(jnp.int32, shape, 0)
    ci = lax.broadcasted_iota(jnp.int32, shape, 1)
    eye = jnp.where(ri == ci, 1.0, 0.0)
    ps = [eye + n for n in neg_lowers]
    qs = list(neg_lowers)
    power = 2
    while power < size:
        qs = [_dot3(q, q) for q in qs]
        ps = [p + _dot3(p, q) for p, q in zip(ps, qs)]
        power *= 2
    return ps


def _mod_kernel(c_ref, w_ref, b_ref, o_ref):
    c = c_ref[...]
    s = _silu(c)
    o_ref[0] = jnp.dot(s, w_ref[0], preferred_element_type=F32,
                       precision=lax.Precision.HIGHEST) + b_ref[0]


def _modulation(cond, w_mod, b_mod):
    n_layers, d, n6 = w_mod.shape
    rows = cond.shape[0]
    tn = _tile(n6, 1024, LANES)
    return pl.pallas_call(
        _mod_kernel,
        grid=(n_layers, n6 // tn),
        in_specs=[pl.BlockSpec((rows, d), lambda l, n: (0, 0)),
                  pl.BlockSpec((1, d, tn), lambda l, n: (l, 0, n)),
                  pl.BlockSpec((1, 1, tn), lambda l, n: (l, 0, n))],
        out_specs=pl.BlockSpec((1, rows, tn), lambda l, n: (l, 0, n)),
        out_shape=jax.ShapeDtypeStruct((n_layers, rows, n6), F32),
        compiler_params=_cparams(("arbitrary", "arbitrary")),
        name="modulation",
    )(cond, w_mod, b_mod.reshape(n_layers, 1, n6))


def _modulated_norm(x, g, ml, mc, is_ctx, shift_row, scale_row):
    y = x * lax.rsqrt(jnp.mean(x * x, axis=-1, keepdims=True) + NORM_EPS) * g
    shift = jnp.where(is_ctx, mc[shift_row:shift_row + 1], ml[shift_row:shift_row + 1])
    scale = jnp.where(is_ctx, mc[scale_row:scale_row + 1], ml[scale_row:scale_row + 1])
    return y * (1.0 + scale) + shift


def _ctx_rows(tile_index, tm, seg, ctx_len):
    row = (tile_index * tm) % seg + lax.broadcasted_iota(jnp.int32, (tm, 1), 0)
    return row < ctx_len


def _inproj_kernel(x_ref, g_ref, ml_ref, mc_ref, w_ref, o_ref, h_scr, *, tm, seg, ctx_len):
    @pl.when(pl.program_id(1) == 0)
    def _():
        is_ctx = _ctx_rows(pl.program_id(0), tm, seg, ctx_len)
        h = _modulated_norm(x_ref[...], g_ref[...], ml_ref[0], mc_ref[0], is_ctx, 0, 1)
        h_scr[...] = h.astype(BF16)

    o_ref[...] = jnp.dot(h_scr[...], w_ref[...], preferred_element_type=F32)


def _in_projection(x, g, mod, w_packed, *, n_batch, seg, ctx_len):
    ntok, d = x.shape
    n_out = w_packed.shape[1]
    tm = _tile(seg, 1024)
    tn = 512
    per = seg // tm
    kern = functools.partial(_inproj_kernel, tm=tm, seg=seg, ctx_len=ctx_len)
    return pl.pallas_call(
        kern,
        grid=(ntok // tm, n_out // tn),
        in_specs=[pl.BlockSpec((tm, d), lambda i, n: (i, 0)),
                  pl.BlockSpec((1, d), lambda i, n: (0, 0)),
                  pl.BlockSpec((1, 6, d), lambda i, n: (i // per, 0, 0)),
                  pl.BlockSpec((1, 6, d), lambda i, n: (n_batch, 0, 0)),
                  pl.BlockSpec((d, tn), lambda i, n: (0, n))],
        out_specs=pl.BlockSpec((tm, tn), lambda i, n: (i, n)),
        out_shape=jax.ShapeDtypeStruct((ntok, n_out), F32),
        scratch_shapes=[pltpu.VMEM((tm, d), BF16)],
        compiler_params=_cparams(("arbitrary", "arbitrary")),
        name="in_projection",
    )(x, g.reshape(1, d), mod, mod, w_packed)


def _pack_w_in(w_in_l, d_model):
    gate_cols = 3 * d_model
    dn0 = gate_cols
    dn_cols = 2 * DN_QK + 2 * DN_V + 4 * DN_HEADS
    rw0 = dn0 + dn_cols
    rw_cols = 3 * RW_WIDTH + 2 * RW_DECAY_LORA + 2 * RW_A_LORA + RW_GATE_LORA
    ss0 = rw0 + rw_cols
    z = lambda n: jnp.zeros((d_model, n), w_in_l.dtype)
    c = lambda a, n: w_in_l[:, a:a + n]
    dn_small = dn0 + 2 * DN_QK + 2 * DN_V
    rw_l = rw0 + 3 * RW_WIDTH
    parts = [
        c(0, gate_cols),
        c(dn0, 2 * DN_QK + DN_V),
        c(dn0 + 2 * DN_QK + DN_V, DN_V),
        c(rw0, 3 * RW_WIDTH),
        c(rw_l, 2 * RW_DECAY_LORA), z(64),
        c(rw_l + 2 * RW_DECAY_LORA, 2 * RW_A_LORA), z(64),
        c(rw_l + 2 * RW_DECAY_LORA + 2 * RW_A_LORA, RW_GATE_LORA), z(256),
        c(ss0, SSM_INNER),
        c(ss0 + SSM_INNER, SSM_XBC),
        c(dn_small, 4 * DN_HEADS), c(ss0 + SSM_INNER + SSM_XBC, 2 * SSM_HEADS), z(512 - 64),
    ]
    return jnp.concatenate(parts, axis=1).astype(BF16)


CONV_PAD = 72


def _conv_rows(seg, ctx_len):
    ctx0 = CONV_PAD
    lat0 = ctx0 + ctx_len + CONV_PAD
    total = lat0 + (seg - ctx_len) + CONV_PAD
    return ctx0, lat0, total


def _fill_padded(scr, x, seg, ctx_len):
    c = x.shape[1]
    n_lat = seg - ctx_len
    ctx0, lat0, total = _conv_rows(seg, ctx_len)
    xc = x[:ctx_len]
    xl = x[ctx_len:]
    col = lax.broadcasted_iota(jnp.int32, (n_lat, c), 0) % GRID_W
    zpad = jnp.zeros((CONV_PAD, c), F32)
    for i, lat in enumerate((xl, jnp.where(col == GRID_W - 1, 0.0, xl), jnp.where(col == 0, 0.0, xl))):
        scr[i, pl.ds(0, CONV_PAD), :] = zpad
        scr[i, pl.ds(ctx0, ctx_len), :] = xc
        scr[i, pl.ds(ctx0 + ctx_len, CONV_PAD), :] = zpad
        scr[i, pl.ds(lat0, n_lat), :] = lat
        scr[i, pl.ds(lat0 + n_lat, CONV_PAD), :] = zpad


def _dwconv_block(scr, w9, seg, ctx_len):
    n_lat = seg - ctx_len
    ctx0, lat0, _ = _conv_rows(seg, ctx_len)
    acc_c = None
    acc_l = None
    for dc in (-1, 0, 1):
        src = {-1: 1, 0: 0, 1: 2}[dc]
        k = 3 + (dc + 1)
        term = scr[0, pl.ds(ctx0 + dc, ctx_len), :] * w9[k:k + 1]
        acc_c = term if acc_c is None else acc_c + term
        for dr in (-1, 0, 1):
            k = (dr + 1) * 3 + (dc + 1)
            term = scr[src, pl.ds(lat0 + dr * GRID_W + dc, n_lat), :] * w9[k:k + 1]
            acc_l = term if acc_l is None else acc_l + term
    return acc_c, acc_l


def _conv_kernel(x_ref, w_ref, b_ref, o_ref, scr, *, seg, ctx_len, n_norm_blocks):
    _fill_padded(scr, x_ref[...], seg, ctx_len)
    do_norm = pl.program_id(1) < n_norm_blocks
    for part, rows in zip(_dwconv_block(scr, w_ref[...], seg, ctx_len),
                          (pl.ds(0, ctx_len), pl.ds(ctx_len, seg - ctx_len))):
        y = _silu(part + b_ref[...])
        yn = y * lax.rsqrt(jnp.sum(y * y, axis=-1, keepdims=True) + L2_EPS)
        o_ref[rows, :] = jnp.where(do_norm, yn, y)


def _conv_silu(p, conv_w, conv_b, *, n_batch, seg, ctx_len, col0, n_norm_blocks, name):
    ntok = p.shape[0]
    cw = conv_w.shape[-1]
    kern = functools.partial(_conv_kernel, seg=seg, ctx_len=ctx_len, n_norm_blocks=n_norm_blocks)
    return pl.pallas_call(
        kern,
        grid=(n_batch, cw // LANES),
        in_specs=[pl.BlockSpec((seg, LANES), lambda b, j: (b, col0 // LANES + j)),
                  pl.BlockSpec((9, LANES), lambda b, j: (0, j)),
                  pl.BlockSpec((1, LANES), lambda b, j: (0, j))],
        out_specs=pl.BlockSpec((seg, LANES), lambda b, j: (b, j)),
        out_shape=jax.ShapeDtypeStruct((ntok, cw), F32),
        scratch_shapes=[pltpu.VMEM((3, _conv_rows(seg, ctx_len)[2], LANES), F32)],
        compiler_params=_cparams(("arbitrary", "arbitrary")),
        name=name,
    )(p, conv_w.reshape(9, cw), conv_b.reshape(1, cw))


def _dn_conv(p, conv_w, *, n_batch, seg, ctx_len, col0):
    return _conv_silu(p, conv_w, jnp.zeros((conv_w.shape[-1],), F32), n_batch=n_batch, seg=seg,
                      ctx_len=ctx_len, col0=col0, n_norm_blocks=2 * DN_QK // LANES, name="dn_conv")


SO_BETA, SO_G, SO_DT, SO_A = 0, 16, 32, 64


def _small_kernel(x_ref, par_ref, o_ref):
    x = x_ref[...][:, :LANES]
    par = par_ref[...]
    tm = x.shape[0]
    beta = _sigmoid(x[:, SM_BETA:SM_BETA + 16])
    sp = _softplus(x[:, SM_A:SM_A + 48] + par[0:1, SM_A:SM_A + 48])
    g = sp[:, 0:16] * par[1:2, SM_A:SM_A + 16]
    dt = sp[:, 16:48]
    a = dt * par[1:2, SM_DT:SM_DT + 32]
    o_ref[...] = jnp.concatenate([beta, g, dt, a, jnp.zeros((tm, LANES - 96), F32)], axis=1)


def _small_prep(p, dn_a_log, dn_dt_bias, ssm_a_log, ssm_dt_bias, *, col0):
    ntok = p.shape[0]
    tm = _tile(ntok, 1024)
    zeros16 = jnp.zeros((16,), F32)
    bias = jnp.concatenate([zeros16, dn_dt_bias.reshape(-1), ssm_dt_bias.reshape(-1),
                            jnp.zeros((LANES - 64,), F32)])
    nega = jnp.concatenate([zeros16, -jnp.exp(dn_a_log.reshape(-1)), -jnp.exp(ssm_a_log.reshape(-1)),
                            jnp.zeros((LANES - 64,), F32)])
    par = jnp.concatenate([bias[None], nega[None], jnp.zeros((6, LANES), F32)], axis=0)
    return pl.pallas_call(
        _small_kernel,
        grid=(ntok // tm,),
        in_specs=[pl.BlockSpec((tm, LANES), lambda i: (i, col0 // LANES)),
                  pl.BlockSpec((8, LANES), lambda i: (0, 0))],
        out_specs=pl.BlockSpec((tm, LANES), lambda i: (i, 0)),
        out_shape=jax.ShapeDtypeStruct((ntok, LANES), F32),
        compiler_params=_cparams(("arbitrary",)),
        name="small_prep",
    )(p, par)


def _chunk_of_step(d, s, n_ctx, n_all):
    bwd = jnp.where(s < n_ctx, n_ctx - 1 - s, n_all + n_ctx - 1 - s)
    return jnp.where(d == 0, s, bwd)


def _order_masks(d, size):
    ri = lax.broadcasted_iota(jnp.int32, (size, size), 0)
    ci = lax.broadcasted_iota(jnp.int32, (size, size), 1)
    delta = (ri - ci) * (1 - 2 * d)
    return delta >= 0, delta > 0, ri == ci


def _col_to_row(col, eye):
    return jnp.sum(jnp.where(eye, col, 0.0), axis=0, keepdims=True)


def _dn_scan_kernel(q_ref, k_ref, v_ref, sm_ref, o_ref, s_scr):
    d = pl.program_id(1)
    c = DN_CHUNK
    h_n = DN_HEADS

    @pl.when(pl.program_id(2) == 0)
    def _():
        s_scr[...] = jnp.zeros(s_scr.shape, F32)

    incl, strict, eye = _order_masks(d, c)
    sm = sm_ref[...]
    fwd = d == 0
    beta = jnp.where(fwd, sm[:, SO_BETA:SO_BETA + h_n], sm[:, SO_BETA + h_n:SO_BETA + 2 * h_n])
    g = jnp.where(fwd, sm[:, SO_G:SO_G + h_n], sm[:, SO_G + h_n:SO_G + 2 * h_n])
    gc = _dot_exact_lhs(incl.astype(BF16), g)
    gtot = jnp.sum(g, axis=0, keepdims=True)
    scale = DN_DK ** -0.5

    heads = range(h_n)
    lanes = [slice(h * DN_DK, (h + 1) * DN_DK) for h in heads]
    qs = [q_ref[:, lanes[h]] * scale for h in heads]
    ks = [k_ref[:, lanes[h]] for h in heads]
    vs = [v_ref[:, lanes[h]] for h in heads]
    bcols = [beta[:, h:h + 1] for h in heads]
    gcols = [gc[:, h:h + 1] for h in heads]
    gts = [gtot[:, h:h + 1] for h in heads]
    decs = [jnp.exp(jnp.where(incl, gcols[h] - _col_to_row(gcols[h], eye), 0.0)) for h in heads]
    kbs = [ks[h] * bcols[h] for h in heads]
    a_s = [_dot_nt(jnp.concatenate([kbs[h], qs[h]], axis=0), ks[h]) for h in heads]
    atts = [jnp.where(incl, a_s[h][c:] * decs[h], 0.0) for h in heads]
    tinvs = _inv_unit_lower([-jnp.where(strict, a_s[h][:c] * decs[h], 0.0) for h in heads], c)
    egs = [jnp.exp(gcols[h]) for h in heads]
    uws = [_dot3(tinvs[h], jnp.concatenate([vs[h] * bcols[h], kbs[h] * egs[h]], axis=1))
           for h in heads]
    s_olds = [s_scr[h] for h in heads]
    wss = [_dot(jnp.concatenate([uws[h][:, DN_DK:], qs[h] * egs[h]], axis=0), s_olds[h])
           for h in heads]
    v_news = [uws[h][:, :DN_DK] - wss[h][:c] for h in heads]
    outs = [wss[h][c:] + _dot(atts[h], v_news[h]) for h in heads]
    upds = [_dot_tn(ks[h] * jnp.exp(gts[h] - gcols[h]), v_news[h]) for h in heads]
    for h in heads:
        s_scr[h] = s_olds[h] * jnp.exp(gts[h]) + upds[h]
        o_ref[0, :, lanes[h]] = outs[h]


def _dn_scan(qkv, small, *, n_batch, seg, ctx_len):
    ntok = qkv.shape[0]
    c = DN_CHUNK
    n_all = seg // c
    n_ctx = ctx_len // c

    def blk(col):
        return lambda b, d, s: (b * n_all + _chunk_of_step(d, s, n_ctx, n_all), col)

    return pl.pallas_call(
        _dn_scan_kernel,
        grid=(n_batch, 2, n_all),
        in_specs=[pl.BlockSpec((c, DN_QK), blk(0)),
                  pl.BlockSpec((c, DN_QK), blk(1)),
                  pl.BlockSpec((c, DN_V), blk(2)),
                  pl.BlockSpec((c, LANES), blk(0))],
        out_specs=pl.BlockSpec(
            (1, c, DN_V), lambda b, d, s: (d, b * n_all + _chunk_of_step(d, s, n_ctx, n_all), 0)),
        out_shape=jax.ShapeDtypeStruct((2, ntok, DN_V), F32),
        scratch_shapes=[pltpu.VMEM((DN_HEADS, DN_DK, DN_DK), F32)],
        compiler_params=_cparams(("arbitrary", "arbitrary", "arbitrary")),
        name="dn_scan",
    )(qkv, qkv, qkv, small)


def _dn_readout_kernel(o_ref, gate_ref, g_ref, y_ref):
    o = o_ref[0] + o_ref[1]
    tm = o.shape[0]
    gate = gate_ref[...]
    outs = []
    for h in range(DN_HEADS):
        lanes = slice(h * DN_DK, (h + 1) * DN_DK)
        oh = o[:, lanes]
        yh = oh * lax.rsqrt(jnp.mean(oh * oh, axis=-1, keepdims=True) + NORM_EPS) * g_ref[...]
        outs.append(yh * _silu(gate[:, lanes]))
    y_ref[...] = jnp.concatenate(outs, axis=1).astype(y_ref.dtype)


def _dn_readout(o2, p, norm_g, *, col_gate):
    ntok = p.shape[0]
    tm = _tile(ntok, 512)
    return pl.pallas_call(
        _dn_readout_kernel,
        grid=(ntok // tm,),
        in_specs=[pl.BlockSpec((2, tm, DN_V), lambda i: (0, i, 0)),
                  pl.BlockSpec((tm, DN_V), lambda i: (i, col_gate // DN_V)),
                  pl.BlockSpec((1, DN_DK), lambda i: (0, 0))],
        out_specs=pl.BlockSpec((tm, DN_V), lambda i: (i, 0)),
        out_shape=jax.ShapeDtypeStruct((ntok, DN_V), BF16),
        compiler_params=_cparams(("arbitrary",)),
        name="dn_readout",
    )(o2, p, norm_g.reshape(1, DN_DK))


def _head_expander(n_heads, head_dim):
    hi = lax.broadcasted_iota(jnp.int32, (n_heads, n_heads * head_dim), 0)
    li = lax.broadcasted_iota(jnp.int32, (n_heads, n_heads * head_dim), 1)
    lo = hi * head_dim
    return jnp.where((li >= lo) & (li < lo + head_dim), 1.0, 0.0).astype(BF16)


def _ssd_scan_kernel(x_ref, b_ref, c_ref, sm_ref, y_ref, s_scr):
    d = pl.program_id(1)
    c = SSM_CHUNK
    hn = SSM_HEADS
    hd = SSM_HEADDIM
    per_group = hn // SSM_GROUPS
    gw = per_group * hd

    @pl.when(pl.program_id(2) == 0)
    def _():
        s_scr[...] = jnp.zeros(s_scr.shape, F32)

    incl, _, eye = _order_masks(d, c)
    sm = sm_ref[...]
    fwd = d == 0
    dt = jnp.where(fwd, sm[:, SO_DT:SO_DT + hn], sm[:, SO_DT + hn:SO_DT + 2 * hn])
    a = jnp.where(fwd, sm[:, SO_A:SO_A + hn], sm[:, SO_A + hn:SO_A + 2 * hn])
    acum = _dot_exact_lhs(incl.astype(BF16), a)
    expand = _head_expander(hn, hd)
    acum_x = _dot_exact_rhs(acum, expand)
    atot_x = jnp.sum(_dot_exact_rhs(a, expand), axis=0, keepdims=True)
    xdt = x_ref[...] * _dot_exact_rhs(dt, expand)
    xdec = xdt * jnp.exp(atot_x - acum_x)
    lane_lo = lax.broadcasted_iota(jnp.int32, (c, 2 * hd), 1) < hd

    groups = range(SSM_GROUPS)
    bgs = [b_ref[:, g * SSM_STATE:(g + 1) * SSM_STATE] for g in groups]
    cgs = [c_ref[:, g * SSM_STATE:(g + 1) * SSM_STATE] for g in groups]
    cbs = [_dot_nt(cgs[g], bgs[g]) for g in groups]
    ms = []
    for h in range(hn):
        acol = acum[:, h:h + 1]
        lmat = jnp.exp(jnp.where(incl, acol - _col_to_row(acol, eye), 0.0))
        ms.append(jnp.where(incl, cbs[h // per_group] * lmat, 0.0))
    ydiag = []
    for pair in range(hn // 2):
        xp = xdt[:, pair * 2 * hd:(pair + 1) * 2 * hd]
        ydiag.append(jnp.where(lane_lo, _dot(ms[2 * pair], xp), _dot(ms[2 * pair + 1], xp)))
    s_olds = [s_scr[g] for g in groups]
    yoffs = [_dot(cgs[g], s_olds[g]) for g in groups]
    sts = [_dot_tn(bgs[g], xdec[:, g * gw:(g + 1) * gw]) for g in groups]
    eacum = jnp.exp(acum_x)
    etot = jnp.exp(atot_x)
    for g in groups:
        lanes = slice(g * gw, (g + 1) * gw)
        s_scr[g] = s_olds[g] * etot[:, lanes] + sts[g]
        yd = jnp.concatenate(ydiag[g * per_group // 2:(g + 1) * per_group // 2], axis=1)
        y_ref[0, :, lanes] = yd + yoffs[g] * eacum[:, lanes]


def _ssd_scan(xbc, small, *, n_batch, seg, ctx_len):
    ntok = xbc.shape[0]
    c = SSM_CHUNK
    n_all = seg // c
    n_ctx = ctx_len // c
    gn = SSM_GROUPS * SSM_STATE

    def blk(col):
        return lambda b, d, s: (b * n_all + _chunk_of_step(d, s, n_ctx, n_all), col)

    return pl.pallas_call(
        _ssd_scan_kernel,
        grid=(n_batch, 2, n_all),
        in_specs=[pl.BlockSpec((c, SSM_INNER), blk(0)),
                  pl.BlockSpec((c, gn), blk(SSM_INNER // gn)),
                  pl.BlockSpec((c, gn), blk(SSM_INNER // gn + 1)),
                  pl.BlockSpec((c, LANES), blk(0))],
        out_specs=pl.BlockSpec(
            (1, c, SSM_INNER),
            lambda b, d, s: (d, b * n_all + _chunk_of_step(d, s, n_ctx, n_all), 0)),
        out_shape=jax.ShapeDtypeStruct((2, ntok, SSM_INNER), F32),
        scratch_shapes=[pltpu.VMEM((SSM_GROUPS, SSM_STATE, SSM_INNER // SSM_GROUPS), F32)],
        compiler_params=_cparams(("arbitrary", "arbitrary", "arbitrary")),
        name="ssd_scan",
    )(xbc, xbc, xbc, small)


def _ssd_readout_kernel(y_ref, xs_ref, z_ref, d_ref, g_ref, o_ref):
    y = (y_ref[0] + y_ref[1] + d_ref[...] * xs_ref[...]) * _silu(z_ref[...])
    gw = SSM_INNER // SSM_GROUPS
    outs = []
    for g in range(SSM_GROUPS):
        lanes = slice(g * gw, (g + 1) * gw)
        yg = y[:, lanes]
        outs.append(yg * lax.rsqrt(jnp.mean(yg * yg, axis=-1, keepdims=True) + SSM_NORM_EPS)
                    * g_ref[:, lanes])
    o_ref[...] = jnp.concatenate(outs, axis=1).astype(o_ref.dtype)


def _ssd_readout(y2, xbc, p, d_skip, norm_g, *, col_z):
    ntok = p.shape[0]
    tm = _tile(ntok, 512)
    w = SSM_INNER
    return pl.pallas_call(
        _ssd_readout_kernel,
        grid=(ntok // tm,),
        in_specs=[pl.BlockSpec((2, tm, w), lambda i: (0, i, 0)),
                  pl.BlockSpec((tm, w), lambda i: (i, 0)),
                  pl.BlockSpec((tm, w), lambda i: (i, col_z // w)),
                  pl.BlockSpec((1, w), lambda i: (0, 0)),
                  pl.BlockSpec((1, w), lambda i: (0, 0))],
        out_specs=pl.BlockSpec((tm, w), lambda i: (i, 0)),
        out_shape=jax.ShapeDtypeStruct((ntok, w), BF16),
        compiler_params=_cparams(("arbitrary",)),
        name="ssd_readout",
    )(y2, xbc, p, jnp.repeat(d_skip, SSM_HEADDIM).reshape(1, w), norm_g.reshape(1, w))


def _rw_shift_kernel(x_ref, mu_ref, o_ref, scr, *, seg, ctx_len):
    x = x_ref[...]
    n_lat = seg - ctx_len
    _fill_padded(scr, x, seg, ctx_len)
    ctx0, lat0, _ = _conv_rows(seg, ctx_len)
    mu = mu_ref[...]
    lane_c = lax.broadcasted_iota(jnp.int32, (ctx_len, LANES), 1)
    prev = scr[0, pl.ds(ctx0 - 1, ctx_len), :]
    nxt = scr[0, pl.ds(ctx0 + 1, ctx_len), :]
    xc = x[:ctx_len]
    o_ref[pl.ds(0, ctx_len), :] = xc + (jnp.where(lane_c % 2 == 0, prev, nxt) - xc) * mu
    sel = lax.broadcasted_iota(jnp.int32, (n_lat, LANES), 1) % 4
    left = scr[1, pl.ds(lat0 - 1, n_lat), :]
    right = scr[2, pl.ds(lat0 + 1, n_lat), :]
    up = scr[0, pl.ds(lat0 - GRID_W, n_lat), :]
    down = scr[0, pl.ds(lat0 + GRID_W, n_lat), :]
    shifted = jnp.where(sel == 0, left, jnp.where(sel == 1, right, jnp.where(sel == 2, up, down)))
    xl = x[ctx_len:]
    o_ref[pl.ds(ctx_len, n_lat), :] = xl + (shifted - xl) * mu


def _rw_shift(p, mu_packed, *, n_batch, seg, ctx_len, col0):
    ntok = p.shape[0]
    cw = mu_packed.shape[-1]
    kern = functools.partial(_rw_shift_kernel, seg=seg, ctx_len=ctx_len)
    return pl.pallas_call(
        kern,
        grid=(n_batch, cw // LANES),
        in_specs=[pl.BlockSpec((seg, LANES), lambda b, j: (b, col0 // LANES + j)),
                  pl.BlockSpec((1, LANES), lambda b, j: (0, j))],
        out_specs=pl.BlockSpec((seg, LANES), lambda b, j: (b, j)),
        out_shape=jax.ShapeDtypeStruct((ntok, cw), F32),
        scratch_shapes=[pltpu.VMEM((3, _conv_rows(seg, ctx_len)[2], LANES), F32)],
        compiler_params=_cparams(("arbitrary", "arbitrary")),
        name="rw_shift",
    )(p, mu_packed.reshape(1, cw))


def _head_sum(x, n_heads, head_dim):
    expand = _head_expander(n_heads, head_dim)
    hi = lax.broadcasted_iota(jnp.int32, (n_heads * head_dim, n_heads), 1)
    li = lax.broadcasted_iota(jnp.int32, (n_heads * head_dim, n_heads), 0)
    lo = hi * head_dim
    reduce = jnp.where((li >= lo) & (li < lo + head_dim), 1.0, 0.0).astype(BF16)
    return _dot_exact_rhs(_dot_exact_rhs(x, reduce), expand)


def _rw_prep_kernel(x_ref, w0_ref, w2_ref, a0_ref, a2_ref, g2_ref, kk_ref, ka_ref, rk_ref,
                    r_ref, v_ref, nk_ref, gate_ref, bonus_ref, lw_ref, kd_ref, b_ref):
    w = RW_WIDTH
    r = x_ref[:, 0:w]
    k = x_ref[:, w:2 * w]
    v = x_ref[:, 2 * w:3 * w]
    lora = 3 * w
    wl = x_ref[:, lora + RWL_W:lora + RWL_W + 256]
    al = x_ref[:, lora + RWL_A:lora + RWL_A + 256]
    gl = x_ref[:, lora + RWL_G:lora + RWL_G + RW_GATE_LORA]
    w_raw = w0_ref[...] + _dot(jnp.tanh(wl), w2_ref[...])
    log_decay = -jnp.exp(-_softplus(-w_raw) - 0.5)
    icl = _sigmoid(a0_ref[...] + _dot(al, a2_ref[...]))
    gate_ref[...] = _dot(_sigmoid(gl), g2_ref[...])
    kx = k * kk_ref[...]
    nk = kx * lax.rsqrt(_head_sum(kx * kx, RW_HEADS, RW_HEAD) + L2_EPS)
    r_ref[...] = r
    v_ref[...] = v
    nk_ref[...] = nk
    ksum = None
    for d in range(2):
        icl_d = icl[:, d * w:(d + 1) * w]
        k_d = k * (1.0 + (icl_d - 1.0) * ka_ref[...])
        lw_ref[d] = log_decay[:, d * w:(d + 1) * w]
        kd_ref[d] = k_d
        b_ref[d] = nk * icl_d
        ksum = k_d if ksum is None else ksum + k_d
    bonus_ref[...] = _head_sum(r * ksum * rk_ref[...], RW_HEADS, RW_HEAD) * v


def _pack_rw_vec(vec):
    w3 = 3 * RW_WIDTH
    z = lambda n: jnp.zeros((n,), vec.dtype)
    l0 = w3 + 2 * RW_DECAY_LORA
    l1 = l0 + 2 * RW_A_LORA
    return jnp.concatenate([vec[:w3], vec[w3:l0], z(64), vec[l0:l1], z(64), vec[l1:], z(256)])


def _pack_lora(m2):
    lora, w = m2.shape[1], m2.shape[2]
    out = jnp.zeros((256, 2 * w), F32)
    out = out.at[0:lora, 0:w].set(m2[0])
    out = out.at[lora:2 * lora, w:2 * w].set(m2[1])
    return out.astype(BF16)


def _rw_prep(xs, w0, w2, a0, a2, g2, k_k, k_a, r_k):
    ntok = xs.shape[0]
    w = RW_WIDTH
    tm = _tile(ntok, 256)
    row = lambda a: a.reshape(1, -1)
    full = lambda shape: pl.BlockSpec(shape, lambda i: (0,) * len(shape))
    tok = pl.BlockSpec((tm, w), lambda i: (i, 0))
    tok2 = pl.BlockSpec((2, tm, w), lambda i: (0, i, 0))
    sd = jax.ShapeDtypeStruct
    return pl.pallas_call(
        _rw_prep_kernel,
        grid=(ntok // tm,),
        in_specs=[pl.BlockSpec((tm, 4 * w), lambda i: (i, 0)),
                  full((1, 2 * w)), full((256, 2 * w)), full((1, 2 * w)), full((256, 2 * w)),
                  full((RW_GATE_LORA, w)), full((1, w)), full((1, w)), full((1, w))],
        out_specs=[tok, tok, tok, tok, tok, tok2, tok2, tok2],
        out_shape=[sd((ntok, w), F32)] * 5 + [sd((2, ntok, w), F32)] * 3,
        compiler_params=_cparams(("arbitrary",)),
        name="rw_prep",
    )(xs, row(w0), _pack_lora(w2), row(a0), _pack_lora(a2), g2.astype(BF16), row(k_k), row(k_a),
      row(r_k))


def _stack_heads(x, lane_lo):
    return jnp.concatenate([jnp.where(lane_lo, x, 0.0), jnp.where(lane_lo, 0.0, x)], axis=0)


def _rw_scan_kernel(r_ref, v_ref, nk_ref, lw_ref, kd_ref, b_ref, y_ref, h_scr):
    d = pl.program_id(1)
    c = RW_CHUNK
    pw = 2 * RW_HEAD
    n_pairs = RW_HEADS // 2

    @pl.when(pl.program_id(2) == 0)
    def _():
        h_scr[...] = jnp.zeros(h_scr.shape, F32)

    incl, _, _ = _order_masks(d, c)
    lw = lw_ref[0]
    cw = _dot_exact_lhs(incl.astype(BF16), lw)
    tot = jnp.sum(lw, axis=0, keepdims=True)
    e_in = jnp.exp(cw)
    e_neg = jnp.exp(-cw)
    e_rem = jnp.exp(tot - cw)
    e_tot = jnp.exp(tot)
    b = b_ref[0]
    kd = kd_ref[0]
    a_t = -nk_ref[...] * jnp.exp(cw - lw)
    r_t = r_ref[...] * e_in
    b_t = b * e_neg
    k_t = kd * e_neg
    b_h = b * e_rem
    k_h = kd * e_rem
    v = v_ref[...]

    ri = lax.broadcasted_iota(jnp.int32, (2 * c, 2 * c), 0) % c
    ci = lax.broadcasted_iota(jnp.int32, (2 * c, 2 * c), 1) % c
    delta = (ri - ci) * (1 - 2 * d)
    incl2 = delta >= 0
    strict2 = delta > 0
    rj = lax.broadcasted_iota(jnp.int32, (pw, pw), 0)
    cj = lax.broadcasted_iota(jnp.int32, (pw, pw), 1)
    eye_p = rj == cj
    lane_lo = lax.broadcasted_iota(jnp.int32, (c, pw), 1) < RW_HEAD

    pairs = range(n_pairs)
    ln = [slice(p * pw, (p + 1) * pw) for p in pairs]
    xs = [jnp.concatenate([_stack_heads(a_t[:, ln[p]], lane_lo), _stack_heads(r_t[:, ln[p]], lane_lo)],
                          axis=0) for p in pairs]
    ys = [jnp.concatenate([_stack_heads(b_t[:, ln[p]], lane_lo), _stack_heads(k_t[:, ln[p]], lane_lo)],
                          axis=0) for p in pairs]
    vst = [_stack_heads(v[:, ln[p]], lane_lo) for p in pairs]
    gs = [_dot_nt(xs[p], ys[p]) for p in pairs]
    tinvs = _inv_unit_lower([jnp.where(strict2, gs[p][:2 * c, :2 * c], 0.0) for p in pairs], c)
    h_olds = [h_scr[p] for p in pairs]
    xh = [_dot(xs[p], h_olds[p]) for p in pairs]
    rhs = [xh[p][:2 * c] + _dot(jnp.where(strict2, gs[p][:2 * c, 2 * c:], 0.0), vst[p])
           for p in pairs]
    us = [_dot3(tinvs[p], rhs[p]) for p in pairs]
    uv = [jnp.concatenate([us[p], vst[p]], axis=0) for p in pairs]
    mask_y = jnp.concatenate([incl2, incl2], axis=1)
    yst = [xh[p][2 * c:] + _dot(jnp.where(mask_y, gs[p][2 * c:, :], 0.0), uv[p]) for p in pairs]
    upd = [_dot_tn(jnp.concatenate([_stack_heads(b_h[:, ln[p]], lane_lo),
                                    _stack_heads(k_h[:, ln[p]], lane_lo)], axis=0), uv[p])
           for p in pairs]
    for p in pairs:
        e_col = jnp.sum(jnp.where(eye_p, e_tot[:, ln[p]], 0.0), axis=1, keepdims=True)
        h_scr[p] = h_olds[p] * e_col + upd[p]
        y_ref[0, :, ln[p]] = yst[p][:c] + yst[p][c:]


def _rw_scan(r, v, nk, lw, kd, b, *, n_batch, seg, ctx_len):
    ntok = r.shape[0]
    c = RW_CHUNK
    w = RW_WIDTH
    n_all = seg // c
    n_ctx = ctx_len // c
    row = lambda bb, d, s: bb * n_all + _chunk_of_step(d, s, n_ctx, n_all)
    tok = pl.BlockSpec((c, w), lambda bb, d, s: (row(bb, d, s), 0))
    tok2 = pl.BlockSpec((1, c, w), lambda bb, d, s: (d, row(bb, d, s), 0))
    return pl.pallas_call(
        _rw_scan_kernel,
        grid=(n_batch, 2, n_all),
        in_specs=[tok, tok, tok, tok2, tok2, tok2],
        out_specs=tok2,
        out_shape=jax.ShapeDtypeStruct((2, ntok, w), F32),
        scratch_shapes=[pltpu.VMEM((RW_HEADS // 2, 2 * RW_HEAD, 2 * RW_HEAD), F32)],
        compiler_params=_cparams(("arbitrary", "arbitrary", "arbitrary")),
        name="rw_scan",
    )(r, v, nk, lw, kd, b)


def _rw_readout_kernel(y_ref, bonus_ref, gate_ref, g_ref, b_ref, o_ref):
    y = y_ref[0] + y_ref[1]
    inv_n = 1.0 / RW_HEAD
    mean = _head_sum(y, RW_HEADS, RW_HEAD) * inv_n
    yc = y - mean
    var = _head_sum(yc * yc, RW_HEADS, RW_HEAD) * inv_n
    yn = yc * lax.rsqrt(var + RW_GN_EPS) * g_ref[...] + b_ref[...]
    o_ref[...] = ((yn + bonus_ref[...]) * gate_ref[...]).astype(o_ref.dtype)


def _rw_readout(y2, bonus, gate, ln_g, ln_b):
    ntok = bonus.shape[0]
    w = RW_WIDTH
    tm = _tile(ntok, 512)
    tok = pl.BlockSpec((tm, w), lambda i: (i, 0))
    vec = pl.BlockSpec((1, w), lambda i: (0, 0))
    return pl.pallas_call(
        _rw_readout_kernel,
        grid=(ntok // tm,),
        in_specs=[pl.BlockSpec((2, tm, w), lambda i: (0, i, 0)), tok, tok, vec, vec],
        out_specs=tok,
        out_shape=jax.ShapeDtypeStruct((ntok, w), BF16),
        compiler_params=_cparams(("arbitrary",)),
        name="rw_readout",
    )(y2, bonus, gate, ln_g.reshape(1, w), ln_b.reshape(1, w))


def _branch_mix_kernel(oa_ref, ob_ref, oc_ref, ga_ref, gb_ref, gc_ref, w_ref, y_ref):
    acc = None
    for i, (o_ref, g_ref) in enumerate(((oa_ref, ga_ref), (ob_ref, gb_ref), (oc_ref, gc_ref))):
        term = _sigmoid(g_ref[...]) * jnp.dot(o_ref[...], w_ref[i], preferred_element_type=F32)
        acc = term if acc is None else acc + term
    y_ref[...] = acc.astype(y_ref.dtype)


def _branch_mix(oa, ob, oc, p, w_branch_bf16, *, d_model):
    ntok = p.shape[0]
    bw = oa.shape[1]
    tm = _tile(ntok, 512)
    tn = 512
    nb = d_model // tn
    o_spec = pl.BlockSpec((tm, bw), lambda i, n: (i, 0))
    g_spec = lambda br: pl.BlockSpec((tm, tn), lambda i, n: (i, br * nb + n))
    return pl.pallas_call(
        _branch_mix_kernel,
        grid=(ntok // tm, nb),
        in_specs=[o_spec, o_spec, o_spec, g_spec(0), g_spec(1), g_spec(2),
                  pl.BlockSpec((3, bw, tn), lambda i, n: (0, 0, n))],
        out_specs=pl.BlockSpec((tm, tn), lambda i, n: (i, n)),
        out_shape=jax.ShapeDtypeStruct((ntok, d_model), BF16),
        compiler_params=_cparams(("arbitrary", "arbitrary")),
        name="branch_mix",
    )(oa, ob, oc, p, p, p, w_branch_bf16)


ROUTE_TILE = 256


def _dot3_nt(a, b):
    ah, al = _split2(a)
    bh, bl = _split2(b)
    d = functools.partial(lax.dot_general, dimension_numbers=(((1,), (1,)), ((), ())),
                          preferred_element_type=F32)
    return d(ah, bh) + d(ah, bl) + d(al, bh)


def _outproj_router_kernel(y_ref, x_ref, ml_ref, mc_ref, wo_ref, g_ref, wr_ref, br_ref,
                           xo_ref, f_ref, idx_ref, wgt_ref, rank_ref, cnt_ref, run_scr,
                           *, tm, seg, ctx_len):
    i = pl.program_id(0)

    @pl.when(i == 0)
    def _():
        run_scr[...] = jnp.zeros(run_scr.shape, F32)

    is_ctx = _ctx_rows(i, tm, seg, ctx_len)
    ml = ml_ref[0]
    mc = mc_ref[0]
    gate = jnp.where(is_ctx, mc[2:3], ml[2:3])
    x_new = x_ref[...] + gate * jnp.dot(y_ref[...], wo_ref[...], preferred_element_type=F32)
    xo_ref[...] = x_new
    f = _modulated_norm(x_new, g_ref[...], ml, mc, is_ctx, 3, 4)
    f_ref[...] = f

    ne = N_EXPERTS
    logits = _dot3_nt(wr_ref[...], f) + br_ref[...]
    eidx = lax.broadcasted_iota(jnp.int32, (ne, tm), 0)
    vals = logits
    picks = []
    tops = []
    hots = []
    for _ in range(TOP_K):
        m = jnp.max(vals, axis=0, keepdims=True)
        pick = jnp.min(jnp.where(vals == m, eidx, ne), axis=0, keepdims=True)
        hot = eidx == pick
        vals = jnp.where(hot, -jnp.inf, vals)
        tops.append(m)
        picks.append(pick)
        hots.append(hot)
    es = [jnp.exp(t - tops[0]) for t in tops]
    denom = es[0] + es[1] + es[2] + es[3]
    hot_all = jnp.where(hots[0] | hots[1] | hots[2] | hots[3], 1.0, 0.0)
    si = lax.broadcasted_iota(jnp.int32, (tm, tm), 0)
    ti = lax.broadcasted_iota(jnp.int32, (tm, tm), 1)
    earlier = jnp.where(si < ti, 1.0, 0.0).astype(BF16)
    before = jnp.dot(hot_all.astype(BF16), earlier, preferred_element_type=F32)
    rank = run_scr[:, 0:1] + before
    run_new = run_scr[...] + jnp.sum(hot_all, axis=1, keepdims=True)
    run_scr[...] = run_new
    cnt_ref[...] = run_new
    idx_ref[...] = jnp.concatenate(picks, axis=0)
    wgt_ref[...] = jnp.concatenate([e / denom for e in es], axis=0)
    rank_ref[...] = jnp.concatenate(
        [jnp.sum(jnp.where(h, rank, 0.0), axis=0, keepdims=True) for h in hots],
        axis=0).astype(jnp.int32)


def _outproj_router(y, x, mod, w_out_bf16, g_ffn, w_router, b_router, *, n_batch, seg, ctx_len):
    ntok, d = x.shape
    tm = _tile(seg, ROUTE_TILE, LANES)
    per = seg // tm
    ne = N_EXPERTS
    kern = functools.partial(_outproj_router_kernel, tm=tm, seg=seg, ctx_len=ctx_len)
    tok = pl.BlockSpec((tm, d), lambda i: (i, 0))
    sel = pl.BlockSpec((TOP_K, tm), lambda i: (0, i))
    sd = jax.ShapeDtypeStruct
    return pl.pallas_call(
        kern,
        grid=(ntok // tm,),
        in_specs=[tok, tok,
                  pl.BlockSpec((1, 6, d), lambda i: (i // per, 0, 0)),
                  pl.BlockSpec((1, 6, d), lambda i: (n_batch, 0, 0)),
                  pl.BlockSpec((d, d), lambda i: (0, 0)),
                  pl.BlockSpec((1, d), lambda i: (0, 0)),
                  pl.BlockSpec((ne, d), lambda i: (0, 0)),
                  pl.BlockSpec((ne, 1), lambda i: (0, 0))],
        out_specs=[tok, tok, sel, sel, sel, pl.BlockSpec((ne, LANES), lambda i: (0, 0))],
        out_shape=[sd((ntok, d), F32), sd((ntok, d), F32), sd((TOP_K, ntok), jnp.int32),
                   sd((TOP_K, ntok), F32), sd((TOP_K, ntok), jnp.int32), sd((ne, LANES), F32)],
        scratch_shapes=[pltpu.VMEM((ne, LANES), F32)],
        compiler_params=_cparams(("arbitrary",)),
        name="outproj_router",
    )(y, x, mod, mod, w_out_bf16, g_ffn.reshape(1, d), w_router.T, b_router.reshape(ne, 1))


EXPERT_TILE = 256


def _dispatch_kernel(dest_ref, f_ref, xs_in_ref, xs_ref, sem, *, tm):
    del xs_in_ref

    def row_copy(j, k):
        dst = dest_ref[0, 0, k * tm + j]
        return pltpu.make_async_copy(f_ref.at[pl.ds(j, 1), :], xs_ref.at[pl.ds(dst, 1), :], sem)

    def issue(j, carry):
        for k in range(TOP_K):
            row_copy(j, k).start()
        return carry

    lax.fori_loop(0, tm, issue, 0)

    def drain(j, carry):
        for k in range(TOP_K):
            row_copy(j, k).wait()
        return carry

    lax.fori_loop(0, tm, drain, 0)


def _dispatch(f, dest_tiles, n_rows):
    ntok, d = f.shape
    tm = dest_tiles.shape[-1] // TOP_K
    kern = functools.partial(_dispatch_kernel, tm=tm)
    return pl.pallas_call(
        kern,
        grid=(ntok // tm,),
        in_specs=[pl.BlockSpec((1, 1, TOP_K * tm), lambda i: (i, 0, 0), memory_space=pltpu.SMEM),
                  pl.BlockSpec((tm, d), lambda i: (i, 0)),
                  pl.BlockSpec(memory_space=pl.ANY)],
        out_specs=pl.BlockSpec(memory_space=pl.ANY),
        out_shape=jax.ShapeDtypeStruct((n_rows, d), F32),
        scratch_shapes=[pltpu.SemaphoreType.DMA(())],
        input_output_aliases={2: 0},
        compiler_params=_cparams(("arbitrary",)),
        name="moe_dispatch",
    )(dest_tiles, f, jnp.zeros((n_rows, d), F32))


def _experts_kernel(te_ref, xs_ref, w1_ref, b1_ref, w2_ref, b2_ref, ys_ref):
    del te_ref
    hu = jnp.dot(xs_ref[...].astype(BF16), w1_ref[0], preferred_element_type=F32) + b1_ref[0]
    gt = jnp.minimum(hu[:, :D_EXPERT], SWIGLU_LIMIT)
    up = jnp.clip(hu[:, D_EXPERT:], -SWIGLU_LIMIT, SWIGLU_LIMIT)
    act = gt * _sigmoid(SWIGLU_ALPHA * gt) * (up + 1.0)
    ys_ref[...] = jnp.dot(act.astype(BF16), w2_ref[0], preferred_element_type=F32) + b2_ref[0]


def _experts(xs, tile_expert, w1_bf16, b1, w2_bf16, b2):
    n_rows, d = xs.shape
    ne, _, de2 = w1_bf16.shape
    tmx = EXPERT_TILE
    grid_spec = pltpu.PrefetchScalarGridSpec(
        num_scalar_prefetch=1,
        grid=(n_rows // tmx,),
        in_specs=[pl.BlockSpec((tmx, d), lambda i, te: (i, 0)),
                  pl.BlockSpec((1, d, de2), lambda i, te: (te[i], 0, 0)),
                  pl.BlockSpec((1, 1, de2), lambda i, te: (te[i], 0, 0)),
                  pl.BlockSpec((1, de2 // 2, d), lambda i, te: (te[i], 0, 0)),
                  pl.BlockSpec((1, 1, d), lambda i, te: (te[i], 0, 0))],
        out_specs=pl.BlockSpec((tmx, d), lambda i, te: (i, 0)),
    )
    return pl.pallas_call(
        _experts_kernel,
        grid_spec=grid_spec,
        out_shape=jax.ShapeDtypeStruct((n_rows, d), F32),
        compiler_params=_cparams(("arbitrary",)),
        name="moe_experts",
    )(tile_expert, xs, w1_bf16, b1.reshape(ne, 1, de2), w2_bf16, b2.reshape(ne, 1, d))


def _combine_kernel(dest_ref, x_ref, wgt_ref, ml_ref, mc_ref, ys_ref, xo_ref, rows_scr, sem,
                    *, tm, seg, ctx_len):
    def row_copy(j, k):
        src = dest_ref[0, 0, k * tm + j]
        return pltpu.make_async_copy(ys_ref.at[pl.ds(src, 1), :],
                                     rows_scr.at[k, pl.ds(j, 1), :], sem)

    def issue(j, carry):
        for k in range(TOP_K):
            row_copy(j, k).start()
        return carry

    lax.fori_loop(0, tm, issue, 0)

    def drain(j, carry):
        for k in range(TOP_K):
            row_copy(j, k).wait()
        return carry

    lax.fori_loop(0, tm, drain, 0)

    wgt = wgt_ref[...]
    y = None
    for k in range(TOP_K):
        term = wgt[:, k:k + 1] * rows_scr[k]
        y = term if y is None else y + term
    is_ctx = _ctx_rows(pl.program_id(0), tm, seg, ctx_len)
    gate = jnp.where(is_ctx, mc_ref[0][5:6], ml_ref[0][5:6])
    xo_ref[...] = x_ref[...] + gate * y


def _combine(ys, dest_tiles, x, wgt_t, mod, *, n_batch, seg, ctx_len):
    ntok, d = x.shape
    tm = dest_tiles.shape[-1] // TOP_K
    per = seg // tm
    kern = functools.partial(_combine_kernel, tm=tm, seg=seg, ctx_len=ctx_len)
    tok = pl.BlockSpec((tm, d), lambda i: (i, 0))
    return pl.pallas_call(
        kern,
        grid=(ntok // tm,),
        in_specs=[pl.BlockSpec((1, 1, TOP_K * tm), lambda i: (i, 0, 0), memory_space=pltpu.SMEM),
                  tok,
                  pl.BlockSpec((tm, TOP_K), lambda i: (i, 0)),
                  pl.BlockSpec((1, 6, d), lambda i: (i // per, 0, 0)),
                  pl.BlockSpec((1, 6, d), lambda i: (n_batch, 0, 0)),
                  pl.BlockSpec(memory_space=pl.ANY)],
        out_specs=tok,
        out_shape=jax.ShapeDtypeStruct((ntok, d), F32),
        scratch_shapes=[pltpu.VMEM((TOP_K, tm, d), F32), pltpu.SemaphoreType.DMA(())],
        compiler_params=_cparams(("arbitrary",)),
        name="moe_combine",
    )(dest_tiles, x, wgt_t, mod, mod, ys)


def _route_plan(idx, rank, counts, tm):
    tmx = EXPERT_TILE
    ntok = idx.shape[1]
    cnt = counts[:, 0].astype(jnp.int32)
    padded = ((cnt + tmx - 1) // tmx) * tmx
    ends = jnp.cumsum(padded)
    starts = ends - padded
    dest = starts[idx] + rank
    n_tiles = (TOP_K * ntok) // tmx + N_EXPERTS
    tile_start = jnp.arange(n_tiles, dtype=jnp.int32) * tmx
    tile_expert = jnp.minimum(jnp.searchsorted(ends, tile_start, side="right"),
                              N_EXPERTS - 1).astype(jnp.int32)
    nt = ntok // tm
    dest_tiles = dest.reshape(TOP_K, nt, tm).transpose(1, 0, 2).reshape(nt, 1, TOP_K * tm)
    return dest_tiles, tile_expert, n_tiles * tmx


def _final_norm_kernel(x_ref, g_ref, o_ref):
    x = x_ref[...]
    o_ref[0] = x * lax.rsqrt(jnp.mean(x * x, axis=-1, keepdims=True) + NORM_EPS) * g_ref[...]


def _final_norm(x, g, *, n_batch, seg, ctx_len):
    d = x.shape[1]
    n_lat = seg - ctx_len
    tm = _tile(math.gcd(ctx_len, n_lat), 512)
    per = n_lat // tm
    return pl.pallas_call(
        _final_norm_kernel,
        grid=(n_batch, per),
        in_specs=[pl.BlockSpec((tm, d), lambda b, j: (b * (seg // tm) + ctx_len // tm + j, 0)),
                  pl.BlockSpec((1, d), lambda b, j: (0, 0))],
        out_specs=pl.BlockSpec((1, tm, d), lambda b, j: (b, j, 0)),
        out_shape=jax.ShapeDtypeStruct((n_batch, n_lat, d), F32),
        compiler_params=_cparams(("arbitrary", "arbitrary")),
        name="final_norm",
    )(x, g.reshape(1, d))


def kernel(x, c, ctx, c_ctx, norm_mix_g, norm_ffn_g, w_mod, b_mod, w_in, dn_conv, dn_a_log, dn_dt_bias, dn_norm_g, rw_mu, rw_w0, rw_w2, rw_a0, rw_a2, rw_g2, rw_k_k, rw_k_a, rw_r_k, rw_ln_g, rw_ln_b, ssm_conv, ssm_conv_b, ssm_a_log, ssm_dt_bias, ssm_d, ssm_norm_g, w_branch, w_out, w_router, b_router, w_e1, b_e1, w_e2, b_e2, final_norm_g):
    n_batch, seq, d = x.shape
    ctx_len = ctx.shape[1]
    seg = seq + ctx_len
    n_layers = w_in.shape[0]
    assert seq % GRID_W == 0 and ctx_len % SSM_CHUNK == 0 and seq % SSM_CHUNK == 0
    off = _p_layout(d)
    dims = dict(n_batch=n_batch, seg=seg, ctx_len=ctx_len)

    mod_rows = -(-(n_batch + 1) // 8) * 8
    cond = jnp.concatenate([c, c_ctx[None], jnp.zeros((mod_rows - n_batch - 1, d), F32)], axis=0)
    mod_all = _modulation(cond, w_mod, b_mod).reshape(n_layers, mod_rows, 6, d)

    xa = jnp.concatenate([ctx, x], axis=1).reshape(n_batch * seg, d)
    route_tile = _tile(seg, ROUTE_TILE, LANES)
    for i in range(n_layers):
        mod = mod_all[i]
        p = _in_projection(xa, norm_mix_g[i], mod, _pack_w_in(w_in[i], d), **dims)
        small = _small_prep(p, dn_a_log[i], dn_dt_bias[i], ssm_a_log[i], ssm_dt_bias[i],
                            col0=off["small"])

        qkv = _dn_conv(p, dn_conv[i], col0=off["dn_qkv"], **dims)
        oa = _dn_readout(_dn_scan(qkv, small, **dims), p, dn_norm_g[i], col_gate=off["dn_gate"])

        shifted = _rw_shift(p, _pack_rw_vec(rw_mu[i]), col0=off["rw_rkv"], **dims)
        r, v, nk, gate, bonus, lw, kd, b = _rw_prep(
            shifted, rw_w0[i], rw_w2[i], rw_a0[i], rw_a2[i], rw_g2[i], rw_k_k[i], rw_k_a[i],
            rw_r_k[i])
        ob = _rw_readout(_rw_scan(r, v, nk, lw, kd, b, **dims), bonus, gate, rw_ln_g[i], rw_ln_b[i])

        xbc = _conv_silu(p, ssm_conv[i], ssm_conv_b[i], col0=off["ssm_xbc"], n_norm_blocks=0,
                         name="ssm_conv", **dims)
        oc = _ssd_readout(_ssd_scan(xbc, small, **dims), xbc, p, ssm_d[i], ssm_norm_g[i],
                          col_z=off["ssm_z"])

        y = _branch_mix(oa, ob, oc, p, w_branch[i].astype(BF16), d_model=d)
        xa, f, idx, wgt, rank, counts = _outproj_router(
            y, xa, mod, w_out[i].astype(BF16), norm_ffn_g[i], w_router[i], b_router[i], **dims)
        dest_tiles, tile_expert, n_rows = _route_plan(idx, rank, counts, route_tile)
        xs = _dispatch(f, dest_tiles, n_rows)
        ys = _experts(xs, tile_expert, w_e1[i].astype(BF16), b_e1[i], w_e2[i].astype(BF16), b_e2[i])
        xa = _combine(ys, dest_tiles, xa, wgt.T, mod, **dims)
    return _final_norm(xa, final_norm_g, **dims)
```

```python
import functools
import math

import jax
import jax.numpy as jnp
from jax import lax
from jax.experimental import pallas as pl
from jax.experimental.pallas import tpu as pltpu

F32 = jnp.float32
BF16 = jnp.bfloat16

GRID_W = 64
NORM_EPS = 1e-6
L2_EPS = 1e-6

DN_HEADS = 8
DN_DK = 128
DN_CHUNK = 64
DN_QK = DN_HEADS * DN_DK
DN_V = DN_QK

RW_HEADS = 16
RW_HEAD = 64
RW_WIDTH = RW_HEADS * RW_HEAD
RW_DECAY_LORA = 96
RW_A_LORA = 96
RW_GATE_LORA = 256
RW_GN_EPS = 64e-5
RW_CHUNK = 64

SSM_HEADS = 16
SSM_HEADDIM = 64
SSM_INNER = SSM_HEADS * SSM_HEADDIM
SSM_GROUPS = 4
SSM_STATE = 128
SSM_CHUNK = 128
SSM_XBC = SSM_INNER + 2 * SSM_GROUPS * SSM_STATE
SSM_NORM_EPS = 1e-5

N_EXPERTS = 32
TOP_K = 4
D_EXPERT = 512
SWIGLU_LIMIT = 7.0
SWIGLU_ALPHA = 1.702

V7X_VMEM_LIMIT = 56 * 1024 * 1024
LANES = 128

SCAN_OUT_DTYPE = jnp.bfloat16

IN_TILE_N = 512


def _p_layout(d_model):
    off = {}
    pos = 0
    for name, width in (("gate", 3 * d_model), ("dn_qkv", 2 * DN_QK + DN_V), ("dn_gate", DN_V),
                        ("rw_rkv", 3 * RW_WIDTH), ("rw_lora", 1024), ("ssm_z", SSM_INNER),
                        ("ssm_xbc", SSM_XBC), ("small", IN_TILE_N)):
        off[name] = pos
        pos += width
    off["main"] = off["small"]
    off["total"] = pos
    return off


RWL_W, RWL_A, RWL_G = 0, 256, 512
SM_BETA, SM_A, SM_DT = 0, 16, 32


def _cparams(sem):
    return pltpu.CompilerParams(dimension_semantics=sem, vmem_limit_bytes=V7X_VMEM_LIMIT)


def _tile(n, cap, mult=8):
    best = None
    for t in range(mult, min(n, cap) + 1, mult):
        if n % t == 0:
            best = t
    assert best is not None, (n, cap)
    return best


def _dot(a, b):
    return jnp.dot(a.astype(BF16), b.astype(BF16), preferred_element_type=F32)


def _dot_nt(a, b):
    return lax.dot_general(a.astype(BF16), b.astype(BF16), (((1,), (1,)), ((), ())),
                           preferred_element_type=F32)


def _dot_tn(a, b):
    return lax.dot_general(a.astype(BF16), b.astype(BF16), (((0,), (0,)), ((), ())),
                           preferred_element_type=F32)


def _split2(a):
    hi = a.astype(BF16)
    lo = (a - hi.astype(F32)).astype(BF16)
    return hi, lo


def _dot3(a, b):
    ah, al = _split2(a)
    bh, bl = _split2(b)
    d = functools.partial(jnp.dot, preferred_element_type=F32)
    return d(ah, bh) + d(ah, bl) + d(al, bh)


def _dot_exact_lhs(m_bf16, x):
    x1 = x.astype(BF16)
    r1 = x - x1.astype(F32)
    x2 = r1.astype(BF16)
    x3 = (r1 - x2.astype(F32)).astype(BF16)
    d = functools.partial(jnp.dot, preferred_element_type=F32)
    return d(m_bf16, x1) + d(m_bf16, x2) + d(m_bf16, x3)


def _dot_exact_rhs(x, m_bf16):
    x1 = x.astype(BF16)
    r1 = x - x1.astype(F32)
    x2 = r1.astype(BF16)
    x3 = (r1 - x2.astype(F32)).astype(BF16)
    d = functools.partial(jnp.dot, preferred_element_type=F32)
    return d(x1, m_bf16) + d(x2, m_bf16) + d(x3, m_bf16)


def _sigmoid(x):
    return 1.0 / (1.0 + jnp.exp(-x))


def _silu(x):
    return x * _sigmoid(x)


def _softplus(x):
    return jnp.maximum(x, 0.0) + jnp.log(1.0 + jnp.exp(-jnp.abs(x)))


def _inv_unit_lower(neg_lowers, size):
    shape = neg_lowers[0].shape
    ri = lax.broadcasted_iota(jnp.int32, shape, 0)
    ci = lax.broadcasted_iota(jnp.int32, shape, 1)
    eye = jnp.where(ri == ci, 1.0, 0.0)
    ps = [eye + n for n in neg_lowers]
    qs = list(neg_lowers)
    power = 2
    while power < size:
        qs = [_dot(q, q) for q in qs]
        ps = [p + _dot(p, q) for p, q in zip(ps, qs)]
        power *= 2
    return ps


def _mod_kernel(c_ref, w_ref, b_ref, o_ref):
    c = c_ref[...]
    s = _silu(c)
    o_ref[0] = jnp.dot(s, w_ref[0], preferred_element_type=F32,
                       precision=lax.Precision.HIGHEST) + b_ref[0]


def _modulation(cond, w_mod, b_mod):
    n_layers, d, n6 = w_mod.shape
    rows = cond.shape[0]
    tn = _tile(n6, 1024, LANES)
    return pl.pallas_call(
        _mod_kernel,
        grid=(n_layers, n6 // tn),
        in_specs=[pl.BlockSpec((rows, d), lambda l, n: (0, 0)),
                  pl.BlockSpec((1, d, tn), lambda l, n: (l, 0, n)),
                  pl.BlockSpec((1, 1, tn), lambda l, n: (l, 0, n))],
        out_specs=pl.BlockSpec((1, rows, tn), lambda l, n: (l, 0, n)),
        out_shape=jax.ShapeDtypeStruct((n_layers, rows, n6), F32),
        compiler_params=_cparams(("arbitrary", "arbitrary")),
        name="modulation",
    )(cond, w_mod, b_mod.reshape(n_layers, 1, n6))


def _modulated_norm(x, g, ml, mc, is_ctx, shift_row, scale_row):
    y = x * lax.rsqrt(jnp.mean(x * x, axis=-1, keepdims=True) + NORM_EPS) * g
    shift = jnp.where(is_ctx, mc[shift_row:shift_row + 1], ml[shift_row:shift_row + 1])
    scale = jnp.where(is_ctx, mc[scale_row:scale_row + 1], ml[scale_row:scale_row + 1])
    return y * (1.0 + scale) + shift


def _ctx_rows(tile_index, tm, seg, ctx_len):
    row = (tile_index * tm) % seg + lax.broadcasted_iota(jnp.int32, (tm, 1), 0)
    return row < ctx_len


def _inproj_kernel(x_ref, g_ref, ml_ref, mc_ref, w_ref, o_ref, s_ref, h_scr,
                   *, tm, seg, ctx_len, n_main):
    n = pl.program_id(1)

    @pl.when(n == 0)
    def _():
        is_ctx = _ctx_rows(pl.program_id(0), tm, seg, ctx_len)
        rows = _tile(tm, 384)
        for r0 in range(0, tm, rows):
            h = _modulated_norm(x_ref[r0:r0 + rows, :], g_ref[...], ml_ref[0], mc_ref[0],
                                is_ctx[r0:r0 + rows], 0, 1)
            h_scr[r0:r0 + rows, :] = h.astype(BF16)

    acc = jnp.dot(h_scr[...], w_ref[...], preferred_element_type=F32)

    @pl.when(n < n_main)
    def _():
        o_ref[...] = acc.astype(o_ref.dtype)

    @pl.when(n == n_main)
    def _():
        s_ref[...] = acc


def _in_projection(x, g, mod, w_packed, *, n_batch, seg, ctx_len):
    ntok, d = x.shape
    tn = IN_TILE_N
    n_main = w_packed.shape[1] // tn - 1
    tm = _tile(seg, 1152, 16)
    per = seg // tm
    kern = functools.partial(_inproj_kernel, tm=tm, seg=seg, ctx_len=ctx_len, n_main=n_main)
    return pl.pallas_call(
        kern,
        grid=(ntok // tm, n_main + 1),
        in_specs=[pl.BlockSpec((tm, d), lambda i, n: (i, 0)),
                  pl.BlockSpec((1, d), lambda i, n: (0, 0)),
                  pl.BlockSpec((1, 6, d), lambda i, n: (i // per, 0, 0)),
                  pl.BlockSpec((1, 6, d), lambda i, n: (n_batch, 0, 0)),
                  pl.BlockSpec((d, tn), lambda i, n: (0, n))],
        out_specs=[pl.BlockSpec((tm, tn), lambda i, n: (i, jnp.minimum(n, n_main - 1))),
                   pl.BlockSpec((tm, tn), lambda i, n: (i, 0))],
        out_shape=[jax.ShapeDtypeStruct((ntok, n_main * tn), BF16),
                   jax.ShapeDtypeStruct((ntok, tn), F32)],
        scratch_shapes=[pltpu.VMEM((tm, d), BF16)],
        compiler_params=_cparams(("arbitrary", "arbitrary")),
        name="in_projection",
    )(x, g.reshape(1, d), mod, mod, w_packed)


def _pack_w_in(w_in_l, d_model):
    gate_cols = 3 * d_model
    dn0 = gate_cols
    dn_cols = 2 * DN_QK + 2 * DN_V + 4 * DN_HEADS
    rw0 = dn0 + dn_cols
    rw_cols = 3 * RW_WIDTH + 2 * RW_DECAY_LORA + 2 * RW_A_LORA + RW_GATE_LORA
    ss0 = rw0 + rw_cols
    z = lambda n: jnp.zeros((d_model, n), w_in_l.dtype)
    c = lambda a, n: w_in_l[:, a:a + n]
    dn_small = dn0 + 2 * DN_QK + 2 * DN_V
    rw_l = rw0 + 3 * RW_WIDTH
    parts = [
        c(0, gate_cols),
        c(dn0, 2 * DN_QK + DN_V),
        c(dn0 + 2 * DN_QK + DN_V, DN_V),
        c(rw0, 3 * RW_WIDTH),
        c(rw_l, 2 * RW_DECAY_LORA), z(64),
        c(rw_l + 2 * RW_DECAY_LORA, 2 * RW_A_LORA), z(64),
        c(rw_l + 2 * RW_DECAY_LORA + 2 * RW_A_LORA, RW_GATE_LORA), z(256),
        c(ss0, SSM_INNER),
        c(ss0 + SSM_INNER, SSM_XBC),
        c(dn_small, 4 * DN_HEADS), c(ss0 + SSM_INNER + SSM_XBC, 2 * SSM_HEADS), z(512 - 64),
    ]
    return jnp.concatenate(parts, axis=1).astype(BF16)


CONV_PAD = 72


def _conv_rows(seg, ctx_len):
    ctx0 = CONV_PAD
    lat0 = ctx0 + ctx_len + CONV_PAD
    total = lat0 + (seg - ctx_len) + CONV_PAD
    return ctx0, lat0, total


def _fill_padded(scr, x, seg, ctx_len):
    c = x.shape[1]
    n_lat = seg - ctx_len
    ctx0, lat0, total = _conv_rows(seg, ctx_len)
    xc = x[:ctx_len]
    xl = x[ctx_len:]
    col = lax.broadcasted_iota(jnp.int32, (n_lat, c), 0) % GRID_W
    zpad = jnp.zeros((CONV_PAD, c), F32)
    for i, lat in enumerate((xl, jnp.where(col == GRID_W - 1, 0.0, xl), jnp.where(col == 0, 0.0, xl))):
        scr[i, pl.ds(0, CONV_PAD), :] = zpad
        scr[i, pl.ds(ctx0, ctx_len), :] = xc
        scr[i, pl.ds(ctx0 + ctx_len, CONV_PAD), :] = zpad
        scr[i, pl.ds(lat0, n_lat), :] = lat
        scr[i, pl.ds(lat0 + n_lat, CONV_PAD), :] = zpad


def _dwconv_block(scr, w9, seg, ctx_len):
    n_lat = seg - ctx_len
    ctx0, lat0, _ = _conv_rows(seg, ctx_len)
    acc_c = None
    acc_l = None
    for dc in (-1, 0, 1):
        src = {-1: 1, 0: 0, 1: 2}[dc]
        k = 3 + (dc + 1)
        term = scr[0, pl.ds(ctx0 + dc, ctx_len), :] * w9[k:k + 1]
        acc_c = term if acc_c is None else acc_c + term
        for dr in (-1, 0, 1):
            k = (dr + 1) * 3 + (dc + 1)
            term = scr[src, pl.ds(lat0 + dr * GRID_W + dc, n_lat), :] * w9[k:k + 1]
            acc_l = term if acc_l is None else acc_l + term
    return acc_c, acc_l


def _conv_kernel(x_ref, w_ref, b_ref, o_ref, scr, *, seg, ctx_len, n_norm_blocks):
    _fill_padded(scr, x_ref[...].astype(F32), seg, ctx_len)
    do_norm = pl.program_id(1) < n_norm_blocks
    for part, rows in zip(_dwconv_block(scr, w_ref[...], seg, ctx_len),
                          (pl.ds(0, ctx_len), pl.ds(ctx_len, seg - ctx_len))):
        y = _silu(part + b_ref[...])
        yn = y * lax.rsqrt(jnp.sum(y * y, axis=-1, keepdims=True) + L2_EPS)
        o_ref[rows, :] = jnp.where(do_norm, yn, y).astype(o_ref.dtype)


def _conv_silu(p, conv_w, conv_b, *, n_batch, seg, ctx_len, col0, n_norm_blocks, name):
    ntok = p.shape[0]
    cw = conv_w.shape[-1]
    kern = functools.partial(_conv_kernel, seg=seg, ctx_len=ctx_len, n_norm_blocks=n_norm_blocks)
    return pl.pallas_call(
        kern,
        grid=(n_batch, cw // LANES),
        in_specs=[pl.BlockSpec((seg, LANES), lambda b, j: (b, col0 // LANES + j)),
                  pl.BlockSpec((9, LANES), lambda b, j: (0, j)),
                  pl.BlockSpec((1, LANES), lambda b, j: (0, j))],
        out_specs=pl.BlockSpec((seg, LANES), lambda b, j: (b, j)),
        out_shape=jax.ShapeDtypeStruct((ntok, cw), BF16),
        scratch_shapes=[pltpu.VMEM((3, _conv_rows(seg, ctx_len)[2], LANES), F32)],
        compiler_params=_cparams(("arbitrary", "arbitrary")),
        name=name,
    )(p, conv_w.reshape(9, cw), conv_b.reshape(1, cw))


def _dn_conv(p, conv_w, *, n_batch, seg, ctx_len, col0):
    return _conv_silu(p, conv_w, jnp.zeros((conv_w.shape[-1],), F32), n_batch=n_batch, seg=seg,
                      ctx_len=ctx_len, col0=col0, n_norm_blocks=2 * DN_QK // LANES, name="dn_conv")


SO_BETA, SO_G, SO_DT, SO_A = 0, 16, 32, 64


def _small_kernel(x_ref, par_ref, o_ref):
    x = x_ref[...]
    par = par_ref[...]
    neg_a = -jnp.exp(par[1:2])
    tm = x.shape[0]
    beta = _sigmoid(x[:, SM_BETA:SM_BETA + 16])
    sp = _softplus(x[:, SM_A:SM_A + 48] + par[0:1, SM_A:SM_A + 48])
    g = sp[:, 0:16] * neg_a[:, SM_A:SM_A + 16]
    dt = sp[:, 16:48]
    a = dt * neg_a[:, SM_DT:SM_DT + 32]
    o_ref[...] = jnp.concatenate([beta, g, dt, a, jnp.zeros((tm, LANES - 96), F32)], axis=1)


def _small_prep(ps, dn_a_log, dn_dt_bias, ssm_a_log, ssm_dt_bias):
    ntok = ps.shape[0]
    tm = _tile(ntok, 1024)
    zeros16 = jnp.zeros((16,), F32)
    bias = jnp.concatenate([zeros16, dn_dt_bias.reshape(-1), ssm_dt_bias.reshape(-1),
                            jnp.zeros((LANES - 64,), F32)])
    a_log = jnp.concatenate([zeros16, dn_a_log.reshape(-1), ssm_a_log.reshape(-1),
                             jnp.zeros((LANES - 64,), F32)])
    par = jnp.concatenate([bias[None], a_log[None], jnp.zeros((6, LANES), F32)], axis=0)
    return pl.pallas_call(
        _small_kernel,
        grid=(ntok // tm,),
        in_specs=[pl.BlockSpec((tm, LANES), lambda i: (i, 0)),
                  pl.BlockSpec((8, LANES), lambda i: (0, 0))],
        out_specs=pl.BlockSpec((tm, LANES), lambda i: (i, 0)),
        out_shape=jax.ShapeDtypeStruct((ntok, LANES), F32),
        compiler_params=_cparams(("arbitrary",)),
        name="small_prep",
    )(ps, par)


def _chunk_of_step(d, s, n_ctx, n_all):
    bwd = jnp.where(s < n_ctx, n_ctx - 1 - s, n_all + n_ctx - 1 - s)
    return jnp.where(d == 0, s, bwd)


def _order_masks(d, size):
    ri = lax.broadcasted_iota(jnp.int32, (size, size), 0)
    ci = lax.broadcasted_iota(jnp.int32, (size, size), 1)
    delta = (ri - ci) * (1 - 2 * d)
    return delta >= 0, delta > 0, ri == ci


def _col_to_row(col, eye):
    return jnp.sum(jnp.where(eye, col, 0.0), axis=0, keepdims=True)


def _dn_scan_kernel(qf_ref, kf_ref, vf_ref, smf_ref, qb_ref, kb_ref, vb_ref, smb_ref,
                    of_ref, ob_ref, s_scr):
    c = DN_CHUNK
    h_n = DN_HEADS

    @pl.when(pl.program_id(1) == 0)
    def _():
        s_scr[...] = jnp.zeros(s_scr.shape, F32)

    scale = DN_DK ** -0.5
    refs = ((qf_ref, kf_ref, vf_ref, smf_ref, of_ref), (qb_ref, kb_ref, vb_ref, smb_ref, ob_ref))
    masks = [_order_masks(d, c) for d in range(2)]
    gcs = []
    gtots = []
    betas = []
    for d in range(2):
        sm = refs[d][3][...]
        betas.append(sm[:, SO_BETA + d * h_n:SO_BETA + (d + 1) * h_n])
        g = sm[:, SO_G + d * h_n:SO_G + (d + 1) * h_n]
        gcs.append(_dot_exact_lhs(masks[d][0].astype(BF16), g))
        gtots.append(jnp.sum(g, axis=0, keepdims=True))

    items = [(d, h) for d in range(2) for h in range(h_n)]
    n = range(len(items))
    lanes = [slice(h * DN_DK, (h + 1) * DN_DK) for _, h in items]
    incl = [masks[d][0] for d, _ in items]
    strict = [masks[d][1] for d, _ in items]
    eye = masks[0][2]
    qs = [refs[d][0][:, lanes[i]].astype(F32) * scale for i, (d, _) in enumerate(items)]
    ks = [refs[d][1][:, lanes[i]].astype(F32) for i, (d, _) in enumerate(items)]
    vs = [refs[d][2][:, lanes[i]].astype(F32) for i, (d, _) in enumerate(items)]
    bcols = [betas[d][:, h:h + 1] for d, h in items]
    gcols = [gcs[d][:, h:h + 1] for d, h in items]
    gts = [gtots[d][:, h:h + 1] for d, h in items]
    decs = [jnp.exp(jnp.where(incl[i], gcols[i] - _col_to_row(gcols[i], eye), 0.0)) for i in n]
    kbs = [ks[i] * bcols[i] for i in n]
    a_s = [_dot_nt(jnp.concatenate([kbs[i], qs[i]], axis=0), ks[i]) for i in n]
    atts = [jnp.where(incl[i], a_s[i][c:] * decs[i], 0.0) for i in n]
    tinvs = _inv_unit_lower([-jnp.where(strict[i], a_s[i][:c] * decs[i], 0.0) for i in n], c)
    egs = [jnp.exp(gcols[i]) for i in n]
    uws = [_dot3(tinvs[i], jnp.concatenate([vs[i] * bcols[i], kbs[i] * egs[i]], axis=1))
           for i in n]
    s_olds = [s_scr[d, h] for d, h in items]
    wss = [_dot(jnp.concatenate([uws[i][:, DN_DK:], qs[i] * egs[i]], axis=0), s_olds[i])
           for i in n]
    v_news = [uws[i][:, :DN_DK] - wss[i][:c] for i in n]
    outs = [wss[i][c:] + _dot(atts[i], v_news[i]) for i in n]
    upds = [_dot_tn(ks[i] * jnp.exp(gts[i] - gcols[i]), v_news[i]) for i in n]
    for i, (d, h) in enumerate(items):
        s_scr[d, h] = s_olds[i] * jnp.exp(gts[i]) + upds[i]
        refs[d][4][:, lanes[i]] = outs[i].astype(refs[d][4].dtype)


def _scan_specs(c, widths, n_all, n_ctx):
    def spec(d, width, col):
        return pl.BlockSpec(
            (c, width), lambda b, s: (b * n_all + _chunk_of_step(d, s, n_ctx, n_all), col))
    return [spec(d, w, col) for d in range(2) for w, col in widths]


def _dn_scan(qkv, small, *, n_batch, seg, ctx_len):
    ntok = qkv.shape[0]
    c = DN_CHUNK
    n_all = seg // c
    n_ctx = ctx_len // c
    ins = _scan_specs(c, [(DN_QK, 0), (DN_QK, 1), (DN_V, 2), (LANES, 0)], n_all, n_ctx)
    outs = _scan_specs(c, [(DN_V, 0)], n_all, n_ctx)
    return pl.pallas_call(
        _dn_scan_kernel,
        grid=(n_batch, n_all),
        in_specs=ins,
        out_specs=outs,
        out_shape=[jax.ShapeDtypeStruct((ntok, DN_V), SCAN_OUT_DTYPE)] * 2,
        scratch_shapes=[pltpu.VMEM((2, DN_HEADS, DN_DK, DN_DK), F32)],
        compiler_params=_cparams(("arbitrary", "arbitrary")),
        name="dn_scan",
    )(qkv, qkv, qkv, small, qkv, qkv, qkv, small)


def _dn_readout_kernel(of_ref, ob_ref, gate_ref, g_ref, y_ref):
    o = of_ref[...].astype(F32) + ob_ref[...].astype(F32)
    tm = o.shape[0]
    gate = gate_ref[...].astype(F32)
    outs = []
    for h in range(DN_HEADS):
        lanes = slice(h * DN_DK, (h + 1) * DN_DK)
        oh = o[:, lanes]
        yh = oh * lax.rsqrt(jnp.mean(oh * oh, axis=-1, keepdims=True) + NORM_EPS) * g_ref[...]
        outs.append(yh * _silu(gate[:, lanes]))
    y_ref[...] = jnp.concatenate(outs, axis=1).astype(y_ref.dtype)


def _dn_readout(o_f, o_b, p, norm_g, *, col_gate):
    ntok = p.shape[0]
    tm = _tile(ntok, 512)
    tok = pl.BlockSpec((tm, DN_V), lambda i: (i, 0))
    return pl.pallas_call(
        _dn_readout_kernel,
        grid=(ntok // tm,),
        in_specs=[tok, tok,
                  pl.BlockSpec((tm, DN_V), lambda i: (i, col_gate // DN_V)),
                  pl.BlockSpec((1, DN_DK), lambda i: (0, 0))],
        out_specs=tok,
        out_shape=jax.ShapeDtypeStruct((ntok, DN_V), BF16),
        compiler_params=_cparams(("arbitrary",)),
        name="dn_readout",
    )(o_f, o_b, p, norm_g.reshape(1, DN_DK))


def _head_expander(n_heads, head_dim):
    hi = lax.broadcasted_iota(jnp.int32, (n_heads, n_heads * head_dim), 0)
    li = lax.broadcasted_iota(jnp.int32, (n_heads, n_heads * head_dim), 1)
    lo = hi * head_dim
    return jnp.where((li >= lo) & (li < lo + head_dim), 1.0, 0.0).astype(BF16)


def _ssd_scan_kernel(x_ref, b_ref, c_ref, sm_ref, y_ref, s_scr):
    d = pl.program_id(1)
    c = SSM_CHUNK
    hn = SSM_HEADS
    hd = SSM_HEADDIM
    per_group = hn // SSM_GROUPS
    gw = per_group * hd

    @pl.when(pl.program_id(2) == 0)
    def _():
        s_scr[...] = jnp.zeros(s_scr.shape, F32)

    incl, _, eye = _order_masks(d, c)
    sm = sm_ref[...]
    fwd = d == 0
    dt = jnp.where(fwd, sm[:, SO_DT:SO_DT + hn], sm[:, SO_DT + hn:SO_DT + 2 * hn])
    a = jnp.where(fwd, sm[:, SO_A:SO_A + hn], sm[:, SO_A + hn:SO_A + 2 * hn])
    acum = _dot_exact_lhs(incl.astype(BF16), a)
    expand = _head_expander(hn, hd)
    acum_x = _dot_exact_rhs(acum, expand)
    atot_x = jnp.sum(_dot_exact_rhs(a, expand), axis=0, keepdims=True)
    xdt = x_ref[...].astype(F32) * _dot_exact_rhs(dt, expand)
    xdec = xdt * jnp.exp(atot_x - acum_x)
    lane_lo = lax.broadcasted_iota(jnp.int32, (c, 2 * hd), 1) < hd

    groups = range(SSM_GROUPS)
    bgs = [b_ref[:, g * SSM_STATE:(g + 1) * SSM_STATE] for g in groups]
    cgs = [c_ref[:, g * SSM_STATE:(g + 1) * SSM_STATE] for g in groups]
    cbs = [_dot_nt(cgs[g], bgs[g]) for g in groups]
    ms = []
    for h in range(hn):
        acol = acum[:, h:h + 1]
        lmat = jnp.exp(jnp.where(incl, acol - _col_to_row(acol, eye), 0.0))
        ms.append(jnp.where(incl, cbs[h // per_group] * lmat, 0.0))
    ydiag = []
    for pair in range(hn // 2):
        xp = xdt[:, pair * 2 * hd:(pair + 1) * 2 * hd]
        ydiag.append(jnp.where(lane_lo, _dot(ms[2 * pair], xp), _dot(ms[2 * pair + 1], xp)))
    s_olds = [s_scr[g] for g in groups]
    yoffs = [_dot(cgs[g], s_olds[g]) for g in groups]
    sts = [_dot_tn(bgs[g], xdec[:, g * gw:(g + 1) * gw]) for g in groups]
    eacum = jnp.exp(acum_x)
    etot = jnp.exp(atot_x)
    for g in groups:
        lanes = slice(g * gw, (g + 1) * gw)
        s_scr[g] = s_olds[g] * etot[:, lanes] + sts[g]
        yd = jnp.concatenate(ydiag[g * per_group // 2:(g + 1) * per_group // 2], axis=1)
        y_ref[0, :, lanes] = (yd + yoffs[g] * eacum[:, lanes]).astype(y_ref.dtype)


def _ssd_scan(xbc, small, *, n_batch, seg, ctx_len):
    ntok = xbc.shape[0]
    c = SSM_CHUNK
    n_all = seg // c
    n_ctx = ctx_len // c
    gn = SSM_GROUPS * SSM_STATE

    def blk(col):
        return lambda b, d, s: (b * n_all + _chunk_of_step(d, s, n_ctx, n_all), col)

    return pl.pallas_call(
        _ssd_scan_kernel,
        grid=(n_batch, 2, n_all),
        in_specs=[pl.BlockSpec((c, SSM_INNER), blk(0)),
                  pl.BlockSpec((c, gn), blk(SSM_INNER // gn)),
                  pl.BlockSpec((c, gn), blk(SSM_INNER // gn + 1)),
                  pl.BlockSpec((c, LANES), blk(0))],
        out_specs=pl.BlockSpec(
            (1, c, SSM_INNER),
            lambda b, d, s: (d, b * n_all + _chunk_of_step(d, s, n_ctx, n_all), 0)),
        out_shape=jax.ShapeDtypeStruct((2, ntok, SSM_INNER), SCAN_OUT_DTYPE),
        scratch_shapes=[pltpu.VMEM((SSM_GROUPS, SSM_STATE, SSM_INNER // SSM_GROUPS), F32)],
        compiler_params=_cparams(("arbitrary", "arbitrary", "arbitrary")),
        name="ssd_scan",
    )(xbc, xbc, xbc, small)


def _ssd_readout_kernel(y_ref, xs_ref, z_ref, d_ref, g_ref, o_ref):
    y = ((y_ref[0].astype(F32) + y_ref[1].astype(F32) + d_ref[...] * xs_ref[...].astype(F32))
         * _silu(z_ref[...].astype(F32)))
    gw = SSM_INNER // SSM_GROUPS
    outs = []
    for g in range(SSM_GROUPS):
        lanes = slice(g * gw, (g + 1) * gw)
        yg = y[:, lanes]
        outs.append(yg * lax.rsqrt(jnp.mean(yg * yg, axis=-1, keepdims=True) + SSM_NORM_EPS)
                    * g_ref[:, lanes])
    o_ref[...] = jnp.concatenate(outs, axis=1).astype(o_ref.dtype)


def _ssd_readout(y2, xbc, p, d_skip, norm_g, *, col_z):
    ntok = p.shape[0]
    tm = _tile(ntok, 512)
    w = SSM_INNER
    return pl.pallas_call(
        _ssd_readout_kernel,
        grid=(ntok // tm,),
        in_specs=[pl.BlockSpec((2, tm, w), lambda i: (0, i, 0)),
                  pl.BlockSpec((tm, w), lambda i: (i, 0)),
                  pl.BlockSpec((tm, w), lambda i: (i, col_z // w)),
                  pl.BlockSpec((1, w), lambda i: (0, 0)),
                  pl.BlockSpec((1, w), lambda i: (0, 0))],
        out_specs=pl.BlockSpec((tm, w), lambda i: (i, 0)),
        out_shape=jax.ShapeDtypeStruct((ntok, w), BF16),
        compiler_params=_cparams(("arbitrary",)),
        name="ssd_readout",
    )(y2, xbc, p, jnp.repeat(d_skip, SSM_HEADDIM).reshape(1, w), norm_g.reshape(1, w))


def _rw_shift_kernel(x_ref, mu_ref, o_ref, scr, *, seg, ctx_len):
    x = x_ref[...].astype(F32)
    n_lat = seg - ctx_len
    _fill_padded(scr, x, seg, ctx_len)
    ctx0, lat0, _ = _conv_rows(seg, ctx_len)
    mu = mu_ref[...]
    lane_c = lax.broadcasted_iota(jnp.int32, (ctx_len, LANES), 1)
    prev = scr[0, pl.ds(ctx0 - 1, ctx_len), :]
    nxt = scr[0, pl.ds(ctx0 + 1, ctx_len), :]
    xc = x[:ctx_len]
    o_ref[pl.ds(0, ctx_len), :] = (xc + (jnp.where(lane_c % 2 == 0, prev, nxt) - xc) * mu
                                   ).astype(o_ref.dtype)
    sel = lax.broadcasted_iota(jnp.int32, (n_lat, LANES), 1) % 4
    left = scr[1, pl.ds(lat0 - 1, n_lat), :]
    right = scr[2, pl.ds(lat0 + 1, n_lat), :]
    up = scr[0, pl.ds(lat0 - GRID_W, n_lat), :]
    down = scr[0, pl.ds(lat0 + GRID_W, n_lat), :]
    shifted = jnp.where(sel == 0, left, jnp.where(sel == 1, right, jnp.where(sel == 2, up, down)))
    xl = x[ctx_len:]
    o_ref[pl.ds(ctx_len, n_lat), :] = (xl + (shifted - xl) * mu).astype(o_ref.dtype)


def _rw_shift(p, mu_packed, *, n_batch, seg, ctx_len, col0):
    ntok = p.shape[0]
    cw = mu_packed.shape[-1]
    kern = functools.partial(_rw_shift_kernel, seg=seg, ctx_len=ctx_len)
    return pl.pallas_call(
        kern,
        grid=(n_batch, cw // LANES),
        in_specs=[pl.BlockSpec((seg, LANES), lambda b, j: (b, col0 // LANES + j)),
                  pl.BlockSpec((1, LANES), lambda b, j: (0, j))],
        out_specs=pl.BlockSpec((seg, LANES), lambda b, j: (b, j)),
        out_shape=jax.ShapeDtypeStruct((ntok, cw), BF16),
        scratch_shapes=[pltpu.VMEM((3, _conv_rows(seg, ctx_len)[2], LANES), F32)],
        compiler_params=_cparams(("arbitrary", "arbitrary")),
        name="rw_shift",
    )(p, mu_packed.reshape(1, cw))


def _head_sum(x, n_heads, head_dim):
    expand = _head_expander(n_heads, head_dim)
    hi = lax.broadcasted_iota(jnp.int32, (n_heads * head_dim, n_heads), 1)
    li = lax.broadcasted_iota(jnp.int32, (n_heads * head_dim, n_heads), 0)
    lo = hi * head_dim
    reduce = jnp.where((li >= lo) & (li < lo + head_dim), 1.0, 0.0).astype(BF16)
    return _dot_exact_rhs(_dot_exact_rhs(x, reduce), expand)


def _rw_prep_kernel(x_ref, w0_ref, w2_ref, a0_ref, a2_ref, g2_ref, kk_ref, ka_ref, rk_ref,
                    r_ref, v_ref, nk_ref, gate_ref, bonus_ref, lw_ref, kd_ref, b_ref):
    w = RW_WIDTH
    r = x_ref[:, 0:w].astype(F32)
    k = x_ref[:, w:2 * w].astype(F32)
    v = x_ref[:, 2 * w:3 * w].astype(F32)
    lora = 3 * w
    wl = x_ref[:, lora + RWL_W:lora + RWL_W + 256].astype(F32)
    al = x_ref[:, lora + RWL_A:lora + RWL_A + 256]
    gl = x_ref[:, lora + RWL_G:lora + RWL_G + RW_GATE_LORA].astype(F32)
    w_raw = w0_ref[...] + _dot(jnp.tanh(wl), w2_ref[...])
    log_decay = -jnp.exp(-_softplus(-w_raw) - 0.5)
    icl = _sigmoid(a0_ref[...] + _dot(al, a2_ref[...]))
    gate_ref[...] = _dot(_sigmoid(gl), g2_ref[...]).astype(gate_ref.dtype)
    kx = k * kk_ref[...]
    nk = kx * lax.rsqrt(_head_sum(kx * kx, RW_HEADS, RW_HEAD) + L2_EPS)
    r_ref[...] = r.astype(r_ref.dtype)
    v_ref[...] = v.astype(v_ref.dtype)
    nk_ref[...] = nk.astype(nk_ref.dtype)
    ksum = None
    for d in range(2):
        icl_d = icl[:, d * w:(d + 1) * w]
        k_d = k * (1.0 + (icl_d - 1.0) * ka_ref[...])
        lw_ref[d] = log_decay[:, d * w:(d + 1) * w]
        kd_ref[d] = k_d.astype(kd_ref.dtype)
        b_ref[d] = (nk * icl_d).astype(b_ref.dtype)
        ksum = k_d if ksum is None else ksum + k_d
    bonus_ref[...] = (_head_sum(r * ksum * rk_ref[...], RW_HEADS, RW_HEAD) * v
                      ).astype(bonus_ref.dtype)


def _pack_rw_vec(vec):
    w3 = 3 * RW_WIDTH
    z = lambda n: jnp.zeros((n,), vec.dtype)
    l0 = w3 + 2 * RW_DECAY_LORA
    l1 = l0 + 2 * RW_A_LORA
    return jnp.concatenate([vec[:w3], vec[w3:l0], z(64), vec[l0:l1], z(64), vec[l1:], z(256)])


def _pack_lora(m2):
    lora, w = m2.shape[1], m2.shape[2]
    out = jnp.zeros((256, 2 * w), F32)
    out = out.at[0:lora, 0:w].set(m2[0])
    out = out.at[lora:2 * lora, w:2 * w].set(m2[1])
    return out.astype(BF16)


def _rw_prep(xs, w0, w2, a0, a2, g2, k_k, k_a, r_k):
    ntok = xs.shape[0]
    w = RW_WIDTH
    tm = _tile(ntok, 256)
    row = lambda a: a.reshape(1, -1)
    full = lambda shape: pl.BlockSpec(shape, lambda i: (0,) * len(shape))
    tok = pl.BlockSpec((tm, w), lambda i: (i, 0))
    tok2 = pl.BlockSpec((2, tm, w), lambda i: (0, i, 0))
    sd = jax.ShapeDtypeStruct
    return pl.pallas_call(
        _rw_prep_kernel,
        grid=(ntok // tm,),
        in_specs=[pl.BlockSpec((tm, 4 * w), lambda i: (i, 0)),
                  full((1, 2 * w)), full((256, 2 * w)), full((1, 2 * w)), full((256, 2 * w)),
                  full((RW_GATE_LORA, w)), full((1, w)), full((1, w)), full((1, w))],
        out_specs=[tok, tok, tok, tok, tok, tok2, tok2, tok2],
        out_shape=[sd((ntok, w), BF16)] * 5
        + [sd((2, ntok, w), F32), sd((2, ntok, w), BF16), sd((2, ntok, w), BF16)],
        compiler_params=_cparams(("arbitrary",)),
        name="rw_prep",
    )(xs, row(w0), _pack_lora(w2), row(a0), _pack_lora(a2), g2.astype(BF16), row(k_k), row(k_a),
      row(r_k))


def _stack_heads(x, lane_lo):
    return jnp.concatenate([jnp.where(lane_lo, x, 0.0), jnp.where(lane_lo, 0.0, x)], axis=0)


def _rw_scan_kernel(rf_ref, vf_ref, nkf_ref, lwf_ref, kdf_ref, bf_ref,
                    rb_ref, vb_ref, nkb_ref, lwb_ref, kdb_ref, bb_ref, yf_ref, yb_ref, h_scr):
    c = RW_CHUNK
    pw = 2 * RW_HEAD
    n_pairs = RW_HEADS // 2

    @pl.when(pl.program_id(1) == 0)
    def _():
        h_scr[...] = jnp.zeros(h_scr.shape, F32)

    refs = ((rf_ref, vf_ref, nkf_ref, lwf_ref, kdf_ref, bf_ref, yf_ref),
            (rb_ref, vb_ref, nkb_ref, lwb_ref, kdb_ref, bb_ref, yb_ref))
    rj = lax.broadcasted_iota(jnp.int32, (pw, pw), 0)
    cj = lax.broadcasted_iota(jnp.int32, (pw, pw), 1)
    eye_p = rj == cj
    lane_lo = lax.broadcasted_iota(jnp.int32, (c, pw), 1) < RW_HEAD
    ri = lax.broadcasted_iota(jnp.int32, (2 * c, 2 * c), 0) % c
    ci = lax.broadcasted_iota(jnp.int32, (2 * c, 2 * c), 1) % c

    per_dir = []
    for d in range(2):
        r_ref, v_ref, nk_ref, lw_ref, kd_ref, b_ref, _ = refs[d]
        incl, _, _ = _order_masks(d, c)
        lw = lw_ref[0]
        cw = _dot_exact_lhs(incl.astype(BF16), lw)
        tot = jnp.sum(lw, axis=0, keepdims=True)
        e_neg = jnp.exp(-cw)
        e_rem = jnp.exp(tot - cw)
        b = b_ref[0].astype(F32)
        kd = kd_ref[0].astype(F32)
        delta = (ri - ci) * (1 - 2 * d)
        incl2 = delta >= 0
        per_dir.append(dict(
            a_t=-nk_ref[...].astype(F32) * jnp.exp(cw - lw), r_t=r_ref[...].astype(F32) * jnp.exp(cw),
            b_t=b * e_neg, k_t=kd * e_neg, b_h=b * e_rem, k_h=kd * e_rem,
            v=v_ref[...].astype(F32), e_tot=jnp.exp(tot), strict2=delta > 0,
            mask_y=jnp.concatenate([incl2, incl2], axis=1)))

    items = [(d, p) for d in range(2) for p in range(n_pairs)]
    n = range(len(items))
    ln = [slice(p * pw, (p + 1) * pw) for _, p in items]
    dd = [per_dir[d] for d, _ in items]

    def stacked(i, top, bottom):
        return jnp.concatenate([_stack_heads(dd[i][top][:, ln[i]], lane_lo),
                                _stack_heads(dd[i][bottom][:, ln[i]], lane_lo)], axis=0)

    xs = [stacked(i, "a_t", "r_t") for i in n]
    ys = [stacked(i, "b_t", "k_t") for i in n]
    vst = [_stack_heads(dd[i]["v"][:, ln[i]], lane_lo) for i in n]
    gs = [_dot_nt(xs[i], ys[i]) for i in n]
    tinvs = _inv_unit_lower(
        [jnp.where(dd[i]["strict2"], gs[i][:2 * c, :2 * c], 0.0) for i in n], c)
    h_olds = [h_scr[d, p] for d, p in items]
    xh = [_dot(xs[i], h_olds[i]) for i in n]
    rhs = [xh[i][:2 * c] + _dot(jnp.where(dd[i]["strict2"], gs[i][:2 * c, 2 * c:], 0.0), vst[i])
           for i in n]
    us = [_dot3(tinvs[i], rhs[i]) for i in n]
    uv = [jnp.concatenate([us[i], vst[i]], axis=0) for i in n]
    yst = [xh[i][2 * c:] + _dot(jnp.where(dd[i]["mask_y"], gs[i][2 * c:, :], 0.0), uv[i])
           for i in n]
    upd = [_dot_tn(stacked(i, "b_h", "k_h"), uv[i]) for i in n]
    for i, (d, p) in enumerate(items):
        e_col = jnp.sum(jnp.where(eye_p, dd[i]["e_tot"][:, ln[i]], 0.0), axis=1, keepdims=True)
        h_scr[d, p] = h_olds[i] * e_col + upd[i]
        refs[d][6][:, ln[i]] = (yst[i][:c] + yst[i][c:]).astype(refs[d][6].dtype)


def _rw_scan(r, v, nk, lw, kd, b, *, n_batch, seg, ctx_len):
    ntok = r.shape[0]
    c = RW_CHUNK
    w = RW_WIDTH
    n_all = seg // c
    n_ctx = ctx_len // c
    def row(d):
        return lambda bb, s: bb * n_all + _chunk_of_step(d, s, n_ctx, n_all)

    def tok(d):
        return pl.BlockSpec((c, w), lambda bb, s: (row(d)(bb, s), 0))

    def tok2(d):
        return pl.BlockSpec((1, c, w), lambda bb, s: (d, row(d)(bb, s), 0))

    return pl.pallas_call(
        _rw_scan_kernel,
        grid=(n_batch, n_all),
        in_specs=[spec(d) for d in range(2) for spec in (tok, tok, tok, tok2, tok2, tok2)],
        out_specs=[tok(0), tok(1)],
        out_shape=[jax.ShapeDtypeStruct((ntok, w), SCAN_OUT_DTYPE)] * 2,
        scratch_shapes=[pltpu.VMEM((2, RW_HEADS // 2, 2 * RW_HEAD, 2 * RW_HEAD), F32)],
        compiler_params=_cparams(("arbitrary", "arbitrary")),
        name="rw_scan",
    )(r, v, nk, lw, kd, b, r, v, nk, lw, kd, b)


def _rw_readout_kernel(yf_ref, yb_ref, bonus_ref, gate_ref, g_ref, b_ref, o_ref):
    y = yf_ref[...].astype(F32) + yb_ref[...].astype(F32)
    inv_n = 1.0 / RW_HEAD
    mean = _head_sum(y, RW_HEADS, RW_HEAD) * inv_n
    yc = y - mean
    var = _head_sum(yc * yc, RW_HEADS, RW_HEAD) * inv_n
    yn = yc * lax.rsqrt(var + RW_GN_EPS) * g_ref[...] + b_ref[...]
    o_ref[...] = ((yn + bonus_ref[...].astype(F32)) * gate_ref[...].astype(F32)).astype(o_ref.dtype)


def _rw_readout(y_f, y_b, bonus, gate, ln_g, ln_b):
    ntok = bonus.shape[0]
    w = RW_WIDTH
    tm = _tile(ntok, 512)
    tok = pl.BlockSpec((tm, w), lambda i: (i, 0))
    vec = pl.BlockSpec((1, w), lambda i: (0, 0))
    return pl.pallas_call(
        _rw_readout_kernel,
        grid=(ntok // tm,),
        in_specs=[tok, tok, tok, tok, vec, vec],
        out_specs=tok,
        out_shape=jax.ShapeDtypeStruct((ntok, w), BF16),
        compiler_params=_cparams(("arbitrary",)),
        name="rw_readout",
    )(y_f, y_b, bonus, gate, ln_g.reshape(1, w), ln_b.reshape(1, w))


def _branch_mix_kernel(oa_ref, ob_ref, oc_ref, ga_ref, gb_ref, gc_ref, w_ref, y_ref):
    acc = None
    for i, (o_ref, g_ref) in enumerate(((oa_ref, ga_ref), (ob_ref, gb_ref), (oc_ref, gc_ref))):
        term = (_sigmoid(g_ref[...].astype(F32))
                * jnp.dot(o_ref[...], w_ref[i], preferred_element_type=F32))
        acc = term if acc is None else acc + term
    y_ref[...] = acc.astype(y_ref.dtype)


def _branch_mix(oa, ob, oc, p, w_branch_bf16, *, d_model):
    ntok = p.shape[0]
    bw = oa.shape[1]
    tm = _tile(ntok, 512)
    tn = 512
    nb = d_model // tn
    o_spec = pl.BlockSpec((tm, bw), lambda i, n: (i, 0))
    g_spec = lambda br: pl.BlockSpec((tm, tn), lambda i, n: (i, br * nb + n))
    return pl.pallas_call(
        _branch_mix_kernel,
        grid=(ntok // tm, nb),
        in_specs=[o_spec, o_spec, o_spec, g_spec(0), g_spec(1), g_spec(2),
                  pl.BlockSpec((3, bw, tn), lambda i, n: (0, 0, n))],
        out_specs=pl.BlockSpec((tm, tn), lambda i, n: (i, n)),
        out_shape=jax.ShapeDtypeStruct((ntok, d_model), BF16),
        compiler_params=_cparams(("arbitrary", "arbitrary")),
        name="branch_mix",
    )(oa, ob, oc, p, p, p, w_branch_bf16)


ROUTE_TILE = 256


def _dot3_nt(a, b):
    ah, al = _split2(a)
    bh, bl = _split2(b)
    d = functools.partial(lax.dot_general, dimension_numbers=(((1,), (1,)), ((), ())),
                          preferred_element_type=F32)
    return d(ah, bh) + d(ah, bl) + d(al, bh)


def _outproj_router_kernel(y_ref, x_ref, ml_ref, mc_ref, wo_ref, g_ref, wr_ref, br_ref,
                           xo_ref, f_ref, idx_ref, wgt_ref, rank_ref, cnt_ref, run_scr,
                           *, tm, seg, ctx_len):
    i = pl.program_id(0)

    @pl.when(i == 0)
    def _():
        run_scr[...] = jnp.zeros(run_scr.shape, F32)

    is_ctx = _ctx_rows(i, tm, seg, ctx_len)
    ml = ml_ref[0]
    mc = mc_ref[0]
    gate = jnp.where(is_ctx, mc[2:3], ml[2:3])
    x_new = x_ref[...] + gate * jnp.dot(y_ref[...], wo_ref[...], preferred_element_type=F32)
    xo_ref[...] = x_new
    f = _modulated_norm(x_new, g_ref[...], ml, mc, is_ctx, 3, 4)
    f_ref[...] = f

    ne = N_EXPERTS
    logits = _dot3_nt(wr_ref[...], f) + br_ref[...]
    eidx = lax.broadcasted_iota(jnp.int32, (ne, tm), 0)
    vals = logits
    picks = []
    tops = []
    hots = []
    for _ in range(TOP_K):
        m = jnp.max(vals, axis=0, keepdims=True)
        pick = jnp.min(jnp.where(vals == m, eidx, ne), axis=0, keepdims=True)
        hot = eidx == pick
        vals = jnp.where(hot, -jnp.inf, vals)
        tops.append(m)
        picks.append(pick)
        hots.append(hot)
    es = [jnp.exp(t - tops[0]) for t in tops]
    denom = es[0] + es[1] + es[2] + es[3]
    hot_all = jnp.where(hots[0] | hots[1] | hots[2] | hots[3], 1.0, 0.0)
    si = lax.broadcasted_iota(jnp.int32, (tm, tm), 0)
    ti = lax.broadcasted_iota(jnp.int32, (tm, tm), 1)
    earlier = jnp.where(si < ti, 1.0, 0.0).astype(BF16)
    before = jnp.dot(hot_all.astype(BF16), earlier, preferred_element_type=F32)
    rank = run_scr[:, 0:1] + before
    run_new = run_scr[...] + jnp.sum(hot_all, axis=1, keepdims=True)
    run_scr[...] = run_new
    cnt_ref[...] = run_new
    idx_ref[...] = jnp.concatenate(picks, axis=0)
    wgt_ref[...] = jnp.concatenate([e / denom for e in es], axis=0)
    rank_ref[...] = jnp.concatenate(
        [jnp.sum(jnp.where(h, rank, 0.0), axis=0, keepdims=True) for h in hots],
        axis=0).astype(jnp.int32)


def _outproj_router(y, x, mod, w_out_bf16, g_ffn, w_router, b_router, *, n_batch, seg, ctx_len):
    ntok, d = x.shape
    tm = _tile(seg, ROUTE_TILE, LANES)
    per = seg // tm
    ne = N_EXPERTS
    kern = functools.partial(_outproj_router_kernel, tm=tm, seg=seg, ctx_len=ctx_len)
    tok = pl.BlockSpec((tm, d), lambda i: (i, 0))
    sel = pl.BlockSpec((TOP_K, tm), lambda i: (0, i))
    sd = jax.ShapeDtypeStruct
    return pl.pallas_call(
        kern,
        grid=(ntok // tm,),
        in_specs=[tok, tok,
                  pl.BlockSpec((1, 6, d), lambda i: (i // per, 0, 0)),
                  pl.BlockSpec((1, 6, d), lambda i: (n_batch, 0, 0)),
                  pl.BlockSpec((d, d), lambda i: (0, 0)),
                  pl.BlockSpec((1, d), lambda i: (0, 0)),
                  pl.BlockSpec((ne, d), lambda i: (0, 0)),
                  pl.BlockSpec((ne, 1), lambda i: (0, 0))],
        out_specs=[tok, tok, sel, sel, sel, pl.BlockSpec((ne, LANES), lambda i: (0, 0))],
        out_shape=[sd((ntok, d), F32), sd((ntok, d), F32), sd((TOP_K, ntok), jnp.int32),
                   sd((TOP_K, ntok), F32), sd((TOP_K, ntok), jnp.int32), sd((ne, LANES), F32)],
        scratch_shapes=[pltpu.VMEM((ne, LANES), F32)],
        compiler_params=_cparams(("arbitrary",)),
        name="outproj_router",
    )(y, x, mod, mod, w_out_bf16, g_ffn.reshape(1, d), w_router.T, b_router.reshape(ne, 1))


EXPERT_TILE = 256


def _dispatch_kernel(dest_ref, f_ref, xs_in_ref, xs_ref, sem, *, tm):
    del xs_in_ref

    def row_copy(j, k):
        dst = dest_ref[0, 0, k * tm + j]
        return pltpu.make_async_copy(f_ref.at[pl.ds(j, 1), :], xs_ref.at[pl.ds(dst, 1), :], sem)

    def issue(j, carry):
        for k in range(TOP_K):
            row_copy(j, k).start()
        return carry

    lax.fori_loop(0, tm, issue, 0)

    def drain(j, carry):
        for k in range(TOP_K):
            row_copy(j, k).wait()
        return carry

    lax.fori_loop(0, tm, drain, 0)


def _dispatch(f, dest_tiles, n_rows):
    ntok, d = f.shape
    tm = dest_tiles.shape[-1] // TOP_K
    kern = functools.partial(_dispatch_kernel, tm=tm)
    return pl.pallas_call(
        kern,
        grid=(ntok // tm,),
        in_specs=[pl.BlockSpec((1, 1, TOP_K * tm), lambda i: (i, 0, 0), memory_space=pltpu.SMEM),
                  pl.BlockSpec((tm, d), lambda i: (i, 0)),
                  pl.BlockSpec(memory_space=pl.ANY)],
        out_specs=pl.BlockSpec(memory_space=pl.ANY),
        out_shape=jax.ShapeDtypeStruct((n_rows, d), F32),
        scratch_shapes=[pltpu.SemaphoreType.DMA(())],
        input_output_aliases={2: 0},
        compiler_params=_cparams(("arbitrary",)),
        name="moe_dispatch",
    )(dest_tiles, f, jnp.zeros((n_rows, d), F32))


def _experts_kernel(te_ref, xs_ref, w1_ref, b1_ref, w2_ref, b2_ref, ys_ref):
    del te_ref
    hu = jnp.dot(xs_ref[...].astype(BF16), w1_ref[0], preferred_element_type=F32) + b1_ref[0]
    gt = jnp.minimum(hu[:, :D_EXPERT], SWIGLU_LIMIT)
    up = jnp.clip(hu[:, D_EXPERT:], -SWIGLU_LIMIT, SWIGLU_LIMIT)
    act = gt * _sigmoid(SWIGLU_ALPHA * gt) * (up + 1.0)
    ys_ref[...] = jnp.dot(act.astype(BF16), w2_ref[0], preferred_element_type=F32) + b2_ref[0]


def _experts(xs, tile_expert, w1_bf16, b1, w2_bf16, b2):
    n_rows, d = xs.shape
    ne, _, de2 = w1_bf16.shape
    tmx = EXPERT_TILE
    grid_spec = pltpu.PrefetchScalarGridSpec(
        num_scalar_prefetch=1,
        grid=(n_rows // tmx,),
        in_specs=[pl.BlockSpec((tmx, d), lambda i, te: (i, 0)),
                  pl.BlockSpec((1, d, de2), lambda i, te: (te[i], 0, 0)),
                  pl.BlockSpec((1, 1, de2), lambda i, te: (te[i], 0, 0)),
                  pl.BlockSpec((1, de2 // 2, d), lambda i, te: (te[i], 0, 0)),
                  pl.BlockSpec((1, 1, d), lambda i, te: (te[i], 0, 0))],
        out_specs=pl.BlockSpec((tmx, d), lambda i, te: (i, 0)),
    )
    return pl.pallas_call(
        _experts_kernel,
        grid_spec=grid_spec,
        out_shape=jax.ShapeDtypeStruct((n_rows, d), F32),
        compiler_params=_cparams(("arbitrary",)),
        name="moe_experts",
    )(tile_expert, xs, w1_bf16, b1.reshape(ne, 1, de2), w2_bf16, b2.reshape(ne, 1, d))


def _combine_kernel(dest_ref, x_ref, wgt_ref, ml_ref, mc_ref, ys_ref, xo_ref, rows_scr, sem,
                    *, tm, seg, ctx_len):
    def row_copy(j, k):
        src = dest_ref[0, 0, k * tm + j]
        return pltpu.make_async_copy(ys_ref.at[pl.ds(src, 1), :],
                                     rows_scr.at[k, pl.ds(j, 1), :], sem)

    def issue(j, carry):
        for k in range(TOP_K):
            row_copy(j, k).start()
        return carry

    lax.fori_loop(0, tm, issue, 0)

    def drain(j, carry):
        for k in range(TOP_K):
            row_copy(j, k).wait()
        return carry

    lax.fori_loop(0, tm, drain, 0)

    wgt = wgt_ref[...]
    y = None
    for k in range(TOP_K):
        term = wgt[:, k:k + 1] * rows_scr[k]
        y = term if y is None else y + term
    is_ctx = _ctx_rows(pl.program_id(0), tm, seg, ctx_len)
    gate = jnp.where(is_ctx, mc_ref[0][5:6], ml_ref[0][5:6])
    xo_ref[...] = x_ref[...] + gate * y


def _combine(ys, dest_tiles, x, wgt_t, mod, *, n_batch, seg, ctx_len):
    ntok, d = x.shape
    tm = dest_tiles.shape[-1] // TOP_K
    per = seg // tm
    kern = functools.partial(_combine_kernel, tm=tm, seg=seg, ctx_len=ctx_len)
    tok = pl.BlockSpec((tm, d), lambda i: (i, 0))
    return pl.pallas_call(
        kern,
        grid=(ntok // tm,),
        in_specs=[pl.BlockSpec((1, 1, TOP_K * tm), lambda i: (i, 0, 0), memory_space=pltpu.SMEM),
                  tok,
                  pl.BlockSpec((tm, TOP_K), lambda i: (i, 0)),
                  pl.BlockSpec((1, 6, d), lambda i: (i // per, 0, 0)),
                  pl.BlockSpec((1, 6, d), lambda i: (n_batch, 0, 0)),
                  pl.BlockSpec(memory_space=pl.ANY)],
        out_specs=tok,
        out_shape=jax.ShapeDtypeStruct((ntok, d), F32),
        scratch_shapes=[pltpu.VMEM((TOP_K, tm, d), F32), pltpu.SemaphoreType.DMA(())],
        compiler_params=_cparams(("arbitrary",)),
        name="moe_combine",
    )(dest_tiles, x, wgt_t, mod, mod, ys)


def _route_plan(idx, rank, counts, tm):
    tmx = EXPERT_TILE
    ntok = idx.shape[1]
    cnt = counts[:, 0].astype(jnp.int32)
    padded = ((cnt + tmx - 1) // tmx) * tmx
    ends = jnp.cumsum(padded)
    starts = ends - padded
    experts = jnp.arange(N_EXPERTS, dtype=jnp.int32)
    dest = rank + jnp.sum(jnp.where(idx[..., None] == experts, starts, 0), axis=-1)
    n_tiles = (TOP_K * ntok) // tmx + N_EXPERTS
    tile_start = jnp.arange(n_tiles, dtype=jnp.int32) * tmx
    tile_expert = jnp.minimum(jnp.sum((tile_start[:, None] >= ends[None, :]).astype(jnp.int32), axis=1),
                              N_EXPERTS - 1)
    nt = ntok // tm
    dest_tiles = dest.reshape(TOP_K, nt, tm).transpose(1, 0, 2).reshape(nt, 1, TOP_K * tm)
    return dest_tiles, tile_expert, n_tiles * tmx


def _final_norm_kernel(x_ref, g_ref, o_ref):
    x = x_ref[...]
    o_ref[0] = x * lax.rsqrt(jnp.mean(x * x, axis=-1, keepdims=True) + NORM_EPS) * g_ref[...]


def _final_norm(x, g, *, n_batch, seg, ctx_len):
    d = x.shape[1]
    n_lat = seg - ctx_len
    tm = _tile(math.gcd(ctx_len, n_lat), 512)
    per = n_lat // tm
    return pl.pallas_call(
        _final_norm_kernel,
        grid=(n_batch, per),
        in_specs=[pl.BlockSpec((tm, d), lambda b, j: (b * (seg // tm) + ctx_len // tm + j, 0)),
                  pl.BlockSpec((1, d), lambda b, j: (0, 0))],
        out_specs=pl.BlockSpec((1, tm, d), lambda b, j: (b, j, 0)),
        out_shape=jax.ShapeDtypeStruct((n_batch, n_lat, d), F32),
        compiler_params=_cparams(("arbitrary", "arbitrary")),
        name="final_norm",
    )(x, g.reshape(1, d))


def kernel(x, c, ctx, c_ctx, norm_mix_g, norm_ffn_g, w_mod, b_mod, w_in, dn_conv, dn_a_log, dn_dt_bias, dn_norm_g, rw_mu, rw_w0, rw_w2, rw_a0, rw_a2, rw_g2, rw_k_k, rw_k_a, rw_r_k, rw_ln_g, rw_ln_b, ssm_conv, ssm_conv_b, ssm_a_log, ssm_dt_bias, ssm_d, ssm_norm_g, w_branch, w_out, w_router, b_router, w_e1, b_e1, w_e2, b_e2, final_norm_g):
    n_batch, seq, d = x.shape
    ctx_len = ctx.shape[1]
    seg = seq + ctx_len
    n_layers = w_in.shape[0]
    assert seq % GRID_W == 0 and ctx_len % SSM_CHUNK == 0 and seq % SSM_CHUNK == 0
    off = _p_layout(d)
    dims = dict(n_batch=n_batch, seg=seg, ctx_len=ctx_len)

    mod_rows = -(-(n_batch + 1) // 8) * 8
    cond = jnp.concatenate([c, c_ctx[None], jnp.zeros((mod_rows - n_batch - 1, d), F32)], axis=0)
    mod_all = _modulation(cond, w_mod, b_mod).reshape(n_layers, mod_rows, 6, d)

    xa = jnp.concatenate([ctx, x], axis=1).reshape(n_batch * seg, d)
    route_tile = _tile(seg, ROUTE_TILE, LANES)
    for i in range(n_layers):
        mod = mod_all[i]
        p, p_small = _in_projection(xa, norm_mix_g[i], mod, _pack_w_in(w_in[i], d), **dims)
        small = _small_prep(p_small, dn_a_log[i], dn_dt_bias[i], ssm_a_log[i], ssm_dt_bias[i])

        qkv = _dn_conv(p, dn_conv[i], col0=off["dn_qkv"], **dims)
        oa = _dn_readout(*_dn_scan(qkv, small, **dims), p, dn_norm_g[i], col_gate=off["dn_gate"])

        shifted = _rw_shift(p, _pack_rw_vec(rw_mu[i]), col0=off["rw_rkv"], **dims)
        r, v, nk, gate, bonus, lw, kd, b = _rw_prep(
            shifted, rw_w0[i], rw_w2[i], rw_a0[i], rw_a2[i], rw_g2[i], rw_k_k[i], rw_k_a[i],
            rw_r_k[i])
        ob = _rw_readout(*_rw_scan(r, v, nk, lw, kd, b, **dims), bonus, gate, rw_ln_g[i],
                         rw_ln_b[i])

        xbc = _conv_silu(p, ssm_conv[i], ssm_conv_b[i], col0=off["ssm_xbc"], n_norm_blocks=0,
                         name="ssm_conv", **dims)
        oc = _ssd_readout(_ssd_scan(xbc, small, **dims), xbc, p, ssm_d[i], ssm_norm_g[i],
                          col_z=off["ssm_z"])

        y = _branch_mix(oa, ob, oc, p, w_branch[i].astype(BF16), d_model=d)
        xa, f, idx, wgt, rank, counts = _outproj_router(
            y, xa, mod, w_out[i].astype(BF16), norm_ffn_g[i], w_router[i], b_router[i], **dims)
        dest_tiles, tile_expert, n_rows = _route_plan(idx, rank, counts, route_tile)
        xs = _dispatch(f, dest_tiles, n_rows)
        ys = _experts(xs, tile_expert, w_e1[i].astype(BF16), b_e1[i], w_e2[i].astype(BF16), b_e2[i])
        xa = _combine(ys, dest_tiles, xa, wgt.T, mod, **dims)
    return _final_norm(xa, final_norm_g, **dims)
```

```python
import functools
import math

import jax
import jax.numpy as jnp
from jax import lax
from jax.experimental import pallas as pl
from jax.experimental.pallas import tpu as pltpu

F32 = jnp.float32
BF16 = jnp.bfloat16

GRID_W = 64
NORM_EPS = 1e-6
L2_EPS = 1e-6

DN_HEADS = 8
DN_DK = 128
DN_CHUNK = 64
DN_QK = DN_HEADS * DN_DK
DN_V = DN_QK

RW_HEADS = 16
RW_HEAD = 64
RW_WIDTH = RW_HEADS * RW_HEAD
RW_DECAY_LORA = 96
RW_A_LORA = 96
RW_GATE_LORA = 256
RW_GN_EPS = 64e-5
RW_CHUNK = 64

SSM_HEADS = 16
SSM_HEADDIM = 64
SSM_INNER = SSM_HEADS * SSM_HEADDIM
SSM_GROUPS = 4
SSM_STATE = 128
SSM_CHUNK = 128
SSM_XBC = SSM_INNER + 2 * SSM_GROUPS * SSM_STATE
SSM_NORM_EPS = 1e-5

N_EXPERTS = 32
TOP_K = 4
D_EXPERT = 512
SWIGLU_LIMIT = 7.0
SWIGLU_ALPHA = 1.702

V7X_VMEM_LIMIT = 56 * 1024 * 1024
LANES = 128

SCAN_OUT_DTYPE = jnp.bfloat16

IN_TILE_N = 512
IN_SUB_ROWS = 384


def _p_layout(d_model):
    off = {}
    pos = 0
    for name, width in (("gate", 3 * d_model), ("dn_qkv", 2 * DN_QK + DN_V), ("dn_gate", DN_V),
                        ("rw_rkv", 3 * RW_WIDTH), ("rw_lora", 1024), ("ssm_z", SSM_INNER),
                        ("ssm_xbc", SSM_XBC), ("small", IN_TILE_N)):
        off[name] = pos
        pos += width
    off["main"] = off["small"]
    off["total"] = pos
    return off


RWL_W, RWL_A, RWL_G = 0, 256, 512
SM_BETA, SM_A, SM_DT = 0, 16, 32


def _cparams(sem):
    return pltpu.CompilerParams(dimension_semantics=sem, vmem_limit_bytes=V7X_VMEM_LIMIT)


def _tile(n, cap, mult=8):
    best = None
    for t in range(mult, min(n, cap) + 1, mult):
        if n % t == 0:
            best = t
    assert best is not None, (n, cap)
    return best


def _dot(a, b):
    return jnp.dot(a.astype(BF16), b.astype(BF16), preferred_element_type=F32)


def _dot_nt(a, b):
    return lax.dot_general(a.astype(BF16), b.astype(BF16), (((1,), (1,)), ((), ())),
                           preferred_element_type=F32)


def _dot_tn(a, b):
    return lax.dot_general(a.astype(BF16), b.astype(BF16), (((0,), (0,)), ((), ())),
                           preferred_element_type=F32)


def _split2(a):
    hi = a.astype(BF16)
    lo = (a - hi.astype(F32)).astype(BF16)
    return hi, lo


def _dot_exact_lhs(m_bf16, x):
    x1 = x.astype(BF16)
    r1 = x - x1.astype(F32)
    x2 = r1.astype(BF16)
    x3 = (r1 - x2.astype(F32)).astype(BF16)
    d = functools.partial(jnp.dot, preferred_element_type=F32)
    return d(m_bf16, x1) + d(m_bf16, x2) + d(m_bf16, x3)


def _dot_exact_rhs(x, m_bf16):
    x1 = x.astype(BF16)
    r1 = x - x1.astype(F32)
    x2 = r1.astype(BF16)
    x3 = (r1 - x2.astype(F32)).astype(BF16)
    d = functools.partial(jnp.dot, preferred_element_type=F32)
    return d(x1, m_bf16) + d(x2, m_bf16) + d(x3, m_bf16)


def _dot_hilo_rhs(x, m_bf16):
    hi, lo = _split2(x)
    d = functools.partial(jnp.dot, preferred_element_type=F32)
    return d(hi, m_bf16) + d(lo, m_bf16)


def _sigmoid(x):
    return 1.0 / (1.0 + jnp.exp(-x))


def _silu(x):
    return x * _sigmoid(x)


def _softplus(x):
    return jnp.maximum(x, 0.0) + jnp.log(1.0 + jnp.exp(-jnp.abs(x)))


def _inv_unit_lower(neg_lowers, size):
    shape = neg_lowers[0].shape
    ri = lax.broadcasted_iota(jnp.int32, shape, 0)
    ci = lax.broadcasted_iota(jnp.int32, shape, 1)
    eye = jnp.where(ri == ci, 1.0, 0.0)
    ps = [eye + n for n in neg_lowers]
    qs = list(neg_lowers)
    power = 2
    while power < size:
        qs = [_dot(q, q) for q in qs]
        ps = [p + _dot(p, q) for p, q in zip(ps, qs)]
        power *= 2
    return ps


def _mod_kernel(c_ref, w_ref, b_ref, o_ref):
    c = c_ref[...]
    s = _silu(c)
    o_ref[0] = jnp.dot(s, w_ref[0], preferred_element_type=F32,
                       precision=lax.Precision.HIGHEST) + b_ref[0]


def _modulation(cond, w_mod, b_mod):
    n_layers, d, n6 = w_mod.shape
    rows = cond.shape[0]
    tn = _tile(n6, 1024, LANES)
    return pl.pallas_call(
        _mod_kernel,
        grid=(n_layers, n6 // tn),
        in_specs=[pl.BlockSpec((rows, d), lambda l, n: (0, 0)),
                  pl.BlockSpec((1, d, tn), lambda l, n: (l, 0, n)),
                  pl.BlockSpec((1, 1, tn), lambda l, n: (l, 0, n))],
        out_specs=pl.BlockSpec((1, rows, tn), lambda l, n: (l, 0, n)),
        out_shape=jax.ShapeDtypeStruct((n_layers, rows, n6), F32),
        compiler_params=_cparams(("arbitrary", "arbitrary")),
        name="modulation",
    )(cond, w_mod, b_mod.reshape(n_layers, 1, n6))


def _modulated_norm(x, g, ml, mc, is_ctx, shift_row, scale_row):
    y = x * lax.rsqrt(jnp.mean(x * x, axis=-1, keepdims=True) + NORM_EPS) * g
    shift = jnp.where(is_ctx, mc[shift_row:shift_row + 1], ml[shift_row:shift_row + 1])
    scale = jnp.where(is_ctx, mc[scale_row:scale_row + 1], ml[scale_row:scale_row + 1])
    return y * (1.0 + scale) + shift


def _ctx_rows(tile_index, tm, seg, ctx_len):
    row = (tile_index * tm) % seg + lax.broadcasted_iota(jnp.int32, (tm, 1), 0)
    return row < ctx_len


def _inproj_kernel(x_ref, g_ref, ml_ref, mc_ref, w_ref, o_ref, s_ref, h_scr,
                   *, tm, seg, ctx_len, n_main):
    n = pl.program_id(1)

    @pl.when(n == 0)
    def _():
        is_ctx = _ctx_rows(pl.program_id(0), tm, seg, ctx_len)
        rows = _tile(tm, 384)
        for r0 in range(0, tm, rows):
            h = _modulated_norm(x_ref[r0:r0 + rows, :], g_ref[...], ml_ref[0], mc_ref[0],
                                is_ctx[r0:r0 + rows], 0, 1)
            h_scr[r0:r0 + rows, :] = h.astype(BF16)

    def project(dst_ref):
        sub = _tile(tm, IN_SUB_ROWS, 16)
        for r0 in range(0, tm, sub):
            dst_ref[r0:r0 + sub, :] = jnp.dot(h_scr[r0:r0 + sub, :], w_ref[...],
                                              preferred_element_type=F32).astype(dst_ref.dtype)

    @pl.when(n < n_main)
    def _():
        project(o_ref)

    @pl.when(n == n_main)
    def _():
        project(s_ref)


def _in_projection(x, g, mod, w_packed, *, n_batch, seg, ctx_len):
    ntok, d = x.shape
    tn = IN_TILE_N
    n_main = w_packed.shape[1] // tn - 1
    tm = _tile(seg, 1152, 16)
    per = seg // tm
    kern = functools.partial(_inproj_kernel, tm=tm, seg=seg, ctx_len=ctx_len, n_main=n_main)
    return pl.pallas_call(
        kern,
        grid=(ntok // tm, n_main + 1),
        in_specs=[pl.BlockSpec((tm, d), lambda i, n: (i, 0)),
                  pl.BlockSpec((1, d), lambda i, n: (0, 0)),
                  pl.BlockSpec((1, 6, d), lambda i, n: (i // per, 0, 0)),
                  pl.BlockSpec((1, 6, d), lambda i, n: (n_batch, 0, 0)),
                  pl.BlockSpec((d, tn), lambda i, n: (0, n))],
        out_specs=[pl.BlockSpec((tm, tn), lambda i, n: (i, jnp.minimum(n, n_main - 1))),
                   pl.BlockSpec((tm, tn), lambda i, n: (i, 0))],
        out_shape=[jax.ShapeDtypeStruct((ntok, n_main * tn), BF16),
                   jax.ShapeDtypeStruct((ntok, tn), F32)],
        scratch_shapes=[pltpu.VMEM((tm, d), BF16)],
        compiler_params=_cparams(("arbitrary", "arbitrary")),
        name="in_projection",
    )(x, g.reshape(1, d), mod, mod, w_packed)


def _pack_w_in(w_in_l, d_model):
    gate_cols = 3 * d_model
    dn0 = gate_cols
    dn_cols = 2 * DN_QK + 2 * DN_V + 4 * DN_HEADS
    rw0 = dn0 + dn_cols
    rw_cols = 3 * RW_WIDTH + 2 * RW_DECAY_LORA + 2 * RW_A_LORA + RW_GATE_LORA
    ss0 = rw0 + rw_cols
    z = lambda n: jnp.zeros((d_model, n), w_in_l.dtype)
    c = lambda a, n: w_in_l[:, a:a + n]
    dn_small = dn0 + 2 * DN_QK + 2 * DN_V
    rw_l = rw0 + 3 * RW_WIDTH
    parts = [
        c(0, gate_cols),
        c(dn0, 2 * DN_QK + DN_V),
        c(dn0 + 2 * DN_QK + DN_V, DN_V),
        c(rw0, 3 * RW_WIDTH),
        c(rw_l, 2 * RW_DECAY_LORA), z(64),
        c(rw_l + 2 * RW_DECAY_LORA, 2 * RW_A_LORA), z(64),
        c(rw_l + 2 * RW_DECAY_LORA + 2 * RW_A_LORA, RW_GATE_LORA), z(256),
        c(ss0, SSM_INNER),
        c(ss0 + SSM_INNER, SSM_XBC),
        c(dn_small, 4 * DN_HEADS), c(ss0 + SSM_INNER + SSM_XBC, 2 * SSM_HEADS), z(512 - 64),
    ]
    return jnp.concatenate(parts, axis=1).astype(BF16)


CONV_PAD = 72


def _conv_rows(seg, ctx_len):
    ctx0 = CONV_PAD
    lat0 = ctx0 + ctx_len + CONV_PAD
    total = lat0 + (seg - ctx_len) + CONV_PAD
    return ctx0, lat0, total


def _fill_padded(scr, x, seg, ctx_len):
    c = x.shape[1]
    n_lat = seg - ctx_len
    ctx0, lat0, total = _conv_rows(seg, ctx_len)
    xc = x[:ctx_len]
    xl = x[ctx_len:]
    col = lax.broadcasted_iota(jnp.int32, (n_lat, c), 0) % GRID_W
    zpad = jnp.zeros((CONV_PAD, c), F32)
    for i, lat in enumerate((xl, jnp.where(col == GRID_W - 1, 0.0, xl), jnp.where(col == 0, 0.0, xl))):
        scr[i, pl.ds(0, CONV_PAD), :] = zpad
        scr[i, pl.ds(ctx0, ctx_len), :] = xc
        scr[i, pl.ds(ctx0 + ctx_len, CONV_PAD), :] = zpad
        scr[i, pl.ds(lat0, n_lat), :] = lat
        scr[i, pl.ds(lat0 + n_lat, CONV_PAD), :] = zpad


def _dwconv_block(scr, w9, seg, ctx_len):
    n_lat = seg - ctx_len
    ctx0, lat0, _ = _conv_rows(seg, ctx_len)
    acc_c = None
    acc_l = None
    for dc in (-1, 0, 1):
        src = {-1: 1, 0: 0, 1: 2}[dc]
        k = 3 + (dc + 1)
        term = scr[0, pl.ds(ctx0 + dc, ctx_len), :] * w9[k:k + 1]
        acc_c = term if acc_c is None else acc_c + term
        for dr in (-1, 0, 1):
            k = (dr + 1) * 3 + (dc + 1)
            term = scr[src, pl.ds(lat0 + dr * GRID_W + dc, n_lat), :] * w9[k:k + 1]
            acc_l = term if acc_l is None else acc_l + term
    return acc_c, acc_l


def _conv_kernel(x_ref, w_ref, b_ref, o_ref, scr, *, seg, ctx_len, n_norm_blocks):
    _fill_padded(scr, x_ref[...].astype(F32), seg, ctx_len)
    do_norm = pl.program_id(1) < n_norm_blocks
    for part, rows in zip(_dwconv_block(scr, w_ref[...], seg, ctx_len),
                          (pl.ds(0, ctx_len), pl.ds(ctx_len, seg - ctx_len))):
        y = _silu(part + b_ref[...])
        yn = y * lax.rsqrt(jnp.sum(y * y, axis=-1, keepdims=True) + L2_EPS)
        o_ref[rows, :] = jnp.where(do_norm, yn, y).astype(o_ref.dtype)


def _conv_silu(p, conv_w, conv_b, *, n_batch, seg, ctx_len, col0, n_norm_blocks, name):
    ntok = p.shape[0]
    cw = conv_w.shape[-1]
    kern = functools.partial(_conv_kernel, seg=seg, ctx_len=ctx_len, n_norm_blocks=n_norm_blocks)
    return pl.pallas_call(
        kern,
        grid=(n_batch, cw // LANES),
        in_specs=[pl.BlockSpec((seg, LANES), lambda b, j: (b, col0 // LANES + j)),
                  pl.BlockSpec((9, LANES), lambda b, j: (0, j)),
                  pl.BlockSpec((1, LANES), lambda b, j: (0, j))],
        out_specs=pl.BlockSpec((seg, LANES), lambda b, j: (b, j)),
        out_shape=jax.ShapeDtypeStruct((ntok, cw), BF16),
        scratch_shapes=[pltpu.VMEM((3, _conv_rows(seg, ctx_len)[2], LANES), F32)],
        compiler_params=_cparams(("arbitrary", "arbitrary")),
        name=name,
    )(p, conv_w.reshape(9, cw), conv_b.reshape(1, cw))


def _dn_conv(p, conv_w, *, n_batch, seg, ctx_len, col0):
    return _conv_silu(p, conv_w, jnp.zeros((conv_w.shape[-1],), F32), n_batch=n_batch, seg=seg,
                      ctx_len=ctx_len, col0=col0, n_norm_blocks=2 * DN_QK // LANES, name="dn_conv")


SO_BETA, SO_G, SO_DT, SO_A = 0, 16, 32, 64


def _small_kernel(x_ref, par_ref, o_ref):
    x = x_ref[...]
    par = par_ref[...]
    neg_a = -jnp.exp(par[1:2])
    tm = x.shape[0]
    beta = _sigmoid(x[:, SM_BETA:SM_BETA + 16])
    sp = _softplus(x[:, SM_A:SM_A + 48] + par[0:1, SM_A:SM_A + 48])
    g = sp[:, 0:16] * neg_a[:, SM_A:SM_A + 16]
    dt = sp[:, 16:48]
    a = dt * neg_a[:, SM_DT:SM_DT + 32]
    o_ref[...] = jnp.concatenate([beta, g, dt, a, jnp.zeros((tm, LANES - 96), F32)], axis=1)


def _small_prep(ps, dn_a_log, dn_dt_bias, ssm_a_log, ssm_dt_bias):
    ntok = ps.shape[0]
    tm = _tile(ntok, 1024)
    zeros16 = jnp.zeros((16,), F32)
    bias = jnp.concatenate([zeros16, dn_dt_bias.reshape(-1), ssm_dt_bias.reshape(-1),
                            jnp.zeros((LANES - 64,), F32)])
    a_log = jnp.concatenate([zeros16, dn_a_log.reshape(-1), ssm_a_log.reshape(-1),
                             jnp.zeros((LANES - 64,), F32)])
    par = jnp.concatenate([bias[None], a_log[None], jnp.zeros((6, LANES), F32)], axis=0)
    return pl.pallas_call(
        _small_kernel,
        grid=(ntok // tm,),
        in_specs=[pl.BlockSpec((tm, LANES), lambda i: (i, 0)),
                  pl.BlockSpec((8, LANES), lambda i: (0, 0))],
        out_specs=pl.BlockSpec((tm, LANES), lambda i: (i, 0)),
        out_shape=jax.ShapeDtypeStruct((ntok, LANES), F32),
        compiler_params=_cparams(("arbitrary",)),
        name="small_prep",
    )(ps, par)


def _chunk_of_step(d, s, n_ctx, n_all):
    bwd = jnp.where(s < n_ctx, n_ctx - 1 - s, n_all + n_ctx - 1 - s)
    return jnp.where(d == 0, s, bwd)


def _order_masks(d, size):
    ri = lax.broadcasted_iota(jnp.int32, (size, size), 0)
    ci = lax.broadcasted_iota(jnp.int32, (size, size), 1)
    delta = (ri - ci) * (1 - 2 * d)
    return delta >= 0, delta > 0, ri == ci


def _col_to_row(col, eye):
    return jnp.sum(jnp.where(eye, col, 0.0), axis=0, keepdims=True)


def _dn_scan_kernel(qf_ref, kf_ref, vf_ref, smf_ref, qb_ref, kb_ref, vb_ref, smb_ref,
                    of_ref, ob_ref, s_scr):
    c = DN_CHUNK
    h_n = DN_HEADS

    @pl.when(pl.program_id(1) == 0)
    def _():
        s_scr[...] = jnp.zeros(s_scr.shape, F32)

    scale = DN_DK ** -0.5
    refs = ((qf_ref, kf_ref, vf_ref, smf_ref, of_ref), (qb_ref, kb_ref, vb_ref, smb_ref, ob_ref))
    masks = [_order_masks(d, c) for d in range(2)]
    gcs = []
    gtots = []
    betas = []
    for d in range(2):
        sm = refs[d][3][...]
        betas.append(sm[:, SO_BETA + d * h_n:SO_BETA + (d + 1) * h_n])
        g = sm[:, SO_G + d * h_n:SO_G + (d + 1) * h_n]
        gcs.append(_dot_exact_lhs(masks[d][0].astype(BF16), g))
        gtots.append(jnp.sum(g, axis=0, keepdims=True))

    items = [(d, h) for d in range(2) for h in range(h_n)]
    n = range(len(items))
    lanes = [slice(h * DN_DK, (h + 1) * DN_DK) for _, h in items]
    incl = [masks[d][0] for d, _ in items]
    strict = [masks[d][1] for d, _ in items]
    eye = masks[0][2]
    qs = [refs[d][0][:, lanes[i]].astype(F32) * scale for i, (d, _) in enumerate(items)]
    ks = [refs[d][1][:, lanes[i]].astype(F32) for i, (d, _) in enumerate(items)]
    vs = [refs[d][2][:, lanes[i]].astype(F32) for i, (d, _) in enumerate(items)]
    bcols = [betas[d][:, h:h + 1] for d, h in items]
    gcols = [gcs[d][:, h:h + 1] for d, h in items]
    gts = [gtots[d][:, h:h + 1] for d, h in items]
    decs = [jnp.exp(jnp.where(incl[i], gcols[i] - _col_to_row(gcols[i], eye), 0.0)) for i in n]
    kbs = [ks[i] * bcols[i] for i in n]
    a_s = [_dot_nt(jnp.concatenate([kbs[i], qs[i]], axis=0), ks[i]) for i in n]
    atts = [jnp.where(incl[i], a_s[i][c:] * decs[i], 0.0) for i in n]
    tinvs = _inv_unit_lower([-jnp.where(strict[i], a_s[i][:c] * decs[i], 0.0) for i in n], c)
    egs = [jnp.exp(gcols[i]) for i in n]
    uws = [_dot(tinvs[i], jnp.concatenate([vs[i] * bcols[i], kbs[i] * egs[i]], axis=1))
           for i in n]
    s_olds = [s_scr[d, h] for d, h in items]
    wss = [_dot(jnp.concatenate([uws[i][:, DN_DK:], qs[i] * egs[i]], axis=0), s_olds[i])
           for i in n]
    v_news = [uws[i][:, :DN_DK] - wss[i][:c] for i in n]
    outs = [wss[i][c:] + _dot(atts[i], v_news[i]) for i in n]
    upds = [_dot_tn(ks[i] * jnp.exp(gts[i] - gcols[i]), v_news[i]) for i in n]
    for i, (d, h) in enumerate(items):
        s_scr[d, h] = s_olds[i] * jnp.exp(gts[i]) + upds[i]
        refs[d][4][:, lanes[i]] = outs[i].astype(refs[d][4].dtype)


def _scan_specs(c, widths, n_all, n_ctx):
    def spec(d, width, col):
        return pl.BlockSpec(
            (c, width), lambda b, s: (b * n_all + _chunk_of_step(d, s, n_ctx, n_all), col))
    return [spec(d, w, col) for d in range(2) for w, col in widths]


def _dn_scan(qkv, small, *, n_batch, seg, ctx_len):
    ntok = qkv.shape[0]
    c = DN_CHUNK
    n_all = seg // c
    n_ctx = ctx_len // c
    ins = _scan_specs(c, [(DN_QK, 0), (DN_QK, 1), (DN_V, 2), (LANES, 0)], n_all, n_ctx)
    outs = _scan_specs(c, [(DN_V, 0)], n_all, n_ctx)
    return pl.pallas_call(
        _dn_scan_kernel,
        grid=(n_batch, n_all),
        in_specs=ins,
        out_specs=outs,
        out_shape=[jax.ShapeDtypeStruct((ntok, DN_V), SCAN_OUT_DTYPE)] * 2,
        scratch_shapes=[pltpu.VMEM((2, DN_HEADS, DN_DK, DN_DK), F32)],
        compiler_params=_cparams(("arbitrary", "arbitrary")),
        name="dn_scan",
    )(qkv, qkv, qkv, small, qkv, qkv, qkv, small)


def _dn_readout_kernel(of_ref, ob_ref, gate_ref, g_ref, y_ref):
    o = of_ref[...].astype(F32) + ob_ref[...].astype(F32)
    tm = o.shape[0]
    gate = gate_ref[...].astype(F32)
    outs = []
    for h in range(DN_HEADS):
        lanes = slice(h * DN_DK, (h + 1) * DN_DK)
        oh = o[:, lanes]
        yh = oh * lax.rsqrt(jnp.mean(oh * oh, axis=-1, keepdims=True) + NORM_EPS) * g_ref[...]
        outs.append(yh * _silu(gate[:, lanes]))
    y_ref[...] = jnp.concatenate(outs, axis=1).astype(y_ref.dtype)


def _dn_readout(o_f, o_b, p, norm_g, *, col_gate):
    ntok = p.shape[0]
    tm = _tile(ntok, 512)
    tok = pl.BlockSpec((tm, DN_V), lambda i: (i, 0))
    return pl.pallas_call(
        _dn_readout_kernel,
        grid=(ntok // tm,),
        in_specs=[tok, tok,
                  pl.BlockSpec((tm, DN_V), lambda i: (i, col_gate // DN_V)),
                  pl.BlockSpec((1, DN_DK), lambda i: (0, 0))],
        out_specs=tok,
        out_shape=jax.ShapeDtypeStruct((ntok, DN_V), BF16),
        compiler_params=_cparams(("arbitrary",)),
        name="dn_readout",
    )(o_f, o_b, p, norm_g.reshape(1, DN_DK))


def _head_expander(n_heads, head_dim):
    hi = lax.broadcasted_iota(jnp.int32, (n_heads, n_heads * head_dim), 0)
    li = lax.broadcasted_iota(jnp.int32, (n_heads, n_heads * head_dim), 1)
    lo = hi * head_dim
    return jnp.where((li >= lo) & (li < lo + head_dim), 1.0, 0.0).astype(BF16)


def _ssd_scan_kernel(xf_ref, bf_ref, cf_ref, smf_ref, xb_ref, bb_ref, cb_ref, smb_ref,
                     yf_ref, yb_ref, s_scr):
    c = SSM_CHUNK
    hn = SSM_HEADS
    hd = SSM_HEADDIM
    per_group = hn // SSM_GROUPS
    gw = per_group * hd

    @pl.when(pl.program_id(1) == 0)
    def _():
        s_scr[...] = jnp.zeros(s_scr.shape, F32)

    refs = ((xf_ref, bf_ref, cf_ref, smf_ref, yf_ref), (xb_ref, bb_ref, cb_ref, smb_ref, yb_ref))
    expand = _head_expander(hn, hd)
    lane_lo = lax.broadcasted_iota(jnp.int32, (c, 2 * hd), 1) < hd
    pre = []
    for d in range(2):
        incl, _, eye = _order_masks(d, c)
        sm = refs[d][3][...]
        dt = sm[:, SO_DT + d * hn:SO_DT + (d + 1) * hn]
        a = sm[:, SO_A + d * hn:SO_A + (d + 1) * hn]
        acum = _dot_exact_lhs(incl.astype(BF16), a)
        acum_x = _dot_exact_rhs(acum, expand)
        atot_x = jnp.sum(_dot_exact_rhs(a, expand), axis=0, keepdims=True)
        xdt = refs[d][0][...].astype(F32) * _dot_exact_rhs(dt, expand)
        pre.append(dict(incl=incl, eye=eye, acum=acum, xdt=xdt,
                        xdec=xdt * jnp.exp(atot_x - acum_x),
                        eacum=jnp.exp(acum_x), etot=jnp.exp(atot_x)))

    dgs = [(d, g) for d in range(2) for g in range(SSM_GROUPS)]
    bgs = [refs[d][1][:, g * SSM_STATE:(g + 1) * SSM_STATE] for d, g in dgs]
    cgs = [refs[d][2][:, g * SSM_STATE:(g + 1) * SSM_STATE] for d, g in dgs]
    cbs = [_dot_nt(cgs[i], bgs[i]) for i in range(len(dgs))]
    ms = {}
    for d in range(2):
        p = pre[d]
        for h in range(hn):
            acol = p["acum"][:, h:h + 1]
            lmat = jnp.exp(jnp.where(p["incl"], acol - _col_to_row(acol, p["eye"]), 0.0))
            ms[d, h] = jnp.where(p["incl"], cbs[d * SSM_GROUPS + h // per_group] * lmat, 0.0)
    ydiag = {}
    for d in range(2):
        for pair in range(hn // 2):
            xp = pre[d]["xdt"][:, pair * 2 * hd:(pair + 1) * 2 * hd]
            ydiag[d, pair] = jnp.where(lane_lo, _dot(ms[d, 2 * pair], xp),
                                       _dot(ms[d, 2 * pair + 1], xp))
    s_olds = [s_scr[d, g] for d, g in dgs]
    yoffs = [_dot(cgs[i], s_olds[i]) for i in range(len(dgs))]
    sts = [_dot_tn(bgs[i], pre[d]["xdec"][:, g * gw:(g + 1) * gw])
           for i, (d, g) in enumerate(dgs)]
    for i, (d, g) in enumerate(dgs):
        lanes = slice(g * gw, (g + 1) * gw)
        s_scr[d, g] = s_olds[i] * pre[d]["etot"][:, lanes] + sts[i]
        yd = jnp.concatenate(
            [ydiag[d, pr] for pr in range(g * per_group // 2, (g + 1) * per_group // 2)], axis=1)
        refs[d][4][:, lanes] = (yd + yoffs[i] * pre[d]["eacum"][:, lanes]).astype(refs[d][4].dtype)


def _ssd_scan(xbc, small, *, n_batch, seg, ctx_len):
    ntok = xbc.shape[0]
    c = SSM_CHUNK
    n_all = seg // c
    n_ctx = ctx_len // c
    gn = SSM_GROUPS * SSM_STATE
    ins = _scan_specs(c, [(SSM_INNER, 0), (gn, SSM_INNER // gn), (gn, SSM_INNER // gn + 1),
                          (LANES, 0)], n_all, n_ctx)
    outs = _scan_specs(c, [(SSM_INNER, 0)], n_all, n_ctx)
    return pl.pallas_call(
        _ssd_scan_kernel,
        grid=(n_batch, n_all),
        in_specs=ins,
        out_specs=outs,
        out_shape=[jax.ShapeDtypeStruct((ntok, SSM_INNER), SCAN_OUT_DTYPE)] * 2,
        scratch_shapes=[pltpu.VMEM((2, SSM_GROUPS, SSM_STATE, SSM_INNER // SSM_GROUPS), F32)],
        compiler_params=_cparams(("arbitrary", "arbitrary")),
        name="ssd_scan",
    )(xbc, xbc, xbc, small, xbc, xbc, xbc, small)


def _ssd_readout_kernel(yf_ref, yb_ref, xs_ref, z_ref, d_ref, g_ref, o_ref):
    y = ((yf_ref[...].astype(F32) + yb_ref[...].astype(F32)
          + d_ref[...] * xs_ref[...].astype(F32)) * _silu(z_ref[...].astype(F32)))
    gw = SSM_INNER // SSM_GROUPS
    outs = []
    for g in range(SSM_GROUPS):
        lanes = slice(g * gw, (g + 1) * gw)
        yg = y[:, lanes]
        outs.append(yg * lax.rsqrt(jnp.mean(yg * yg, axis=-1, keepdims=True) + SSM_NORM_EPS)
                    * g_ref[:, lanes])
    o_ref[...] = jnp.concatenate(outs, axis=1).astype(o_ref.dtype)


def _ssd_readout(y_f, y_b, xbc, p, d_skip, norm_g, *, col_z):
    ntok = p.shape[0]
    tm = _tile(ntok, 512)
    w = SSM_INNER
    return pl.pallas_call(
        _ssd_readout_kernel,
        grid=(ntok // tm,),
        in_specs=[pl.BlockSpec((tm, w), lambda i: (i, 0)),
                  pl.BlockSpec((tm, w), lambda i: (i, 0)),
                  pl.BlockSpec((tm, w), lambda i: (i, 0)),
                  pl.BlockSpec((tm, w), lambda i: (i, col_z // w)),
                  pl.BlockSpec((1, w), lambda i: (0, 0)),
                  pl.BlockSpec((1, w), lambda i: (0, 0))],
        out_specs=pl.BlockSpec((tm, w), lambda i: (i, 0)),
        out_shape=jax.ShapeDtypeStruct((ntok, w), BF16),
        compiler_params=_cparams(("arbitrary",)),
        name="ssd_readout",
    )(y_f, y_b, xbc, p, jnp.repeat(d_skip, SSM_HEADDIM).reshape(1, w), norm_g.reshape(1, w))


def _rw_shift_kernel(x_ref, mu_ref, o_ref, scr, *, seg, ctx_len):
    x = x_ref[...].astype(F32)
    n_lat = seg - ctx_len
    _fill_padded(scr, x, seg, ctx_len)
    ctx0, lat0, _ = _conv_rows(seg, ctx_len)
    mu = mu_ref[...]
    lane_c = lax.broadcasted_iota(jnp.int32, (ctx_len, LANES), 1)
    prev = scr[0, pl.ds(ctx0 - 1, ctx_len), :]
    nxt = scr[0, pl.ds(ctx0 + 1, ctx_len), :]
    xc = x[:ctx_len]
    o_ref[pl.ds(0, ctx_len), :] = (xc + (jnp.where(lane_c % 2 == 0, prev, nxt) - xc) * mu
                                   ).astype(o_ref.dtype)
    sel = lax.broadcasted_iota(jnp.int32, (n_lat, LANES), 1) % 4
    left = scr[1, pl.ds(lat0 - 1, n_lat), :]
    right = scr[2, pl.ds(lat0 + 1, n_lat), :]
    up = scr[0, pl.ds(lat0 - GRID_W, n_lat), :]
    down = scr[0, pl.ds(lat0 + GRID_W, n_lat), :]
    shifted = jnp.where(sel == 0, left, jnp.where(sel == 1, right, jnp.where(sel == 2, up, down)))
    xl = x[ctx_len:]
    o_ref[pl.ds(ctx_len, n_lat), :] = (xl + (shifted - xl) * mu).astype(o_ref.dtype)


def _rw_shift(p, mu_packed, *, n_batch, seg, ctx_len, col0):
    ntok = p.shape[0]
    cw = mu_packed.shape[-1]
    kern = functools.partial(_rw_shift_kernel, seg=seg, ctx_len=ctx_len)
    return pl.pallas_call(
        kern,
        grid=(n_batch, cw // LANES),
        in_specs=[pl.BlockSpec((seg, LANES), lambda b, j: (b, col0 // LANES + j)),
                  pl.BlockSpec((1, LANES), lambda b, j: (0, j))],
        out_specs=pl.BlockSpec((seg, LANES), lambda b, j: (b, j)),
        out_shape=jax.ShapeDtypeStruct((ntok, cw), BF16),
        scratch_shapes=[pltpu.VMEM((3, _conv_rows(seg, ctx_len)[2], LANES), F32)],
        compiler_params=_cparams(("arbitrary", "arbitrary")),
        name="rw_shift",
    )(p, mu_packed.reshape(1, cw))


def _head_sum(x, n_heads, head_dim):
    expand = _head_expander(n_heads, head_dim)
    hi = lax.broadcasted_iota(jnp.int32, (n_heads * head_dim, n_heads), 1)
    li = lax.broadcasted_iota(jnp.int32, (n_heads * head_dim, n_heads), 0)
    lo = hi * head_dim
    reduce = jnp.where((li >= lo) & (li < lo + head_dim), 1.0, 0.0).astype(BF16)
    return _dot_hilo_rhs(_dot_hilo_rhs(x, reduce), expand)


def _rw_prep_kernel(x_ref, w0_ref, w2_ref, a0_ref, a2_ref, g2_ref, kk_ref, ka_ref, rk_ref,
                    r_ref, v_ref, nk_ref, gate_ref, bonus_ref, lw_ref, kd_ref, b_ref):
    w = RW_WIDTH
    r = x_ref[:, 0:w].astype(F32)
    k = x_ref[:, w:2 * w].astype(F32)
    v = x_ref[:, 2 * w:3 * w].astype(F32)
    lora = 3 * w
    wl = x_ref[:, lora + RWL_W:lora + RWL_W + 256].astype(F32)
    al = x_ref[:, lora + RWL_A:lora + RWL_A + 256]
    gl = x_ref[:, lora + RWL_G:lora + RWL_G + RW_GATE_LORA].astype(F32)
    w_raw = w0_ref[...] + _dot(jnp.tanh(wl), w2_ref[...])
    log_decay = -jnp.exp(-_softplus(-w_raw) - 0.5)
    icl = _sigmoid(a0_ref[...] + _dot(al, a2_ref[...]))
    gate_ref[...] = _dot(_sigmoid(gl), g2_ref[...]).astype(gate_ref.dtype)
    kx = k * kk_ref[...]
    nk = kx * lax.rsqrt(_head_sum(kx * kx, RW_HEADS, RW_HEAD) + L2_EPS)
    r_ref[...] = r.astype(r_ref.dtype)
    v_ref[...] = v.astype(v_ref.dtype)
    nk_ref[...] = nk.astype(nk_ref.dtype)
    ksum = None
    for d in range(2):
        icl_d = icl[:, d * w:(d + 1) * w]
        k_d = k * (1.0 + (icl_d - 1.0) * ka_ref[...])
        lw_ref[d] = log_decay[:, d * w:(d + 1) * w]
        kd_ref[d] = k_d.astype(kd_ref.dtype)
        b_ref[d] = (nk * icl_d).astype(b_ref.dtype)
        ksum = k_d if ksum is None else ksum + k_d
    bonus_ref[...] = (_head_sum(r * ksum * rk_ref[...], RW_HEADS, RW_HEAD) * v
                      ).astype(bonus_ref.dtype)


def _pack_rw_vec(vec):
    w3 = 3 * RW_WIDTH
    z = lambda n: jnp.zeros((n,), vec.dtype)
    l0 = w3 + 2 * RW_DECAY_LORA
    l1 = l0 + 2 * RW_A_LORA
    return jnp.concatenate([vec[:w3], vec[w3:l0], z(64), vec[l0:l1], z(64), vec[l1:], z(256)])


def _pack_lora(m2):
    lora, w = m2.shape[1], m2.shape[2]
    out = jnp.zeros((256, 2 * w), F32)
    out = out.at[0:lora, 0:w].set(m2[0])
    out = out.at[lora:2 * lora, w:2 * w].set(m2[1])
    return out.astype(BF16)


def _rw_prep(xs, w0, w2, a0, a2, g2, k_k, k_a, r_k):
    ntok = xs.shape[0]
    w = RW_WIDTH
    tm = _tile(ntok, 256)
    row = lambda a: a.reshape(1, -1)
    full = lambda shape: pl.BlockSpec(shape, lambda i: (0,) * len(shape))
    tok = pl.BlockSpec((tm, w), lambda i: (i, 0))
    tok2 = pl.BlockSpec((2, tm, w), lambda i: (0, i, 0))
    sd = jax.ShapeDtypeStruct
    return pl.pallas_call(
        _rw_prep_kernel,
        grid=(ntok // tm,),
        in_specs=[pl.BlockSpec((tm, 4 * w), lambda i: (i, 0)),
                  full((1, 2 * w)), full((256, 2 * w)), full((1, 2 * w)), full((256, 2 * w)),
                  full((RW_GATE_LORA, w)), full((1, w)), full((1, w)), full((1, w))],
        out_specs=[tok, tok, tok, tok, tok, tok2, tok2, tok2],
        out_shape=[sd((ntok, w), BF16)] * 5
        + [sd((2, ntok, w), F32), sd((2, ntok, w), BF16), sd((2, ntok, w), BF16)],
        compiler_params=_cparams(("arbitrary",)),
        name="rw_prep",
    )(xs, row(w0), _pack_lora(w2), row(a0), _pack_lora(a2), g2.astype(BF16), row(k_k), row(k_a),
      row(r_k))


def _stack_heads(x, lane_lo):
    return jnp.concatenate([jnp.where(lane_lo, x, 0.0), jnp.where(lane_lo, 0.0, x)], axis=0)


def _rw_scan_kernel(rf_ref, vf_ref, nkf_ref, lwf_ref, kdf_ref, bf_ref,
                    rb_ref, vb_ref, nkb_ref, lwb_ref, kdb_ref, bb_ref, yf_ref, yb_ref, h_scr):
    c = RW_CHUNK
    pw = 2 * RW_HEAD
    n_pairs = RW_HEADS // 2

    @pl.when(pl.program_id(1) == 0)
    def _():
        h_scr[...] = jnp.zeros(h_scr.shape, F32)

    refs = ((rf_ref, vf_ref, nkf_ref, lwf_ref, kdf_ref, bf_ref, yf_ref),
            (rb_ref, vb_ref, nkb_ref, lwb_ref, kdb_ref, bb_ref, yb_ref))
    rj = lax.broadcasted_iota(jnp.int32, (pw, pw), 0)
    cj = lax.broadcasted_iota(jnp.int32, (pw, pw), 1)
    eye_p = rj == cj
    lane_lo = lax.broadcasted_iota(jnp.int32, (c, pw), 1) < RW_HEAD
    ri = lax.broadcasted_iota(jnp.int32, (2 * c, 2 * c), 0) % c
    ci = lax.broadcasted_iota(jnp.int32, (2 * c, 2 * c), 1) % c

    per_dir = []
    for d in range(2):
        r_ref, v_ref, nk_ref, lw_ref, kd_ref, b_ref, _ = refs[d]
        incl, _, _ = _order_masks(d, c)
        lw = lw_ref[0]
        cw = _dot_exact_lhs(incl.astype(BF16), lw)
        tot = jnp.sum(lw, axis=0, keepdims=True)
        e_neg = jnp.exp(-cw)
        e_rem = jnp.exp(tot - cw)
        b = b_ref[0].astype(F32)
        kd = kd_ref[0].astype(F32)
        delta = (ri - ci) * (1 - 2 * d)
        incl2 = delta >= 0
        per_dir.append(dict(
            a_t=-nk_ref[...].astype(F32) * jnp.exp(cw - lw), r_t=r_ref[...].astype(F32) * jnp.exp(cw),
            b_t=b * e_neg, k_t=kd * e_neg, b_h=b * e_rem, k_h=kd * e_rem,
            v=v_ref[...].astype(F32), e_tot=jnp.exp(tot), strict2=delta > 0,
            mask_y=jnp.concatenate([incl2, incl2], axis=1)))

    items = [(d, p) for d in range(2) for p in range(n_pairs)]
    n = range(len(items))
    ln = [slice(p * pw, (p + 1) * pw) for _, p in items]
    dd = [per_dir[d] for d, _ in items]

    def stacked(i, top, bottom):
        return jnp.concatenate([_stack_heads(dd[i][top][:, ln[i]], lane_lo),
                                _stack_heads(dd[i][bottom][:, ln[i]], lane_lo)], axis=0)

    xs = [stacked(i, "a_t", "r_t") for i in n]
    ys = [stacked(i, "b_t", "k_t") for i in n]
    vst = [_stack_heads(dd[i]["v"][:, ln[i]], lane_lo) for i in n]
    gs = [_dot_nt(xs[i], ys[i]) for i in n]
    tinvs = _inv_unit_lower(
        [jnp.where(dd[i]["strict2"], gs[i][:2 * c, :2 * c], 0.0) for i in n], c)
    h_olds = [h_scr[d, p] for d, p in items]
    xh = [_dot(xs[i], h_olds[i]) for i in n]
    rhs = [xh[i][:2 * c] + _dot(jnp.where(dd[i]["strict2"], gs[i][:2 * c, 2 * c:], 0.0), vst[i])
           for i in n]
    us = [_dot(tinvs[i], rhs[i]) for i in n]
    uv = [jnp.concatenate([us[i], vst[i]], axis=0) for i in n]
    yst = [xh[i][2 * c:] + _dot(jnp.where(dd[i]["mask_y"], gs[i][2 * c:, :], 0.0), uv[i])
           for i in n]
    upd = [_dot_tn(stacked(i, "b_h", "k_h"), uv[i]) for i in n]
    for i, (d, p) in enumerate(items):
        e_col = jnp.sum(jnp.where(eye_p, dd[i]["e_tot"][:, ln[i]], 0.0), axis=1, keepdims=True)
        h_scr[d, p] = h_olds[i] * e_col + upd[i]
        refs[d][6][:, ln[i]] = (yst[i][:c] + yst[i][c:]).astype(refs[d][6].dtype)


def _rw_scan(r, v, nk, lw, kd, b, *, n_batch, seg, ctx_len):
    ntok = r.shape[0]
    c = RW_CHUNK
    w = RW_WIDTH
    n_all = seg // c
    n_ctx = ctx_len // c
    def row(d):
        return lambda bb, s: bb * n_all + _chunk_of_step(d, s, n_ctx, n_all)

    def tok(d):
        return pl.BlockSpec((c, w), lambda bb, s: (row(d)(bb, s), 0))

    def tok2(d):
        return pl.BlockSpec((1, c, w), lambda bb, s: (d, row(d)(bb, s), 0))

    return pl.pallas_call(
        _rw_scan_kernel,
        grid=(n_batch, n_all),
        in_specs=[spec(d) for d in range(2) for spec in (tok, tok, tok, tok2, tok2, tok2)],
        out_specs=[tok(0), tok(1)],
        out_shape=[jax.ShapeDtypeStruct((ntok, w), SCAN_OUT_DTYPE)] * 2,
        scratch_shapes=[pltpu.VMEM((2, RW_HEADS // 2, 2 * RW_HEAD, 2 * RW_HEAD), F32)],
        compiler_params=_cparams(("arbitrary", "arbitrary")),
        name="rw_scan",
    )(r, v, nk, lw, kd, b, r, v, nk, lw, kd, b)


def _rw_readout_kernel(yf_ref, yb_ref, bonus_ref, gate_ref, g_ref, b_ref, o_ref):
    inv_n = 1.0 / RW_HEAD
    tm = o_ref.shape[0]
    halves = [slice(0, tm // 2), slice(tm // 2, tm)]
    ys = [yf_ref[h, :].astype(F32) + yb_ref[h, :].astype(F32) for h in halves]
    means = [_head_sum(y, RW_HEADS, RW_HEAD) * inv_n for y in ys]
    ycs = [y - m for y, m in zip(ys, means)]
    vs = [_head_sum(yc * yc, RW_HEADS, RW_HEAD) * inv_n for yc in ycs]
    for h, yc, var in zip(halves, ycs, vs):
        yn = yc * lax.rsqrt(var + RW_GN_EPS) * g_ref[...] + b_ref[...]
        o_ref[h, :] = ((yn + bonus_ref[h, :].astype(F32)) * gate_ref[h, :].astype(F32)
                       ).astype(o_ref.dtype)


def _rw_readout(y_f, y_b, bonus, gate, ln_g, ln_b):
    ntok = bonus.shape[0]
    w = RW_WIDTH
    tm = _tile(ntok, 512)
    tok = pl.BlockSpec((tm, w), lambda i: (i, 0))
    vec = pl.BlockSpec((1, w), lambda i: (0, 0))
    return pl.pallas_call(
        _rw_readout_kernel,
        grid=(ntok // tm,),
        in_specs=[tok, tok, tok, tok, vec, vec],
        out_specs=tok,
        out_shape=jax.ShapeDtypeStruct((ntok, w), BF16),
        compiler_params=_cparams(("arbitrary",)),
        name="rw_readout",
    )(y_f, y_b, bonus, gate, ln_g.reshape(1, w), ln_b.reshape(1, w))


def _branch_mix_kernel(oa_ref, ob_ref, oc_ref, ga_ref, gb_ref, gc_ref, w_ref, y_ref):
    acc = None
    for i, (o_ref, g_ref) in enumerate(((oa_ref, ga_ref), (ob_ref, gb_ref), (oc_ref, gc_ref))):
        term = (_sigmoid(g_ref[...].astype(F32))
                * jnp.dot(o_ref[...], w_ref[i], preferred_element_type=F32))
        acc = term if acc is None else acc + term
    y_ref[...] = acc.astype(y_ref.dtype)


def _branch_mix(oa, ob, oc, p, w_branch_bf16, *, d_model):
    ntok = p.shape[0]
    bw = oa.shape[1]
    tm = _tile(ntok, 512)
    tn = 512
    nb = d_model // tn
    o_spec = pl.BlockSpec((tm, bw), lambda i, n: (i, 0))
    g_spec = lambda br: pl.BlockSpec((tm, tn), lambda i, n: (i, br * nb + n))
    return pl.pallas_call(
        _branch_mix_kernel,
        grid=(ntok // tm, nb),
        in_specs=[o_spec, o_spec, o_spec, g_spec(0), g_spec(1), g_spec(2),
                  pl.BlockSpec((3, bw, tn), lambda i, n: (0, 0, n))],
        out_specs=pl.BlockSpec((tm, tn), lambda i, n: (i, n)),
        out_shape=jax.ShapeDtypeStruct((ntok, d_model), BF16),
        compiler_params=_cparams(("arbitrary", "arbitrary")),
        name="branch_mix",
    )(oa, ob, oc, p, p, p, w_branch_bf16)


ROUTE_TILE = 256


def _dot3_nt(a, b):
    ah, al = _split2(a)
    bh, bl = _split2(b)
    d = functools.partial(lax.dot_general, dimension_numbers=(((1,), (1,)), ((), ())),
                          preferred_element_type=F32)
    return d(ah, bh) + d(ah, bl) + d(al, bh)


def _outproj_router_kernel(y_ref, x_ref, ml_ref, mc_ref, wo_ref, g_ref, wr_ref, br_ref,
                           xo_ref, f_ref, idx_ref, wgt_ref, rank_ref, cnt_ref, run_scr,
                           *, tm, seg, ctx_len):
    i = pl.program_id(0)

    @pl.when(i == 0)
    def _():
        run_scr[...] = jnp.zeros(run_scr.shape, F32)

    is_ctx = _ctx_rows(i, tm, seg, ctx_len)
    ml = ml_ref[0]
    mc = mc_ref[0]
    gate = jnp.where(is_ctx, mc[2:3], ml[2:3])
    halves = [slice(0, tm // 2), slice(tm // 2, tm)]
    x_news = [x_ref[h, :] + gate[h] * jnp.dot(y_ref[h, :], wo_ref[...], preferred_element_type=F32)
              for h in halves]
    fs = [_modulated_norm(x_news[j], g_ref[...], ml, mc, is_ctx[h], 3, 4)
          for j, h in enumerate(halves)]
    for j, h in enumerate(halves):
        xo_ref[h, :] = x_news[j]
        f_ref[h, :] = fs[j]

    ne = N_EXPERTS
    logits = jnp.concatenate([_dot3_nt(wr_ref[...], f) for f in fs], axis=1) + br_ref[...]
    eidx = lax.broadcasted_iota(jnp.int32, (ne, tm), 0)
    vals = logits
    picks = []
    tops = []
    hots = []
    for _ in range(TOP_K):
        m = jnp.max(vals, axis=0, keepdims=True)
        pick = jnp.min(jnp.where(vals == m, eidx, ne), axis=0, keepdims=True)
        hot = eidx == pick
        vals = jnp.where(hot, -jnp.inf, vals)
        tops.append(m)
        picks.append(pick)
        hots.append(hot)
    es = [jnp.exp(t - tops[0]) for t in tops]
    denom = es[0] + es[1] + es[2] + es[3]
    hot_all = jnp.where(hots[0] | hots[1] | hots[2] | hots[3], 1.0, 0.0)
    si = lax.broadcasted_iota(jnp.int32, (tm, tm), 0)
    ti = lax.broadcasted_iota(jnp.int32, (tm, tm), 1)
    earlier = jnp.where(si < ti, 1.0, 0.0).astype(BF16)
    before = jnp.dot(hot_all.astype(BF16), earlier, preferred_element_type=F32)
    rank = run_scr[:, 0:1] + before
    run_new = run_scr[...] + jnp.sum(hot_all, axis=1, keepdims=True)
    run_scr[...] = run_new
    cnt_ref[...] = run_new
    idx_ref[...] = jnp.concatenate(picks, axis=0)
    wgt_ref[...] = jnp.concatenate([e / denom for e in es], axis=0)
    rank_ref[...] = jnp.concatenate(
        [jnp.sum(jnp.where(h, rank, 0.0), axis=0, keepdims=True) for h in hots],
        axis=0).astype(jnp.int32)


def _outproj_router(y, x, mod, w_out_bf16, g_ffn, w_router, b_router, *, n_batch, seg, ctx_len):
    ntok, d = x.shape
    tm = _tile(seg, ROUTE_TILE, LANES)
    per = seg // tm
    ne = N_EXPERTS
    kern = functools.partial(_outproj_router_kernel, tm=tm, seg=seg, ctx_len=ctx_len)
    tok = pl.BlockSpec((tm, d), lambda i: (i, 0))
    sel = pl.BlockSpec((TOP_K, tm), lambda i: (0, i))
    sd = jax.ShapeDtypeStruct
    return pl.pallas_call(
        kern,
        grid=(ntok // tm,),
        in_specs=[tok, tok,
                  pl.BlockSpec((1, 6, d), lambda i: (i // per, 0, 0)),
                  pl.BlockSpec((1, 6, d), lambda i: (n_batch, 0, 0)),
                  pl.BlockSpec((d, d), lambda i: (0, 0)),
                  pl.BlockSpec((1, d), lambda i: (0, 0)),
                  pl.BlockSpec((ne, d), lambda i: (0, 0)),
                  pl.BlockSpec((ne, 1), lambda i: (0, 0))],
        out_specs=[tok, tok, sel, sel, sel, pl.BlockSpec((ne, LANES), lambda i: (0, 0))],
        out_shape=[sd((ntok, d), F32), sd((ntok, d), F32), sd((TOP_K, ntok), jnp.int32),
                   sd((TOP_K, ntok), F32), sd((TOP_K, ntok), jnp.int32), sd((ne, LANES), F32)],
        scratch_shapes=[pltpu.VMEM((ne, LANES), F32)],
        compiler_params=_cparams(("arbitrary",)),
        name="outproj_router",
    )(y, x, mod, mod, w_out_bf16, g_ffn.reshape(1, d), w_router.T, b_router.reshape(ne, 1))


EXPERT_TILE = 256


def _dispatch_kernel(dest_ref, f_ref, xs_in_ref, xs_ref, sem, *, tm):
    del xs_in_ref

    def row_copy(j, k):
        dst = dest_ref[0, 0, k * tm + j]
        return pltpu.make_async_copy(f_ref.at[pl.ds(j, 1), :], xs_ref.at[pl.ds(dst, 1), :], sem)

    def issue(j, carry):
        for k in range(TOP_K):
            row_copy(j, k).start()
        return carry

    lax.fori_loop(0, tm, issue, 0)

    def drain(j, carry):
        for k in range(TOP_K):
            row_copy(j, k).wait()
        return carry

    lax.fori_loop(0, tm, drain, 0)


def _dispatch(f, dest_tiles, n_rows):
    ntok, d = f.shape
    tm = dest_tiles.shape[-1] // TOP_K
    kern = functools.partial(_dispatch_kernel, tm=tm)
    return pl.pallas_call(
        kern,
        grid=(ntok // tm,),
        in_specs=[pl.BlockSpec((1, 1, TOP_K * tm), lambda i: (i, 0, 0), memory_space=pltpu.SMEM),
                  pl.BlockSpec((tm, d), lambda i: (i, 0)),
                  pl.BlockSpec(memory_space=pl.ANY)],
        out_specs=pl.BlockSpec(memory_space=pl.ANY),
        out_shape=jax.ShapeDtypeStruct((n_rows, d), F32),
        scratch_shapes=[pltpu.SemaphoreType.DMA(())],
        input_output_aliases={2: 0},
        compiler_params=_cparams(("arbitrary",)),
        name="moe_dispatch",
    )(dest_tiles, f, jnp.zeros((n_rows, d), F32))


def _experts_kernel(te_ref, xs_ref, w1_ref, b1_ref, w2_ref, b2_ref, ys_ref, w1_scr, w2_scr):
    i = pl.program_id(0)

    @pl.when((i == 0) | (te_ref[i] != te_ref[jnp.maximum(i - 1, 0)]))
    def _():
        w1_scr[...] = w1_ref[0, 0].astype(BF16)
        w2_scr[...] = w2_ref[0, 0].astype(BF16)

    hu = jnp.dot(xs_ref[...].astype(BF16), w1_scr[...], preferred_element_type=F32) + b1_ref[0]
    gt = jnp.minimum(hu[:, :D_EXPERT], SWIGLU_LIMIT)
    up = jnp.clip(hu[:, D_EXPERT:], -SWIGLU_LIMIT, SWIGLU_LIMIT)
    act = gt * _sigmoid(SWIGLU_ALPHA * gt) * (up + 1.0)
    ys_ref[...] = jnp.dot(act.astype(BF16), w2_scr[...], preferred_element_type=F32) + b2_ref[0]


def _experts(xs, tile_expert, w1_all, b1, w2_all, b2, *, layer):
    n_rows, d = xs.shape
    _, ne, _, de2 = w1_all.shape
    tmx = EXPERT_TILE
    grid_spec = pltpu.PrefetchScalarGridSpec(
        num_scalar_prefetch=1,
        grid=(n_rows // tmx,),
        in_specs=[pl.BlockSpec((tmx, d), lambda i, te: (i, 0)),
                  pl.BlockSpec((1, 1, d, de2), lambda i, te: (layer, te[i], 0, 0)),
                  pl.BlockSpec((1, 1, de2), lambda i, te: (te[i], 0, 0)),
                  pl.BlockSpec((1, 1, de2 // 2, d), lambda i, te: (layer, te[i], 0, 0)),
                  pl.BlockSpec((1, 1, d), lambda i, te: (te[i], 0, 0))],
        out_specs=pl.BlockSpec((tmx, d), lambda i, te: (i, 0)),
        scratch_shapes=[pltpu.VMEM((d, de2), BF16), pltpu.VMEM((de2 // 2, d), BF16)],
    )
    return pl.pallas_call(
        _experts_kernel,
        grid_spec=grid_spec,
        out_shape=jax.ShapeDtypeStruct((n_rows, d), F32),
        compiler_params=_cparams(("arbitrary",)),
        name="moe_experts",
    )(tile_expert, xs, w1_all, b1.reshape(ne, 1, de2), w2_all, b2.reshape(ne, 1, d))


def _combine_kernel(dest_ref, x_ref, wgt_ref, ml_ref, mc_ref, ys_ref, xo_ref, rows_scr, sem,
                    *, tm, seg, ctx_len):
    def row_copy(j, k):
        src = dest_ref[0, 0, k * tm + j]
        return pltpu.make_async_copy(ys_ref.at[pl.ds(src, 1), :],
                                     rows_scr.at[k, pl.ds(j, 1), :], sem)

    def issue(j, carry):
        for k in range(TOP_K):
            row_copy(j, k).start()
        return carry

    lax.fori_loop(0, tm, issue, 0)

    def drain(j, carry):
        for k in range(TOP_K):
            row_copy(j, k).wait()
        return carry

    lax.fori_loop(0, tm, drain, 0)

    wgt = wgt_ref[...]
    y = None
    for k in range(TOP_K):
        term = wgt[:, k:k + 1] * rows_scr[k]
        y = term if y is None else y + term
    is_ctx = _ctx_rows(pl.program_id(0), tm, seg, ctx_len)
    gate = jnp.where(is_ctx, mc_ref[0][5:6], ml_ref[0][5:6])
    xo_ref[...] = x_ref[...] + gate * y


def _combine(ys, dest_tiles, x, wgt_t, mod, *, n_batch, seg, ctx_len):
    ntok, d = x.shape
    tm = dest_tiles.shape[-1] // TOP_K
    per = seg // tm
    kern = functools.partial(_combine_kernel, tm=tm, seg=seg, ctx_len=ctx_len)
    tok = pl.BlockSpec((tm, d), lambda i: (i, 0))
    return pl.pallas_call(
        kern,
        grid=(ntok // tm,),
        in_specs=[pl.BlockSpec((1, 1, TOP_K * tm), lambda i: (i, 0, 0), memory_space=pltpu.SMEM),
                  tok,
                  pl.BlockSpec((tm, TOP_K), lambda i: (i, 0)),
                  pl.BlockSpec((1, 6, d), lambda i: (i // per, 0, 0)),
                  pl.BlockSpec((1, 6, d), lambda i: (n_batch, 0, 0)),
                  pl.BlockSpec(memory_space=pl.ANY)],
        out_specs=tok,
        out_shape=jax.ShapeDtypeStruct((ntok, d), F32),
        scratch_shapes=[pltpu.VMEM((TOP_K, tm, d), F32), pltpu.SemaphoreType.DMA(())],
        compiler_params=_cparams(("arbitrary",)),
        name="moe_combine",
    )(dest_tiles, x, wgt_t, mod, mod, ys)


def _route_plan(idx, rank, counts, tm):
    tmx = EXPERT_TILE
    ntok = idx.shape[1]
    cnt = counts[:, 0].astype(jnp.int32)
    padded = ((cnt + tmx - 1) // tmx) * tmx
    ends = jnp.cumsum(padded)
    starts = ends - padded
    experts = jnp.arange(N_EXPERTS, dtype=jnp.int32)
    dest = rank + jnp.sum(jnp.where(idx[..., None] == experts, starts, 0), axis=-1)
    n_tiles = (TOP_K * ntok) // tmx + N_EXPERTS
    tile_start = jnp.arange(n_tiles, dtype=jnp.int32) * tmx
    tile_expert = jnp.minimum(jnp.sum((tile_start[:, None] >= ends[None, :]).astype(jnp.int32), axis=1),
                              N_EXPERTS - 1)
    nt = ntok // tm
    dest_tiles = dest.reshape(TOP_K, nt, tm).transpose(1, 0, 2).reshape(nt, 1, TOP_K * tm)
    return dest_tiles, tile_expert, n_tiles * tmx


def _final_norm_kernel(x_ref, g_ref, o_ref):
    x = x_ref[...]
    o_ref[0] = x * lax.rsqrt(jnp.mean(x * x, axis=-1, keepdims=True) + NORM_EPS) * g_ref[...]


def _final_norm(x, g, *, n_batch, seg, ctx_len):
    d = x.shape[1]
    n_lat = seg - ctx_len
    tm = _tile(math.gcd(ctx_len, n_lat), 512)
    per = n_lat // tm
    return pl.pallas_call(
        _final_norm_kernel,
        grid=(n_batch, per),
        in_specs=[pl.BlockSpec((tm, d), lambda b, j: (b * (seg // tm) + ctx_len // tm + j, 0)),
                  pl.BlockSpec((1, d), lambda b, j: (0, 0))],
        out_specs=pl.BlockSpec((1, tm, d), lambda b, j: (b, j, 0)),
        out_shape=jax.ShapeDtypeStruct((n_batch, n_lat, d), F32),
        compiler_params=_cparams(("arbitrary", "arbitrary")),
        name="final_norm",
    )(x, g.reshape(1, d))


def kernel(x, c, ctx, c_ctx, norm_mix_g, norm_ffn_g, w_mod, b_mod, w_in, dn_conv, dn_a_log, dn_dt_bias, dn_norm_g, rw_mu, rw_w0, rw_w2, rw_a0, rw_a2, rw_g2, rw_k_k, rw_k_a, rw_r_k, rw_ln_g, rw_ln_b, ssm_conv, ssm_conv_b, ssm_a_log, ssm_dt_bias, ssm_d, ssm_norm_g, w_branch, w_out, w_router, b_router, w_e1, b_e1, w_e2, b_e2, final_norm_g):
    n_batch, seq, d = x.shape
    ctx_len = ctx.shape[1]
    seg = seq + ctx_len
    n_layers = w_in.shape[0]
    assert seq % GRID_W == 0 and ctx_len % SSM_CHUNK == 0 and seq % SSM_CHUNK == 0
    off = _p_layout(d)
    dims = dict(n_batch=n_batch, seg=seg, ctx_len=ctx_len)

    mod_rows = -(-(n_batch + 1) // 8) * 8
    cond = jnp.concatenate([c, c_ctx[None], jnp.zeros((mod_rows - n_batch - 1, d), F32)], axis=0)
    mod_all = _modulation(cond, w_mod, b_mod).reshape(n_layers, mod_rows, 6, d)

    xa = jnp.concatenate([ctx, x], axis=1).reshape(n_batch * seg, d)
    route_tile = _tile(seg, ROUTE_TILE, LANES)
    for i in range(n_layers):
        mod = mod_all[i]
        p, p_small = _in_projection(xa, norm_mix_g[i], mod, _pack_w_in(w_in[i], d), **dims)
        small = _small_prep(p_small, dn_a_log[i], dn_dt_bias[i], ssm_a_log[i], ssm_dt_bias[i])

        qkv = _dn_conv(p, dn_conv[i], col0=off["dn_qkv"], **dims)
        oa = _dn_readout(*_dn_scan(qkv, small, **dims), p, dn_norm_g[i], col_gate=off["dn_gate"])

        shifted = _rw_shift(p, _pack_rw_vec(rw_mu[i]), col0=off["rw_rkv"], **dims)
        r, v, nk, gate, bonus, lw, kd, b = _rw_prep(
            shifted, rw_w0[i], rw_w2[i], rw_a0[i], rw_a2[i], rw_g2[i], rw_k_k[i], rw_k_a[i],
            rw_r_k[i])
        ob = _rw_readout(*_rw_scan(r, v, nk, lw, kd, b, **dims), bonus, gate, rw_ln_g[i],
                         rw_ln_b[i])

        xbc = _conv_silu(p, ssm_conv[i], ssm_conv_b[i], col0=off["ssm_xbc"], n_norm_blocks=0,
                         name="ssm_conv", **dims)
        oc = _ssd_readout(*_ssd_scan(xbc, small, **dims), xbc, p, ssm_d[i], ssm_norm_g[i],
                          col_z=off["ssm_z"])

        y = _branch_mix(oa, ob, oc, p, w_branch[i].astype(BF16), d_model=d)
        xa, f, idx, wgt, rank, counts = _outproj_router(
            y, xa, mod, w_out[i].astype(BF16), norm_ffn_g[i], w_router[i], b_router[i], **dims)
        dest_tiles, tile_expert, n_rows = _route_plan(idx, rank, counts, route_tile)
        xs = _dispatch(f, dest_tiles, n_rows)
        ys = _experts(xs, tile_expert, w_e1, b_e1[i], w_e2, b_e2[i], layer=i)
        xa = _combine(ys, dest_tiles, xa, wgt.T, mod, **dims)
    return _final_norm(xa, final_norm_g, **dims)
```

```python
import functools
import math

import jax
import jax.numpy as jnp
from jax import lax
from jax.experimental import pallas as pl
from jax.experimental.pallas import tpu as pltpu

F32 = jnp.float32
BF16 = jnp.bfloat16

GRID_W = 64
NORM_EPS = 1e-6
L2_EPS = 1e-6

DN_HEADS = 8
DN_DK = 128
DN_CHUNK = 64
DN_QK = DN_HEADS * DN_DK
DN_V = DN_QK

RW_HEADS = 16
RW_HEAD = 64
RW_WIDTH = RW_HEADS * RW_HEAD
RW_DECAY_LORA = 96
RW_A_LORA = 96
RW_GATE_LORA = 256
RW_GN_EPS = 64e-5
RW_CHUNK = 64

SSM_HEADS = 16
SSM_HEADDIM = 64
SSM_INNER = SSM_HEADS * SSM_HEADDIM
SSM_GROUPS = 4
SSM_STATE = 128
SSM_CHUNK = 128
SSM_XBC = SSM_INNER + 2 * SSM_GROUPS * SSM_STATE
SSM_NORM_EPS = 1e-5

N_EXPERTS = 32
TOP_K = 4
D_EXPERT = 512
SWIGLU_LIMIT = 7.0
SWIGLU_ALPHA = 1.702

V7X_VMEM_LIMIT = 56 * 1024 * 1024
LANES = 128

SCAN_OUT_DTYPE = jnp.bfloat16

IN_TILE_N = 512
IN_SUB_ROWS = 384


def _p_layout(d_model):
    off = {}
    pos = 0
    for name, width in (("gate", 3 * d_model), ("dn_qkv", 2 * DN_QK + DN_V), ("dn_gate", DN_V),
                        ("rw_rkv", 3 * RW_WIDTH), ("rw_lora", 1024), ("ssm_z", SSM_INNER),
                        ("ssm_xbc", SSM_XBC), ("small", IN_TILE_N)):
        off[name] = pos
        pos += width
    off["main"] = off["small"]
    off["total"] = pos
    return off


RWL_W, RWL_A, RWL_G = 0, 256, 512
SM_BETA, SM_A, SM_DT = 0, 16, 32


def _cparams(sem):
    return pltpu.CompilerParams(dimension_semantics=sem, vmem_limit_bytes=V7X_VMEM_LIMIT)


def _tile(n, cap, mult=8):
    best = None
    for t in range(mult, min(n, cap) + 1, mult):
        if n % t == 0:
            best = t
    assert best is not None, (n, cap)
    return best


def _dot(a, b):
    return jnp.dot(a.astype(BF16), b.astype(BF16), preferred_element_type=F32)


def _dot_nt(a, b):
    return lax.dot_general(a.astype(BF16), b.astype(BF16), (((1,), (1,)), ((), ())),
                           preferred_element_type=F32)


def _dot_tn(a, b):
    return lax.dot_general(a.astype(BF16), b.astype(BF16), (((0,), (0,)), ((), ())),
                           preferred_element_type=F32)


def _split2(a):
    hi = a.astype(BF16)
    lo = (a - hi.astype(F32)).astype(BF16)
    return hi, lo


def _dot_exact_lhs(m_bf16, x):
    x1 = x.astype(BF16)
    r1 = x - x1.astype(F32)
    x2 = r1.astype(BF16)
    x3 = (r1 - x2.astype(F32)).astype(BF16)
    d = functools.partial(jnp.dot, preferred_element_type=F32)
    return d(m_bf16, x1) + d(m_bf16, x2) + d(m_bf16, x3)


def _dot_exact_rhs(x, m_bf16):
    x1 = x.astype(BF16)
    r1 = x - x1.astype(F32)
    x2 = r1.astype(BF16)
    x3 = (r1 - x2.astype(F32)).astype(BF16)
    d = functools.partial(jnp.dot, preferred_element_type=F32)
    return d(x1, m_bf16) + d(x2, m_bf16) + d(x3, m_bf16)


def _dot_hilo_rhs(x, m_bf16):
    hi, lo = _split2(x)
    d = functools.partial(jnp.dot, preferred_element_type=F32)
    return d(hi, m_bf16) + d(lo, m_bf16)


def _pack_bf16_pair(a, b):
    def rounded(x):
        u = lax.bitcast_convert_type(x, jnp.uint32)
        return u + jnp.uint32(0x7FFF) + ((u >> 16) & jnp.uint32(1))
    return (rounded(a) >> 16) | (rounded(b) & jnp.uint32(0xFFFF0000))


def _unpack_bf16_pair(p):
    lo = lax.bitcast_convert_type(p << 16, F32)
    hi = lax.bitcast_convert_type(p & jnp.uint32(0xFFFF0000), F32)
    return lo, hi


def _sigmoid(x):
    return 1.0 / (1.0 + jnp.exp(-x))


def _silu(x):
    return x * _sigmoid(x)


def _softplus(x):
    return jnp.maximum(x, 0.0) + jnp.log(1.0 + jnp.exp(-jnp.abs(x)))


def _inv_unit_lower(neg_lowers, size):
    shape = neg_lowers[0].shape
    ri = lax.broadcasted_iota(jnp.int32, shape, 0)
    ci = lax.broadcasted_iota(jnp.int32, shape, 1)
    eye = jnp.where(ri == ci, 1.0, 0.0)
    ps = [eye + n for n in neg_lowers]
    qs = list(neg_lowers)
    power = 2
    while power < size:
        qs = [_dot(q, q) for q in qs]
        ps = [p + _dot(p, q) for p, q in zip(ps, qs)]
        power *= 2
    return ps


def _mod_kernel(c_ref, w_ref, b_ref, o_ref):
    c = c_ref[...]
    s = _silu(c)
    o_ref[0] = jnp.dot(s, w_ref[0], preferred_element_type=F32,
                       precision=lax.Precision.HIGHEST) + b_ref[0]


def _modulation(cond, w_mod, b_mod):
    n_layers, d, n6 = w_mod.shape
    rows = cond.shape[0]
    tn = _tile(n6, 1024, LANES)
    return pl.pallas_call(
        _mod_kernel,
        grid=(n_layers, n6 // tn),
        in_specs=[pl.BlockSpec((rows, d), lambda l, n: (0, 0)),
                  pl.BlockSpec((1, d, tn), lambda l, n: (l, 0, n)),
                  pl.BlockSpec((1, 1, tn), lambda l, n: (l, 0, n))],
        out_specs=pl.BlockSpec((1, rows, tn), lambda l, n: (l, 0, n)),
        out_shape=jax.ShapeDtypeStruct((n_layers, rows, n6), F32),
        compiler_params=_cparams(("arbitrary", "arbitrary")),
        name="modulation",
    )(cond, w_mod, b_mod.reshape(n_layers, 1, n6))


def _modulated_norm(x, g, ml, mc, is_ctx, shift_row, scale_row):
    y = x * lax.rsqrt(jnp.mean(x * x, axis=-1, keepdims=True) + NORM_EPS) * g
    shift = jnp.where(is_ctx, mc[shift_row:shift_row + 1], ml[shift_row:shift_row + 1])
    scale = jnp.where(is_ctx, mc[scale_row:scale_row + 1], ml[scale_row:scale_row + 1])
    return y * (1.0 + scale) + shift


def _ctx_rows(tile_index, tm, seg, ctx_len):
    row = (tile_index * tm) % seg + lax.broadcasted_iota(jnp.int32, (tm, 1), 0)
    return row < ctx_len


def _inproj_kernel(x_ref, g_ref, ml_ref, mc_ref, w_ref, o_ref, s_ref, h_scr,
                   *, tm, seg, ctx_len, n_main):
    n = pl.program_id(1)

    @pl.when(n == 0)
    def _():
        is_ctx = _ctx_rows(pl.program_id(0), tm, seg, ctx_len)
        rows = _tile(tm, 384)
        for r0 in range(0, tm, rows):
            h = _modulated_norm(x_ref[r0:r0 + rows, :], g_ref[...], ml_ref[0], mc_ref[0],
                                is_ctx[r0:r0 + rows], 0, 1)
            h_scr[r0:r0 + rows, :] = h.astype(BF16)

    def project(dst_ref):
        sub = _tile(tm, IN_SUB_ROWS, 16)
        for r0 in range(0, tm, sub):
            dst_ref[r0:r0 + sub, :] = jnp.dot(h_scr[r0:r0 + sub, :], w_ref[...],
                                              preferred_element_type=F32).astype(dst_ref.dtype)

    @pl.when(n < n_main)
    def _():
        project(o_ref)

    @pl.when(n == n_main)
    def _():
        project(s_ref)


def _in_projection(x, g, mod, w_packed, *, n_batch, seg, ctx_len):
    ntok, d = x.shape
    tn = IN_TILE_N
    n_main = w_packed.shape[1] // tn - 1
    tm = _tile(seg, 1152, 16)
    per = seg // tm
    kern = functools.partial(_inproj_kernel, tm=tm, seg=seg, ctx_len=ctx_len, n_main=n_main)
    return pl.pallas_call(
        kern,
        grid=(ntok // tm, n_main + 1),
        in_specs=[pl.BlockSpec((tm, d), lambda i, n: (i, 0)),
                  pl.BlockSpec((1, d), lambda i, n: (0, 0)),
                  pl.BlockSpec((1, 6, d), lambda i, n: (i // per, 0, 0)),
                  pl.BlockSpec((1, 6, d), lambda i, n: (n_batch, 0, 0)),
                  pl.BlockSpec((d, tn), lambda i, n: (0, n))],
        out_specs=[pl.BlockSpec((tm, tn), lambda i, n: (i, jnp.minimum(n, n_main - 1))),
                   pl.BlockSpec((tm, tn), lambda i, n: (i, 0))],
        out_shape=[jax.ShapeDtypeStruct((ntok, n_main * tn), BF16),
                   jax.ShapeDtypeStruct((ntok, tn), F32)],
        scratch_shapes=[pltpu.VMEM((tm, d), BF16)],
        compiler_params=_cparams(("arbitrary", "arbitrary")),
        name="in_projection",
    )(x, g.reshape(1, d), mod, mod, w_packed)


def _pack_w_in(w_in_l, d_model):
    gate_cols = 3 * d_model
    dn0 = gate_cols
    dn_cols = 2 * DN_QK + 2 * DN_V + 4 * DN_HEADS
    rw0 = dn0 + dn_cols
    rw_cols = 3 * RW_WIDTH + 2 * RW_DECAY_LORA + 2 * RW_A_LORA + RW_GATE_LORA
    ss0 = rw0 + rw_cols
    z = lambda n: jnp.zeros((d_model, n), w_in_l.dtype)
    c = lambda a, n: w_in_l[:, a:a + n]
    dn_small = dn0 + 2 * DN_QK + 2 * DN_V
    rw_l = rw0 + 3 * RW_WIDTH
    parts = [
        c(0, gate_cols),
        c(dn0, 2 * DN_QK + DN_V),
        c(dn0 + 2 * DN_QK + DN_V, DN_V),
        c(rw0, 3 * RW_WIDTH),
        c(rw_l, 2 * RW_DECAY_LORA), z(64),
        c(rw_l + 2 * RW_DECAY_LORA, 2 * RW_A_LORA), z(64),
        c(rw_l + 2 * RW_DECAY_LORA + 2 * RW_A_LORA, RW_GATE_LORA), z(256),
        c(ss0, SSM_INNER),
        c(ss0 + SSM_INNER, SSM_XBC),
        c(dn_small, 4 * DN_HEADS), c(ss0 + SSM_INNER + SSM_XBC, 2 * SSM_HEADS), z(512 - 64),
    ]
    return jnp.concatenate(parts, axis=1).astype(BF16)


CONV_PAD = 72


def _conv_rows(seg, ctx_len):
    ctx0 = CONV_PAD
    lat0 = ctx0 + ctx_len + CONV_PAD
    total = lat0 + (seg - ctx_len) + CONV_PAD
    return ctx0, lat0, total


def _fill_padded(scr, x, seg, ctx_len):
    c = x.shape[1]
    n_lat = seg - ctx_len
    ctx0, lat0, total = _conv_rows(seg, ctx_len)
    xc = x[:ctx_len]
    xl = x[ctx_len:]
    col = lax.broadcasted_iota(jnp.int32, (n_lat, c), 0) % GRID_W
    zpad = jnp.zeros((CONV_PAD, c), F32)
    for i, lat in enumerate((xl, jnp.where(col == GRID_W - 1, 0.0, xl), jnp.where(col == 0, 0.0, xl))):
        scr[i, pl.ds(0, CONV_PAD), :] = zpad
        scr[i, pl.ds(ctx0, ctx_len), :] = xc
        scr[i, pl.ds(ctx0 + ctx_len, CONV_PAD), :] = zpad
        scr[i, pl.ds(lat0, n_lat), :] = lat
        scr[i, pl.ds(lat0 + n_lat, CONV_PAD), :] = zpad


def _dwconv_block(scr, w9, seg, ctx_len):
    n_lat = seg - ctx_len
    ctx0, lat0, _ = _conv_rows(seg, ctx_len)
    acc_c = None
    for dc in (-1, 0, 1):
        k = 3 + (dc + 1)
        term = scr[0, pl.ds(ctx0 + dc, ctx_len), :] * w9[k:k + 1]
        acc_c = term if acc_c is None else acc_c + term
    cols = [scr[{-1: 1, 0: 0, 1: 2}[dc], pl.ds(lat0 + dc, n_lat), :] for dc in (-1, 0, 1)]
    rows = []
    for dr in (-1, 0, 1):
        k = (dr + 1) * 3
        rows.append(cols[0] * w9[k:k + 1] + cols[1] * w9[k + 1:k + 2] + cols[2] * w9[k + 2:k + 3])
    zeros = jnp.zeros((GRID_W, cols[0].shape[1]), F32)
    acc_l = (rows[1] + jnp.concatenate([zeros, rows[0][:n_lat - GRID_W]], axis=0)
             + jnp.concatenate([rows[2][GRID_W:], zeros], axis=0))
    return acc_c, acc_l


def _conv_kernel(x_ref, w_ref, b_ref, o_ref, scr, *, seg, ctx_len, n_norm_blocks):
    _fill_padded(scr, x_ref[...].astype(F32), seg, ctx_len)
    do_norm = pl.program_id(1) < n_norm_blocks
    for part, rows in zip(_dwconv_block(scr, w_ref[...], seg, ctx_len),
                          (pl.ds(0, ctx_len), pl.ds(ctx_len, seg - ctx_len))):
        y = _silu(part + b_ref[...])
        if n_norm_blocks:
            yn = y * lax.rsqrt(jnp.sum(y * y, axis=-1, keepdims=True) + L2_EPS)
            y = jnp.where(do_norm, yn, y)
        o_ref[rows, :] = y.astype(o_ref.dtype)


def _conv_silu(p, conv_w, conv_b, *, n_batch, seg, ctx_len, col0, n_norm_blocks, name):
    ntok = p.shape[0]
    cw = conv_w.shape[-1]
    kern = functools.partial(_conv_kernel, seg=seg, ctx_len=ctx_len, n_norm_blocks=n_norm_blocks)
    return pl.pallas_call(
        kern,
        grid=(n_batch, cw // LANES),
        in_specs=[pl.BlockSpec((seg, LANES), lambda b, j: (b, col0 // LANES + j)),
                  pl.BlockSpec((9, LANES), lambda b, j: (0, j)),
                  pl.BlockSpec((1, LANES), lambda b, j: (0, j))],
        out_specs=pl.BlockSpec((seg, LANES), lambda b, j: (b, j)),
        out_shape=jax.ShapeDtypeStruct((ntok, cw), BF16),
        scratch_shapes=[pltpu.VMEM((3, _conv_rows(seg, ctx_len)[2], LANES), F32)],
        compiler_params=_cparams(("arbitrary", "arbitrary")),
        name=name,
    )(p, conv_w.reshape(9, cw), conv_b.reshape(1, cw))


def _dn_conv(p, conv_w, *, n_batch, seg, ctx_len, col0):
    return _conv_silu(p, conv_w, jnp.zeros((conv_w.shape[-1],), F32), n_batch=n_batch, seg=seg,
                      ctx_len=ctx_len, col0=col0, n_norm_blocks=2 * DN_QK // LANES, name="dn_conv")


SO_BETA, SO_G, SO_DT, SO_A = 0, 16, 32, 64


def _small_kernel(x_ref, par_ref, o_ref):
    x = x_ref[...]
    par = par_ref[...]
    neg_a = -jnp.exp(par[1:2])
    tm = x.shape[0]
    beta = _sigmoid(x[:, SM_BETA:SM_BETA + 16])
    sp = _softplus(x[:, SM_A:SM_A + 48] + par[0:1, SM_A:SM_A + 48])
    g = sp[:, 0:16] * neg_a[:, SM_A:SM_A + 16]
    dt = sp[:, 16:48]
    a = dt * neg_a[:, SM_DT:SM_DT + 32]
    o_ref[...] = jnp.concatenate([beta, g, dt, a, jnp.zeros((tm, LANES - 96), F32)], axis=1)


def _small_prep(ps, dn_a_log, dn_dt_bias, ssm_a_log, ssm_dt_bias):
    ntok = ps.shape[0]
    tm = _tile(ntok, 1024)
    zeros16 = jnp.zeros((16,), F32)
    bias = jnp.concatenate([zeros16, dn_dt_bias.reshape(-1), ssm_dt_bias.reshape(-1),
                            jnp.zeros((LANES - 64,), F32)])
    a_log = jnp.concatenate([zeros16, dn_a_log.reshape(-1), ssm_a_log.reshape(-1),
                             jnp.zeros((LANES - 64,), F32)])
    par = jnp.concatenate([bias[None], a_log[None], jnp.zeros((6, LANES), F32)], axis=0)
    return pl.pallas_call(
        _small_kernel,
        grid=(ntok // tm,),
        in_specs=[pl.BlockSpec((tm, LANES), lambda i: (i, 0)),
                  pl.BlockSpec((8, LANES), lambda i: (0, 0))],
        out_specs=pl.BlockSpec((tm, LANES), lambda i: (i, 0)),
        out_shape=jax.ShapeDtypeStruct((ntok, LANES), F32),
        compiler_params=_cparams(("arbitrary",)),
        name="small_prep",
    )(ps, par)


def _chunk_of_step(d, s, n_ctx, n_all):
    bwd = jnp.where(s < n_ctx, n_ctx - 1 - s, n_all + n_ctx - 1 - s)
    return jnp.where(d == 0, s, bwd)


def _order_masks(d, size):
    ri = lax.broadcasted_iota(jnp.int32, (size, size), 0)
    ci = lax.broadcasted_iota(jnp.int32, (size, size), 1)
    delta = (ri - ci) * (1 - 2 * d)
    return delta >= 0, delta > 0, ri == ci


def _col_to_row(col, eye):
    return jnp.sum(jnp.where(eye, col, 0.0), axis=0, keepdims=True)


def _dn_scan_kernel(qf_ref, kf_ref, vf_ref, smf_ref, qb_ref, kb_ref, vb_ref, smb_ref,
                    of_ref, ob_ref, s_scr):
    c = DN_CHUNK
    h_n = DN_HEADS

    @pl.when(pl.program_id(1) == 0)
    def _():
        s_scr[...] = jnp.zeros(s_scr.shape, F32)

    scale = DN_DK ** -0.5
    refs = ((qf_ref, kf_ref, vf_ref, smf_ref, of_ref), (qb_ref, kb_ref, vb_ref, smb_ref, ob_ref))
    masks = [_order_masks(d, c) for d in range(2)]
    gcs = []
    gtots = []
    betas = []
    for d in range(2):
        sm = refs[d][3][...]
        betas.append(sm[:, SO_BETA + d * h_n:SO_BETA + (d + 1) * h_n])
        g = sm[:, SO_G + d * h_n:SO_G + (d + 1) * h_n]
        gcs.append(_dot_exact_lhs(masks[d][0].astype(BF16), g))
        gtots.append(jnp.sum(g, axis=0, keepdims=True))

    items = [(d, h) for d in range(2) for h in range(h_n)]
    n = range(len(items))
    lanes = [slice(h * DN_DK, (h + 1) * DN_DK) for _, h in items]
    incl = [masks[d][0] for d, _ in items]
    strict = [masks[d][1] for d, _ in items]
    eye = masks[0][2]
    qs = [refs[d][0][:, lanes[i]].astype(F32) * scale for i, (d, _) in enumerate(items)]
    ks = [refs[d][1][:, lanes[i]].astype(F32) for i, (d, _) in enumerate(items)]
    vs = [refs[d][2][:, lanes[i]].astype(F32) for i, (d, _) in enumerate(items)]
    bcols = [betas[d][:, h:h + 1] for d, h in items]
    gcols = [gcs[d][:, h:h + 1] for d, h in items]
    gts = [gtots[d][:, h:h + 1] for d, h in items]
    decs = [jnp.exp(jnp.where(incl[i], gcols[i] - _col_to_row(gcols[i], eye), 0.0)) for i in n]
    kbs = [ks[i] * bcols[i] for i in n]
    a_s = [_dot_nt(jnp.concatenate([kbs[i], qs[i]], axis=0), ks[i]) for i in n]
    atts = [jnp.where(incl[i], a_s[i][c:] * decs[i], 0.0) for i in n]
    tinvs = _inv_unit_lower([-jnp.where(strict[i], a_s[i][:c] * decs[i], 0.0) for i in n], c)
    egs = [jnp.exp(gcols[i]) for i in n]
    uws = [_dot(tinvs[i], jnp.concatenate([vs[i] * bcols[i], kbs[i] * egs[i]], axis=1))
           for i in n]
    s_olds = [s_scr[d, h] for d, h in items]
    wss = [_dot(jnp.concatenate([uws[i][:, DN_DK:], qs[i] * egs[i]], axis=0), s_olds[i])
           for i in n]
    v_news = [uws[i][:, :DN_DK] - wss[i][:c] for i in n]
    outs = [wss[i][c:] + _dot(atts[i], v_news[i]) for i in n]
    upds = [_dot_tn(ks[i] * jnp.exp(gts[i] - gcols[i]), v_news[i]) for i in n]
    for i, (d, h) in enumerate(items):
        s_scr[d, h] = s_olds[i] * jnp.exp(gts[i]) + upds[i]
        refs[d][4][:, lanes[i]] = outs[i].astype(refs[d][4].dtype)


def _scan_specs(c, widths, n_all, n_ctx):
    def spec(d, width, col):
        return pl.BlockSpec(
            (c, width), lambda b, s: (b * n_all + _chunk_of_step(d, s, n_ctx, n_all), col))
    return [spec(d, w, col) for d in range(2) for w, col in widths]


def _dn_scan(qkv, small, *, n_batch, seg, ctx_len):
    ntok = qkv.shape[0]
    c = DN_CHUNK
    n_all = seg // c
    n_ctx = ctx_len // c
    ins = _scan_specs(c, [(DN_QK, 0), (DN_QK, 1), (DN_V, 2), (LANES, 0)], n_all, n_ctx)
    outs = _scan_specs(c, [(DN_V, 0)], n_all, n_ctx)
    return pl.pallas_call(
        _dn_scan_kernel,
        grid=(n_batch, n_all),
        in_specs=ins,
        out_specs=outs,
        out_shape=[jax.ShapeDtypeStruct((ntok, DN_V), SCAN_OUT_DTYPE)] * 2,
        scratch_shapes=[pltpu.VMEM((2, DN_HEADS, DN_DK, DN_DK), F32)],
        compiler_params=_cparams(("arbitrary", "arbitrary")),
        name="dn_scan",
    )(qkv, qkv, qkv, small, qkv, qkv, qkv, small)


def _dn_readout_kernel(of_ref, ob_ref, gate_ref, g_ref, y_ref):
    o = of_ref[...].astype(F32) + ob_ref[...].astype(F32)
    tm = o.shape[0]
    gate = gate_ref[...].astype(F32)
    outs = []
    for h in range(DN_HEADS):
        lanes = slice(h * DN_DK, (h + 1) * DN_DK)
        oh = o[:, lanes]
        yh = oh * lax.rsqrt(jnp.mean(oh * oh, axis=-1, keepdims=True) + NORM_EPS) * g_ref[...]
        outs.append(yh * _silu(gate[:, lanes]))
    y_ref[...] = jnp.concatenate(outs, axis=1).astype(y_ref.dtype)


def _dn_readout(o_f, o_b, p, norm_g, *, col_gate):
    ntok = p.shape[0]
    tm = _tile(ntok, 512)
    tok = pl.BlockSpec((tm, DN_V), lambda i: (i, 0))
    return pl.pallas_call(
        _dn_readout_kernel,
        grid=(ntok // tm,),
        in_specs=[tok, tok,
                  pl.BlockSpec((tm, DN_V), lambda i: (i, col_gate // DN_V)),
                  pl.BlockSpec((1, DN_DK), lambda i: (0, 0))],
        out_specs=tok,
        out_shape=jax.ShapeDtypeStruct((ntok, DN_V), BF16),
        compiler_params=_cparams(("arbitrary",)),
        name="dn_readout",
    )(o_f, o_b, p, norm_g.reshape(1, DN_DK))


def _head_expander(n_heads, head_dim):
    hi = lax.broadcasted_iota(jnp.int32, (n_heads, n_heads * head_dim), 0)
    li = lax.broadcasted_iota(jnp.int32, (n_heads, n_heads * head_dim), 1)
    lo = hi * head_dim
    return jnp.where((li >= lo) & (li < lo + head_dim), 1.0, 0.0).astype(BF16)


def _ssd_scan_kernel(xf_ref, bf_ref, cf_ref, smf_ref, xb_ref, bb_ref, cb_ref, smb_ref,
                     yf_ref, yb_ref, s_scr):
    c = SSM_CHUNK
    hn = SSM_HEADS
    hd = SSM_HEADDIM
    per_group = hn // SSM_GROUPS
    gw = per_group * hd

    @pl.when(pl.program_id(1) == 0)
    def _():
        s_scr[...] = jnp.zeros(s_scr.shape, F32)

    refs = ((xf_ref, bf_ref, cf_ref, smf_ref, yf_ref), (xb_ref, bb_ref, cb_ref, smb_ref, yb_ref))
    expand = _head_expander(hn, hd)
    lane_lo = lax.broadcasted_iota(jnp.int32, (c, 2 * hd), 1) < hd
    pre = []
    for d in range(2):
        incl, _, eye = _order_masks(d, c)
        sm = refs[d][3][...]
        dt = sm[:, SO_DT + d * hn:SO_DT + (d + 1) * hn]
        a = sm[:, SO_A + d * hn:SO_A + (d + 1) * hn]
        acum = _dot_exact_lhs(incl.astype(BF16), a)
        acum_x = _dot_exact_rhs(acum, expand)
        atot_x = jnp.sum(_dot_exact_rhs(a, expand), axis=0, keepdims=True)
        xdt = refs[d][0][...].astype(F32) * _dot_exact_rhs(dt, expand)
        pre.append(dict(incl=incl, eye=eye, acum=acum, xdt=xdt,
                        xdec=xdt * jnp.exp(atot_x - acum_x),
                        eacum=jnp.exp(acum_x), etot=jnp.exp(atot_x)))

    dgs = [(d, g) for d in range(2) for g in range(SSM_GROUPS)]
    bgs = [refs[d][1][:, g * SSM_STATE:(g + 1) * SSM_STATE] for d, g in dgs]
    cgs = [refs[d][2][:, g * SSM_STATE:(g + 1) * SSM_STATE] for d, g in dgs]
    cbs = [_dot_nt(cgs[i], bgs[i]) for i in range(len(dgs))]
    ms = {}
    for d in range(2):
        p = pre[d]
        for h in range(hn):
            acol = p["acum"][:, h:h + 1]
            lmat = jnp.exp(jnp.where(p["incl"], acol - _col_to_row(acol, p["eye"]), 0.0))
            ms[d, h] = jnp.where(p["incl"], cbs[d * SSM_GROUPS + h // per_group] * lmat, 0.0)
    ydiag = {}
    for d in range(2):
        for pair in range(hn // 2):
            xp = pre[d]["xdt"][:, pair * 2 * hd:(pair + 1) * 2 * hd]
            ydiag[d, pair] = jnp.where(lane_lo, _dot(ms[d, 2 * pair], xp),
                                       _dot(ms[d, 2 * pair + 1], xp))
    s_olds = [s_scr[d, g] for d, g in dgs]
    yoffs = [_dot(cgs[i], s_olds[i]) for i in range(len(dgs))]
    sts = [_dot_tn(bgs[i], pre[d]["xdec"][:, g * gw:(g + 1) * gw])
           for i, (d, g) in enumerate(dgs)]
    for i, (d, g) in enumerate(dgs):
        lanes = slice(g * gw, (g + 1) * gw)
        s_scr[d, g] = s_olds[i] * pre[d]["etot"][:, lanes] + sts[i]
        yd = jnp.concatenate(
            [ydiag[d, pr] for pr in range(g * per_group // 2, (g + 1) * per_group // 2)], axis=1)
        refs[d][4][:, lanes] = (yd + yoffs[i] * pre[d]["eacum"][:, lanes]).astype(refs[d][4].dtype)


def _ssd_scan(xbc, small, *, n_batch, seg, ctx_len):
    ntok = xbc.shape[0]
    c = SSM_CHUNK
    n_all = seg // c
    n_ctx = ctx_len // c
    gn = SSM_GROUPS * SSM_STATE
    ins = _scan_specs(c, [(SSM_INNER, 0), (gn, SSM_INNER // gn), (gn, SSM_INNER // gn + 1),
                          (LANES, 0)], n_all, n_ctx)
    outs = _scan_specs(c, [(SSM_INNER, 0)], n_all, n_ctx)
    return pl.pallas_call(
        _ssd_scan_kernel,
        grid=(n_batch, n_all),
        in_specs=ins,
        out_specs=outs,
        out_shape=[jax.ShapeDtypeStruct((ntok, SSM_INNER), SCAN_OUT_DTYPE)] * 2,
        scratch_shapes=[pltpu.VMEM((2, SSM_GROUPS, SSM_STATE, SSM_INNER // SSM_GROUPS), F32)],
        compiler_params=_cparams(("arbitrary", "arbitrary")),
        name="ssd_scan",
    )(xbc, xbc, xbc, small, xbc, xbc, xbc, small)


def _ssd_readout_kernel(yf_ref, yb_ref, xs_ref, z_ref, d_ref, g_ref, o_ref):
    y = ((yf_ref[...].astype(F32) + yb_ref[...].astype(F32)
          + d_ref[...] * xs_ref[...].astype(F32)) * _silu(z_ref[...].astype(F32)))
    gw = SSM_INNER // SSM_GROUPS
    outs = []
    for g in range(SSM_GROUPS):
        lanes = slice(g * gw, (g + 1) * gw)
        yg = y[:, lanes]
        outs.append(yg * lax.rsqrt(jnp.mean(yg * yg, axis=-1, keepdims=True) + SSM_NORM_EPS)
                    * g_ref[:, lanes])
    o_ref[...] = jnp.concatenate(outs, axis=1).astype(o_ref.dtype)


def _ssd_readout(y_f, y_b, xbc, p, d_skip, norm_g, *, col_z):
    ntok = p.shape[0]
    tm = _tile(ntok, 512)
    w = SSM_INNER
    return pl.pallas_call(
        _ssd_readout_kernel,
        grid=(ntok // tm,),
        in_specs=[pl.BlockSpec((tm, w), lambda i: (i, 0)),
                  pl.BlockSpec((tm, w), lambda i: (i, 0)),
                  pl.BlockSpec((tm, w), lambda i: (i, 0)),
                  pl.BlockSpec((tm, w), lambda i: (i, col_z // w)),
                  pl.BlockSpec((1, w), lambda i: (0, 0)),
                  pl.BlockSpec((1, w), lambda i: (0, 0))],
        out_specs=pl.BlockSpec((tm, w), lambda i: (i, 0)),
        out_shape=jax.ShapeDtypeStruct((ntok, w), BF16),
        compiler_params=_cparams(("arbitrary",)),
        name="ssd_readout",
    )(y_f, y_b, xbc, p, jnp.repeat(d_skip, SSM_HEADDIM).reshape(1, w), norm_g.reshape(1, w))


def _rw_shift_kernel(x_ref, mu_ref, o_ref, scr, *, seg, ctx_len):
    x = x_ref[...].astype(F32)
    n_lat = seg - ctx_len
    _fill_padded(scr, x, seg, ctx_len)
    ctx0, lat0, _ = _conv_rows(seg, ctx_len)
    mu = mu_ref[...]
    lane_c = lax.broadcasted_iota(jnp.int32, (ctx_len, LANES), 1)
    prev = scr[0, pl.ds(ctx0 - 1, ctx_len), :]
    nxt = scr[0, pl.ds(ctx0 + 1, ctx_len), :]
    xc = x[:ctx_len]
    o_ref[pl.ds(0, ctx_len), :] = (xc + (jnp.where(lane_c % 2 == 0, prev, nxt) - xc) * mu
                                   ).astype(o_ref.dtype)
    sel = lax.broadcasted_iota(jnp.int32, (n_lat, LANES), 1) % 4
    left = scr[1, pl.ds(lat0 - 1, n_lat), :]
    right = scr[2, pl.ds(lat0 + 1, n_lat), :]
    up = scr[0, pl.ds(lat0 - GRID_W, n_lat), :]
    down = scr[0, pl.ds(lat0 + GRID_W, n_lat), :]
    shifted = jnp.where(sel == 0, left, jnp.where(sel == 1, right, jnp.where(sel == 2, up, down)))
    xl = x[ctx_len:]
    o_ref[pl.ds(ctx_len, n_lat), :] = (xl + (shifted - xl) * mu).astype(o_ref.dtype)


def _rw_shift(p, mu_packed, *, n_batch, seg, ctx_len, col0):
    ntok = p.shape[0]
    cw = mu_packed.shape[-1]
    kern = functools.partial(_rw_shift_kernel, seg=seg, ctx_len=ctx_len)
    return pl.pallas_call(
        kern,
        grid=(n_batch, cw // LANES),
        in_specs=[pl.BlockSpec((seg, LANES), lambda b, j: (b, col0 // LANES + j)),
                  pl.BlockSpec((1, LANES), lambda b, j: (0, j))],
        out_specs=pl.BlockSpec((seg, LANES), lambda b, j: (b, j)),
        out_shape=jax.ShapeDtypeStruct((ntok, cw), BF16),
        scratch_shapes=[pltpu.VMEM((3, _conv_rows(seg, ctx_len)[2], LANES), F32)],
        compiler_params=_cparams(("arbitrary", "arbitrary")),
        name="rw_shift",
    )(p, mu_packed.reshape(1, cw))


def _head_sum(x, n_heads, head_dim):
    expand = _head_expander(n_heads, head_dim)
    hi = lax.broadcasted_iota(jnp.int32, (n_heads * head_dim, n_heads), 1)
    li = lax.broadcasted_iota(jnp.int32, (n_heads * head_dim, n_heads), 0)
    lo = hi * head_dim
    reduce = jnp.where((li >= lo) & (li < lo + head_dim), 1.0, 0.0).astype(BF16)
    return _dot_hilo_rhs(_dot_hilo_rhs(x, reduce), expand)


def _rw_prep_kernel(x_ref, w0_ref, w2_ref, a0_ref, a2_ref, g2_ref, kk_ref, ka_ref, rk_ref,
                    r_ref, v_ref, nk_ref, gate_ref, bonus_ref, lw_ref, kd_ref, b_ref):
    w = RW_WIDTH
    r = x_ref[:, 0:w].astype(F32)
    k = x_ref[:, w:2 * w].astype(F32)
    v = x_ref[:, 2 * w:3 * w].astype(F32)
    lora = 3 * w
    wl = x_ref[:, lora + RWL_W:lora + RWL_W + 256].astype(F32)
    al = x_ref[:, lora + RWL_A:lora + RWL_A + 256]
    gl = x_ref[:, lora + RWL_G:lora + RWL_G + RW_GATE_LORA].astype(F32)
    w_raw = w0_ref[...] + _dot(jnp.tanh(wl), w2_ref[...])
    log_decay = -jnp.exp(-_softplus(-w_raw) - 0.5)
    icl = _sigmoid(a0_ref[...] + _dot(al, a2_ref[...]))
    gate_ref[...] = _dot(_sigmoid(gl), g2_ref[...]).astype(gate_ref.dtype)
    kx = k * kk_ref[...]
    nk = kx * lax.rsqrt(_head_sum(kx * kx, RW_HEADS, RW_HEAD) + L2_EPS)
    r_ref[...] = r.astype(r_ref.dtype)
    v_ref[...] = v.astype(v_ref.dtype)
    nk_ref[...] = nk.astype(nk_ref.dtype)
    ksum = None
    for d in range(2):
        icl_d = icl[:, d * w:(d + 1) * w]
        k_d = k * (1.0 + (icl_d - 1.0) * ka_ref[...])
        lw_ref[d] = log_decay[:, d * w:(d + 1) * w]
        kd_ref[d] = k_d.astype(kd_ref.dtype)
        b_ref[d] = (nk * icl_d).astype(b_ref.dtype)
        ksum = k_d if ksum is None else ksum + k_d
    bonus_ref[...] = (_head_sum(r * ksum * rk_ref[...], RW_HEADS, RW_HEAD) * v
                      ).astype(bonus_ref.dtype)


def _pack_rw_vec(vec):
    w3 = 3 * RW_WIDTH
    z = lambda n: jnp.zeros((n,), vec.dtype)
    l0 = w3 + 2 * RW_DECAY_LORA
    l1 = l0 + 2 * RW_A_LORA
    return jnp.concatenate([vec[:w3], vec[w3:l0], z(64), vec[l0:l1], z(64), vec[l1:], z(256)])


def _pack_lora(m2):
    lora, w = m2.shape[1], m2.shape[2]
    out = jnp.zeros((256, 2 * w), F32)
    out = out.at[0:lora, 0:w].set(m2[0])
    out = out.at[lora:2 * lora, w:2 * w].set(m2[1])
    return out.astype(BF16)


def _rw_prep(xs, w0, w2, a0, a2, g2, k_k, k_a, r_k):
    ntok = xs.shape[0]
    w = RW_WIDTH
    tm = _tile(ntok, 256)
    row = lambda a: a.reshape(1, -1)
    full = lambda shape: pl.BlockSpec(shape, lambda i: (0,) * len(shape))
    tok = pl.BlockSpec((tm, w), lambda i: (i, 0))
    tok2 = pl.BlockSpec((2, tm, w), lambda i: (0, i, 0))
    sd = jax.ShapeDtypeStruct
    return pl.pallas_call(
        _rw_prep_kernel,
        grid=(ntok // tm,),
        in_specs=[pl.BlockSpec((tm, 4 * w), lambda i: (i, 0)),
                  full((1, 2 * w)), full((256, 2 * w)), full((1, 2 * w)), full((256, 2 * w)),
                  full((RW_GATE_LORA, w)), full((1, w)), full((1, w)), full((1, w))],
        out_specs=[tok, tok, tok, tok, tok, tok2, tok2, tok2],
        out_shape=[sd((ntok, w), BF16)] * 5
        + [sd((2, ntok, w), F32), sd((2, ntok, w), BF16), sd((2, ntok, w), BF16)],
        compiler_params=_cparams(("arbitrary",)),
        name="rw_prep",
    )(xs, row(w0), _pack_lora(w2), row(a0), _pack_lora(a2), g2.astype(BF16), row(k_k), row(k_a),
      row(r_k))


def _stack_heads(x, lane_lo):
    return jnp.concatenate([jnp.where(lane_lo, x, 0.0), jnp.where(lane_lo, 0.0, x)], axis=0)


def _rw_scan_kernel(rf_ref, vf_ref, nkf_ref, lwf_ref, kdf_ref, bf_ref,
                    rb_ref, vb_ref, nkb_ref, lwb_ref, kdb_ref, bb_ref, yf_ref, yb_ref, h_scr):
    c = RW_CHUNK
    pw = 2 * RW_HEAD
    n_pairs = RW_HEADS // 2

    @pl.when(pl.program_id(1) == 0)
    def _():
        h_scr[...] = jnp.zeros(h_scr.shape, F32)

    refs = ((rf_ref, vf_ref, nkf_ref, lwf_ref, kdf_ref, bf_ref, yf_ref),
            (rb_ref, vb_ref, nkb_ref, lwb_ref, kdb_ref, bb_ref, yb_ref))
    rj = lax.broadcasted_iota(jnp.int32, (pw, pw), 0)
    cj = lax.broadcasted_iota(jnp.int32, (pw, pw), 1)
    eye_p = rj == cj
    lane_lo = lax.broadcasted_iota(jnp.int32, (c, pw), 1) < RW_HEAD
    ri = lax.broadcasted_iota(jnp.int32, (2 * c, 2 * c), 0) % c
    ci = lax.broadcasted_iota(jnp.int32, (2 * c, 2 * c), 1) % c

    per_dir = []
    for d in range(2):
        r_ref, v_ref, nk_ref, lw_ref, kd_ref, b_ref, _ = refs[d]
        incl, _, _ = _order_masks(d, c)
        lw = lw_ref[0]
        cw = _dot_exact_lhs(incl.astype(BF16), lw)
        tot = jnp.sum(lw, axis=0, keepdims=True)
        e_neg = jnp.exp(-cw)
        e_rem = jnp.exp(tot - cw)
        b = b_ref[0].astype(F32)
        kd = kd_ref[0].astype(F32)
        delta = (ri - ci) * (1 - 2 * d)
        incl2 = delta >= 0
        per_dir.append(dict(
            a_t=-nk_ref[...].astype(F32) * jnp.exp(cw - lw), r_t=r_ref[...].astype(F32) * jnp.exp(cw),
            b_t=b * e_neg, k_t=kd * e_neg, b_h=b * e_rem, k_h=kd * e_rem,
            v=v_ref[...].astype(F32), e_tot=jnp.exp(tot), strict2=delta > 0,
            mask_y=jnp.concatenate([incl2, incl2], axis=1)))

    items = [(d, p) for d in range(2) for p in range(n_pairs)]
    n = range(len(items))
    ln = [slice(p * pw, (p + 1) * pw) for _, p in items]
    dd = [per_dir[d] for d, _ in items]

    def stacked(i, top, bottom):
        return jnp.concatenate([_stack_heads(dd[i][top][:, ln[i]], lane_lo),
                                _stack_heads(dd[i][bottom][:, ln[i]], lane_lo)], axis=0)

    xs = [stacked(i, "a_t", "r_t") for i in n]
    ys = [stacked(i, "b_t", "k_t") for i in n]
    vst = [_stack_heads(dd[i]["v"][:, ln[i]], lane_lo) for i in n]
    gs = [_dot_nt(xs[i], ys[i]) for i in n]
    tinvs = _inv_unit_lower(
        [jnp.where(dd[i]["strict2"], gs[i][:2 * c, :2 * c], 0.0) for i in n], c)
    h_olds = [h_scr[d, p] for d, p in items]
    xh = [_dot(xs[i], h_olds[i]) for i in n]
    rhs = [xh[i][:2 * c] + _dot(jnp.where(dd[i]["strict2"], gs[i][:2 * c, 2 * c:], 0.0), vst[i])
           for i in n]
    us = [_dot(tinvs[i], rhs[i]) for i in n]
    uv = [jnp.concatenate([us[i], vst[i]], axis=0) for i in n]
    yst = [xh[i][2 * c:] + _dot(jnp.where(dd[i]["mask_y"], gs[i][2 * c:, :], 0.0), uv[i])
           for i in n]
    upd = [_dot_tn(stacked(i, "b_h", "k_h"), uv[i]) for i in n]
    for i, (d, p) in enumerate(items):
        e_col = jnp.sum(jnp.where(eye_p, dd[i]["e_tot"][:, ln[i]], 0.0), axis=1, keepdims=True)
        h_scr[d, p] = h_olds[i] * e_col + upd[i]
        refs[d][6][:, ln[i]] = (yst[i][:c] + yst[i][c:]).astype(refs[d][6].dtype)


def _rw_scan(r, v, nk, lw, kd, b, *, n_batch, seg, ctx_len):
    ntok = r.shape[0]
    c = RW_CHUNK
    w = RW_WIDTH
    n_all = seg // c
    n_ctx = ctx_len // c
    def row(d):
        return lambda bb, s: bb * n_all + _chunk_of_step(d, s, n_ctx, n_all)

    def tok(d):
        return pl.BlockSpec((c, w), lambda bb, s: (row(d)(bb, s), 0))

    def tok2(d):
        return pl.BlockSpec((1, c, w), lambda bb, s: (d, row(d)(bb, s), 0))

    return pl.pallas_call(
        _rw_scan_kernel,
        grid=(n_batch, n_all),
        in_specs=[spec(d) for d in range(2) for spec in (tok, tok, tok, tok2, tok2, tok2)],
        out_specs=[tok(0), tok(1)],
        out_shape=[jax.ShapeDtypeStruct((ntok, w), SCAN_OUT_DTYPE)] * 2,
        scratch_shapes=[pltpu.VMEM((2, RW_HEADS // 2, 2 * RW_HEAD, 2 * RW_HEAD), F32)],
        compiler_params=_cparams(("arbitrary", "arbitrary")),
        name="rw_scan",
    )(r, v, nk, lw, kd, b, r, v, nk, lw, kd, b)


def _rw_readout_kernel(yf_ref, yb_ref, bonus_ref, gate_ref, g_ref, b_ref, o_ref):
    inv_n = 1.0 / RW_HEAD
    tm = o_ref.shape[0]
    halves = [slice(0, tm // 2), slice(tm // 2, tm)]
    ys = [yf_ref[h, :].astype(F32) + yb_ref[h, :].astype(F32) for h in halves]
    means = [_head_sum(y, RW_HEADS, RW_HEAD) * inv_n for y in ys]
    ycs = [y - m for y, m in zip(ys, means)]
    vs = [_head_sum(yc * yc, RW_HEADS, RW_HEAD) * inv_n for yc in ycs]
    for h, yc, var in zip(halves, ycs, vs):
        yn = yc * lax.rsqrt(var + RW_GN_EPS) * g_ref[...] + b_ref[...]
        o_ref[h, :] = ((yn + bonus_ref[h, :].astype(F32)) * gate_ref[h, :].astype(F32)
                       ).astype(o_ref.dtype)


def _rw_readout(y_f, y_b, bonus, gate, ln_g, ln_b):
    ntok = bonus.shape[0]
    w = RW_WIDTH
    tm = _tile(ntok, 512)
    tok = pl.BlockSpec((tm, w), lambda i: (i, 0))
    vec = pl.BlockSpec((1, w), lambda i: (0, 0))
    return pl.pallas_call(
        _rw_readout_kernel,
        grid=(ntok // tm,),
        in_specs=[tok, tok, tok, tok, vec, vec],
        out_specs=tok,
        out_shape=jax.ShapeDtypeStruct((ntok, w), BF16),
        compiler_params=_cparams(("arbitrary",)),
        name="rw_readout",
    )(y_f, y_b, bonus, gate, ln_g.reshape(1, w), ln_b.reshape(1, w))


def _branch_mix_kernel(oa_ref, ob_ref, oc_ref, ga_ref, gb_ref, gc_ref, w_ref, y_ref):
    acc = None
    for i, (o_ref, g_ref) in enumerate(((oa_ref, ga_ref), (ob_ref, gb_ref), (oc_ref, gc_ref))):
        term = (_sigmoid(g_ref[...].astype(F32))
                * jnp.dot(o_ref[...], w_ref[i], preferred_element_type=F32))
        acc = term if acc is None else acc + term
    y_ref[...] = acc.astype(y_ref.dtype)


def _branch_mix(oa, ob, oc, p, w_branch_bf16, *, d_model):
    ntok = p.shape[0]
    bw = oa.shape[1]
    tm = _tile(ntok, 512)
    tn = 512
    nb = d_model // tn
    o_spec = pl.BlockSpec((tm, bw), lambda i, n: (i, 0))
    g_spec = lambda br: pl.BlockSpec((tm, tn), lambda i, n: (i, br * nb + n))
    return pl.pallas_call(
        _branch_mix_kernel,
        grid=(ntok // tm, nb),
        in_specs=[o_spec, o_spec, o_spec, g_spec(0), g_spec(1), g_spec(2),
                  pl.BlockSpec((3, bw, tn), lambda i, n: (0, 0, n))],
        out_specs=pl.BlockSpec((tm, tn), lambda i, n: (i, n)),
        out_shape=jax.ShapeDtypeStruct((ntok, d_model), BF16),
        compiler_params=_cparams(("arbitrary", "arbitrary")),
        name="branch_mix",
    )(oa, ob, oc, p, p, p, w_branch_bf16)


ROUTE_TILE = 256


def _dot3_nt(a, b):
    ah, al = _split2(a)
    bh, bl = _split2(b)
    d = functools.partial(lax.dot_general, dimension_numbers=(((1,), (1,)), ((), ())),
                          preferred_element_type=F32)
    return d(ah, bh) + d(ah, bl) + d(al, bh)


def _outproj_router_kernel(y_ref, x_ref, ml_ref, mc_ref, wo_ref, g_ref, wr_ref, br_ref,
                           xo_ref, f_ref, idx_ref, wgt_ref, rank_ref, cnt_ref, run_scr,
                           *, tm, seg, ctx_len):
    i = pl.program_id(0)

    @pl.when(i == 0)
    def _():
        run_scr[...] = jnp.zeros(run_scr.shape, F32)

    is_ctx = _ctx_rows(i, tm, seg, ctx_len)
    ml = ml_ref[0]
    mc = mc_ref[0]
    gate = jnp.where(is_ctx, mc[2:3], ml[2:3])
    halves = [slice(0, tm // 2), slice(tm // 2, tm)]
    x_news = [x_ref[h, :] + gate[h] * jnp.dot(y_ref[h, :], wo_ref[...], preferred_element_type=F32)
              for h in halves]
    fs = [_modulated_norm(x_news[j], g_ref[...], ml, mc, is_ctx[h], 3, 4)
          for j, h in enumerate(halves)]
    dh = x_ref.shape[1] // 2
    for j, h in enumerate(halves):
        xo_ref[h, :] = x_news[j]
        f_ref[h, :] = _pack_bf16_pair(fs[j][:, :dh], fs[j][:, dh:])

    ne = N_EXPERTS
    logits = jnp.concatenate([_dot3_nt(wr_ref[...], f) for f in fs], axis=1) + br_ref[...]
    eidx = lax.broadcasted_iota(jnp.int32, (ne, tm), 0)
    vals = logits
    picks = []
    tops = []
    hots = []
    for _ in range(TOP_K):
        m = jnp.max(vals, axis=0, keepdims=True)
        pick = jnp.min(jnp.where(vals == m, eidx, ne), axis=0, keepdims=True)
        hot = eidx == pick
        vals = jnp.where(hot, -jnp.inf, vals)
        tops.append(m)
        picks.append(pick)
        hots.append(hot)
    es = [jnp.exp(t - tops[0]) for t in tops]
    denom = es[0] + es[1] + es[2] + es[3]
    hot_all = jnp.where(hots[0] | hots[1] | hots[2] | hots[3], 1.0, 0.0)
    si = lax.broadcasted_iota(jnp.int32, (tm, tm), 0)
    ti = lax.broadcasted_iota(jnp.int32, (tm, tm), 1)
    earlier = jnp.where(si < ti, 1.0, 0.0).astype(BF16)
    before = jnp.dot(hot_all.astype(BF16), earlier, preferred_element_type=F32)
    rank = run_scr[:, 0:1] + before
    run_new = run_scr[...] + jnp.sum(hot_all, axis=1, keepdims=True)
    run_scr[...] = run_new
    cnt_ref[...] = run_new
    idx_ref[...] = jnp.concatenate(picks, axis=0)
    wgt_ref[...] = jnp.concatenate([e / denom for e in es], axis=0)
    rank_ref[...] = jnp.concatenate(
        [jnp.sum(jnp.where(h, rank, 0.0), axis=0, keepdims=True) for h in hots],
        axis=0).astype(jnp.int32)


def _outproj_router(y, x, mod, w_out_bf16, g_ffn, w_router, b_router, *, n_batch, seg, ctx_len):
    ntok, d = x.shape
    tm = _tile(seg, ROUTE_TILE, LANES)
    per = seg // tm
    ne = N_EXPERTS
    kern = functools.partial(_outproj_router_kernel, tm=tm, seg=seg, ctx_len=ctx_len)
    tok = pl.BlockSpec((tm, d), lambda i: (i, 0))
    sel = pl.BlockSpec((TOP_K, tm), lambda i: (0, i))
    sd = jax.ShapeDtypeStruct
    return pl.pallas_call(
        kern,
        grid=(ntok // tm,),
        in_specs=[tok, tok,
                  pl.BlockSpec((1, 6, d), lambda i: (i // per, 0, 0)),
                  pl.BlockSpec((1, 6, d), lambda i: (n_batch, 0, 0)),
                  pl.BlockSpec((d, d), lambda i: (0, 0)),
                  pl.BlockSpec((1, d), lambda i: (0, 0)),
                  pl.BlockSpec((ne, d), lambda i: (0, 0)),
                  pl.BlockSpec((ne, 1), lambda i: (0, 0))],
        out_specs=[tok, pl.BlockSpec((tm, d // 2), lambda i: (i, 0)), sel, sel, sel,
                   pl.BlockSpec((ne, LANES), lambda i: (0, 0))],
        out_shape=[sd((ntok, d), F32), sd((ntok, d // 2), jnp.uint32), sd((TOP_K, ntok), jnp.int32),
                   sd((TOP_K, ntok), F32), sd((TOP_K, ntok), jnp.int32), sd((ne, LANES), F32)],
        scratch_shapes=[pltpu.VMEM((ne, LANES), F32)],
        compiler_params=_cparams(("arbitrary",)),
        name="outproj_router",
    )(y, x, mod, mod, w_out_bf16, g_ffn.reshape(1, d), w_router.T, b_router.reshape(ne, 1))


EXPERT_TILE = 256
DMA_UNROLL = 4


def _dispatch_kernel(dest_ref, f_ref, xs_in_ref, xs_ref, sem, *, tm):
    del xs_in_ref

    def row_copy(j, k):
        dst = dest_ref[0, 0, k * tm + j]
        return pltpu.make_async_copy(f_ref.at[pl.ds(j, 1), :], xs_ref.at[pl.ds(dst, 1), :], sem)

    def issue(j, carry):
        for k in range(TOP_K):
            row_copy(j, k).start()
        return carry

    lax.fori_loop(0, tm, issue, 0, unroll=DMA_UNROLL)

    def drain(j, carry):
        for k in range(TOP_K):
            row_copy(j, k).wait()
        return carry

    lax.fori_loop(0, tm, drain, 0, unroll=DMA_UNROLL)


def _dispatch(f, dest_tiles, n_rows):
    ntok, d = f.shape
    tm = dest_tiles.shape[-1] // TOP_K
    kern = functools.partial(_dispatch_kernel, tm=tm)
    return pl.pallas_call(
        kern,
        grid=(ntok // tm,),
        in_specs=[pl.BlockSpec((1, 1, TOP_K * tm), lambda i: (i, 0, 0), memory_space=pltpu.SMEM),
                  pl.BlockSpec((tm, d), lambda i: (i, 0)),
                  pl.BlockSpec(memory_space=pl.ANY)],
        out_specs=pl.BlockSpec(memory_space=pl.ANY),
        out_shape=jax.ShapeDtypeStruct((n_rows, d), f.dtype),
        scratch_shapes=[pltpu.SemaphoreType.DMA(())],
        input_output_aliases={2: 0},
        compiler_params=_cparams(("arbitrary",)),
        name="moe_dispatch",
    )(dest_tiles, f, jnp.zeros((n_rows, d), f.dtype))


def _experts_kernel(te_ref, xs_ref, w1_ref, b1_ref, w2_ref, b2_ref, ys_ref, w1_scr, w2_scr):
    i = pl.program_id(0)

    @pl.when((i == 0) | (te_ref[i] != te_ref[jnp.maximum(i - 1, 0)]))
    def _():
        w1_scr[...] = w1_ref[0, 0].astype(BF16)
        w2_scr[...] = w2_ref[0, 0].astype(BF16)

    dh = xs_ref.shape[1]
    x_lo, x_hi = _unpack_bf16_pair(xs_ref[...])
    hu = (jnp.dot(x_lo.astype(BF16), w1_scr[0:dh, :], preferred_element_type=F32)
          + jnp.dot(x_hi.astype(BF16), w1_scr[dh:2 * dh, :], preferred_element_type=F32)
          + b1_ref[0])
    gt = jnp.minimum(hu[:, :D_EXPERT], SWIGLU_LIMIT)
    up = jnp.clip(hu[:, D_EXPERT:], -SWIGLU_LIMIT, SWIGLU_LIMIT)
    act = gt * _sigmoid(SWIGLU_ALPHA * gt) * (up + 1.0)
    y = jnp.dot(act.astype(BF16), w2_scr[...], preferred_element_type=F32) + b2_ref[0]
    ys_ref[...] = _pack_bf16_pair(y[:, :dh], y[:, dh:])


def _experts(xs, tile_expert, w1_all, b1, w2_all, b2, *, layer):
    n_rows, dh = xs.shape
    _, ne, d, de2 = w1_all.shape
    tmx = EXPERT_TILE
    grid_spec = pltpu.PrefetchScalarGridSpec(
        num_scalar_prefetch=1,
        grid=(n_rows // tmx,),
        in_specs=[pl.BlockSpec((tmx, dh), lambda i, te: (i, 0)),
                  pl.BlockSpec((1, 1, d, de2), lambda i, te: (layer, te[i], 0, 0)),
                  pl.BlockSpec((1, 1, de2), lambda i, te: (te[i], 0, 0)),
                  pl.BlockSpec((1, 1, de2 // 2, d), lambda i, te: (layer, te[i], 0, 0)),
                  pl.BlockSpec((1, 1, d), lambda i, te: (te[i], 0, 0))],
        out_specs=pl.BlockSpec((tmx, dh), lambda i, te: (i, 0)),
        scratch_shapes=[pltpu.VMEM((d, de2), BF16), pltpu.VMEM((de2 // 2, d), BF16)],
    )
    return pl.pallas_call(
        _experts_kernel,
        grid_spec=grid_spec,
        out_shape=jax.ShapeDtypeStruct((n_rows, dh), jnp.uint32),
        compiler_params=_cparams(("arbitrary",)),
        name="moe_experts",
    )(tile_expert, xs, w1_all, b1.reshape(ne, 1, de2), w2_all, b2.reshape(ne, 1, d))


def _combine_kernel(dest_ref, x_ref, wgt_ref, ml_ref, mc_ref, ys_ref, xo_ref, rows_scr, sem,
                    *, tm, seg, ctx_len):
    def row_copy(j, k):
        src = dest_ref[0, 0, k * tm + j]
        return pltpu.make_async_copy(ys_ref.at[pl.ds(src, 1), :],
                                     rows_scr.at[k, pl.ds(j, 1), :], sem)

    def issue(j, carry):
        for k in range(TOP_K):
            row_copy(j, k).start()
        return carry

    lax.fori_loop(0, tm, issue, 0, unroll=DMA_UNROLL)

    def drain(j, carry):
        for k in range(TOP_K):
            row_copy(j, k).wait()
        return carry

    lax.fori_loop(0, tm, drain, 0, unroll=DMA_UNROLL)

    wgt = wgt_ref[...]
    y_lo = None
    y_hi = None
    for k in range(TOP_K):
        lo, hi = _unpack_bf16_pair(rows_scr[k])
        wk = wgt[:, k:k + 1]
        y_lo = wk * lo if y_lo is None else y_lo + wk * lo
        y_hi = wk * hi if y_hi is None else y_hi + wk * hi
    is_ctx = _ctx_rows(pl.program_id(0), tm, seg, ctx_len)
    gate = jnp.where(is_ctx, mc_ref[0][5:6], ml_ref[0][5:6])
    xo_ref[...] = x_ref[...] + gate * jnp.concatenate([y_lo, y_hi], axis=1)


def _combine(ys, dest_tiles, x, wgt_t, mod, *, n_batch, seg, ctx_len):
    ntok, d = x.shape
    tm = dest_tiles.shape[-1] // TOP_K
    per = seg // tm
    kern = functools.partial(_combine_kernel, tm=tm, seg=seg, ctx_len=ctx_len)
    tok = pl.BlockSpec((tm, d), lambda i: (i, 0))
    return pl.pallas_call(
        kern,
        grid=(ntok // tm,),
        in_specs=[pl.BlockSpec((1, 1, TOP_K * tm), lambda i: (i, 0, 0), memory_space=pltpu.SMEM),
                  tok,
                  pl.BlockSpec((tm, TOP_K), lambda i: (i, 0)),
                  pl.BlockSpec((1, 6, d), lambda i: (i // per, 0, 0)),
                  pl.BlockSpec((1, 6, d), lambda i: (n_batch, 0, 0)),
                  pl.BlockSpec(memory_space=pl.ANY)],
        out_specs=tok,
        out_shape=jax.ShapeDtypeStruct((ntok, d), F32),
        scratch_shapes=[pltpu.VMEM((TOP_K, tm, ys.shape[1]), ys.dtype), pltpu.SemaphoreType.DMA(())],
        compiler_params=_cparams(("arbitrary",)),
        name="moe_combine",
    )(dest_tiles, x, wgt_t, mod, mod, ys)


def _route_plan(idx, rank, counts, tm):
    tmx = EXPERT_TILE
    ntok = idx.shape[1]
    cnt = counts[:, 0].astype(jnp.int32)
    padded = ((cnt + tmx - 1) // tmx) * tmx
    ends = jnp.cumsum(padded)
    starts = ends - padded
    experts = jnp.arange(N_EXPERTS, dtype=jnp.int32)
    dest = rank + jnp.sum(jnp.where(idx[..., None] == experts, starts, 0), axis=-1)
    n_tiles = (TOP_K * ntok) // tmx + N_EXPERTS
    tile_start = jnp.arange(n_tiles, dtype=jnp.int32) * tmx
    tile_expert = jnp.minimum(jnp.sum((tile_start[:, None] >= ends[None, :]).astype(jnp.int32), axis=1),
                              N_EXPERTS - 1)
    nt = ntok // tm
    dest_tiles = dest.reshape(TOP_K, nt, tm).transpose(1, 0, 2).reshape(nt, 1, TOP_K * tm)
    return dest_tiles, tile_expert, n_tiles * tmx


def _final_norm_kernel(x_ref, g_ref, o_ref):
    x = x_ref[...]
    o_ref[0] = x * lax.rsqrt(jnp.mean(x * x, axis=-1, keepdims=True) + NORM_EPS) * g_ref[...]


def _final_norm(x, g, *, n_batch, seg, ctx_len):
    d = x.shape[1]
    n_lat = seg - ctx_len
    tm = _tile(math.gcd(ctx_len, n_lat), 512)
    per = n_lat // tm
    return pl.pallas_call(
        _final_norm_kernel,
        grid=(n_batch, per),
        in_specs=[pl.BlockSpec((tm, d), lambda b, j: (b * (seg // tm) + ctx_len // tm + j, 0)),
                  pl.BlockSpec((1, d), lambda b, j: (0, 0))],
        out_specs=pl.BlockSpec((1, tm, d), lambda b, j: (b, j, 0)),
        out_shape=jax.ShapeDtypeStruct((n_batch, n_lat, d), F32),
        compiler_params=_cparams(("arbitrary", "arbitrary")),
        name="final_norm",
    )(x, g.reshape(1, d))


def kernel(x, c, ctx, c_ctx, norm_mix_g, norm_ffn_g, w_mod, b_mod, w_in, dn_conv, dn_a_log, dn_dt_bias, dn_norm_g, rw_mu, rw_w0, rw_w2, rw_a0, rw_a2, rw_g2, rw_k_k, rw_k_a, rw_r_k, rw_ln_g, rw_ln_b, ssm_conv, ssm_conv_b, ssm_a_log, ssm_dt_bias, ssm_d, ssm_norm_g, w_branch, w_out, w_router, b_router, w_e1, b_e1, w_e2, b_e2, final_norm_g):
    n_batch, seq, d = x.shape
    ctx_len = ctx.shape[1]
    seg = seq + ctx_len
    n_layers = w_in.shape[0]
    assert seq % GRID_W == 0 and ctx_len % SSM_CHUNK == 0 and seq % SSM_CHUNK == 0
    off = _p_layout(d)
    dims = dict(n_batch=n_batch, seg=seg, ctx_len=ctx_len)

    mod_rows = -(-(n_batch + 1) // 8) * 8
    cond = jnp.concatenate([c, c_ctx[None], jnp.zeros((mod_rows - n_batch - 1, d), F32)], axis=0)
    mod_all = _modulation(cond, w_mod, b_mod).reshape(n_layers, mod_rows, 6, d)

    xa = jnp.concatenate([ctx, x], axis=1).reshape(n_batch * seg, d)
    route_tile = _tile(seg, ROUTE_TILE, LANES)
    for i in range(n_layers):
        mod = mod_all[i]
        p, p_small = _in_projection(xa, norm_mix_g[i], mod, _pack_w_in(w_in[i], d), **dims)
        small = _small_prep(p_small, dn_a_log[i], dn_dt_bias[i], ssm_a_log[i], ssm_dt_bias[i])

        qkv = _dn_conv(p, dn_conv[i], col0=off["dn_qkv"], **dims)
        oa = _dn_readout(*_dn_scan(qkv, small, **dims), p, dn_norm_g[i], col_gate=off["dn_gate"])

        shifted = _rw_shift(p, _pack_rw_vec(rw_mu[i]), col0=off["rw_rkv"], **dims)
        r, v, nk, gate, bonus, lw, kd, b = _rw_prep(
            shifted, rw_w0[i], rw_w2[i], rw_a0[i], rw_a2[i], rw_g2[i], rw_k_k[i], rw_k_a[i],
            rw_r_k[i])
        ob = _rw_readout(*_rw_scan(r, v, nk, lw, kd, b, **dims), bonus, gate, rw_ln_g[i],
                         rw_ln_b[i])

        xbc = _conv_silu(p, ssm_conv[i], ssm_conv_b[i], col0=off["ssm_xbc"], n_norm_blocks=0,
                         name="ssm_conv", **dims)
        oc = _ssd_readout(*_ssd_scan(xbc, small, **dims), xbc, p, ssm_d[i], ssm_norm_g[i],
                          col_z=off["ssm_z"])

        y = _branch_mix(oa, ob, oc, p, w_branch[i].astype(BF16), d_model=d)
        xa, f, idx, wgt, rank, counts = _outproj_router(
            y, xa, mod, w_out[i].astype(BF16), norm_ffn_g[i], w_router[i], b_router[i], **dims)
        dest_tiles, tile_expert, n_rows = _route_plan(idx, rank, counts, route_tile)
        xs = _dispatch(f, dest_tiles, n_rows)
        ys = _experts(xs, tile_expert, w_e1, b_e1[i], w_e2, b_e2[i], layer=i)
        xa = _combine(ys, dest_tiles, xa, wgt.T, mod, **dims)
    return _final_norm(xa, final_norm_g, **dims)
```

```python
import functools
import math

import jax
import jax.numpy as jnp
from jax import lax
from jax.experimental import pallas as pl
from jax.experimental.pallas import tpu as pltpu

F32 = jnp.float32
BF16 = jnp.bfloat16

GRID_W = 64
NORM_EPS = 1e-6
L2_EPS = 1e-6

DN_HEADS = 8
DN_DK = 128
DN_CHUNK = 64
DN_QK = DN_HEADS * DN_DK
DN_V = DN_QK

RW_HEADS = 16
RW_HEAD = 64
RW_WIDTH = RW_HEADS * RW_HEAD
RW_DECAY_LORA = 96
RW_A_LORA = 96
RW_GATE_LORA = 256
RW_GN_EPS = 64e-5
RW_CHUNK = 64

SSM_HEADS = 16
SSM_HEADDIM = 64
SSM_INNER = SSM_HEADS * SSM_HEADDIM
SSM_GROUPS = 4
SSM_STATE = 128
SSM_CHUNK = 128
SSM_XBC = SSM_INNER + 2 * SSM_GROUPS * SSM_STATE
SSM_NORM_EPS = 1e-5

N_EXPERTS = 32
TOP_K = 4
D_EXPERT = 512
SWIGLU_LIMIT = 7.0
SWIGLU_ALPHA = 1.702

V7X_VMEM_LIMIT = 56 * 1024 * 1024
LANES = 128

SCAN_OUT_DTYPE = jnp.bfloat16

IN_TILE_N = 512
IN_SUB_ROWS = 384


def _p_layout(d_model):
    off = {}
    pos = 0
    for name, width in (("gate", 3 * d_model), ("dn_qkv", 2 * DN_QK + DN_V), ("dn_gate", DN_V),
                        ("rw_rkv", 3 * RW_WIDTH), ("rw_lora", 1024), ("ssm_z", SSM_INNER),
                        ("ssm_xbc", SSM_XBC), ("small", IN_TILE_N)):
        off[name] = pos
        pos += width
    off["main"] = off["small"]
    off["total"] = pos
    return off


RWL_W, RWL_A, RWL_G = 0, 256, 512
SM_BETA, SM_A, SM_DT = 0, 16, 32


def _cparams(sem):
    return pltpu.CompilerParams(dimension_semantics=sem, vmem_limit_bytes=V7X_VMEM_LIMIT)


def _tile(n, cap, mult=8):
    best = None
    for t in range(mult, min(n, cap) + 1, mult):
        if n % t == 0:
            best = t
    assert best is not None, (n, cap)
    return best


def _dot(a, b):
    return jnp.dot(a.astype(BF16), b.astype(BF16), preferred_element_type=F32)


def _dot_nt(a, b):
    return lax.dot_general(a.astype(BF16), b.astype(BF16), (((1,), (1,)), ((), ())),
                           preferred_element_type=F32)


def _dot_tn(a, b):
    return lax.dot_general(a.astype(BF16), b.astype(BF16), (((0,), (0,)), ((), ())),
                           preferred_element_type=F32)


def _split2(a):
    hi = a.astype(BF16)
    lo = (a - hi.astype(F32)).astype(BF16)
    return hi, lo


def _dot_hilo_lhs(m_bf16, x):
    hi, lo = _split2(x)
    d = functools.partial(jnp.dot, preferred_element_type=F32)
    return d(m_bf16, hi) + d(m_bf16, lo)


def _dot_exact_rhs(x, m_bf16):
    x1 = x.astype(BF16)
    r1 = x - x1.astype(F32)
    x2 = r1.astype(BF16)
    x3 = (r1 - x2.astype(F32)).astype(BF16)
    d = functools.partial(jnp.dot, preferred_element_type=F32)
    return d(x1, m_bf16) + d(x2, m_bf16) + d(x3, m_bf16)


def _pack_bf16_pair(a, b):
    def rounded(x):
        u = lax.bitcast_convert_type(x, jnp.uint32)
        return u + jnp.uint32(0x7FFF) + ((u >> 16) & jnp.uint32(1))
    return (rounded(a) >> 16) | (rounded(b) & jnp.uint32(0xFFFF0000))


def _unpack_bf16_pair(p):
    lo = lax.bitcast_convert_type(p << 16, F32)
    hi = lax.bitcast_convert_type(p & jnp.uint32(0xFFFF0000), F32)
    return lo, hi


def _sigmoid(x):
    return 1.0 / (1.0 + jnp.exp(-x))


def _silu(x):
    return x * _sigmoid(x)


def _softplus(x):
    return jnp.maximum(x, 0.0) + jnp.log(1.0 + jnp.exp(-jnp.abs(x)))


def _inv_unit_lower(neg_lowers, size):
    shape = neg_lowers[0].shape
    ri = lax.broadcasted_iota(jnp.int32, shape, 0)
    ci = lax.broadcasted_iota(jnp.int32, shape, 1)
    eye = jnp.where(ri == ci, 1.0, 0.0)
    ps = [eye + n for n in neg_lowers]
    qs = list(neg_lowers)
    power = 2
    while power < size:
        qs = [_dot(q, q) for q in qs]
        ps = [p + _dot(p, q) for p, q in zip(ps, qs)]
        power *= 2
    return ps


def _mod_kernel(c_ref, w_ref, b_ref, o_ref):
    c = c_ref[...]
    s = _silu(c)
    o_ref[0] = jnp.dot(s, w_ref[0], preferred_element_type=F32,
                       precision=lax.Precision.HIGHEST) + b_ref[0]


def _modulation(cond, w_mod, b_mod):
    n_layers, d, n6 = w_mod.shape
    rows = cond.shape[0]
    tn = _tile(n6, 1024, LANES)
    return pl.pallas_call(
        _mod_kernel,
        grid=(n_layers, n6 // tn),
        in_specs=[pl.BlockSpec((rows, d), lambda l, n: (0, 0)),
                  pl.BlockSpec((1, d, tn), lambda l, n: (l, 0, n)),
                  pl.BlockSpec((1, 1, tn), lambda l, n: (l, 0, n))],
        out_specs=pl.BlockSpec((1, rows, tn), lambda l, n: (l, 0, n)),
        out_shape=jax.ShapeDtypeStruct((n_layers, rows, n6), F32),
        compiler_params=_cparams(("arbitrary", "arbitrary")),
        name="modulation",
    )(cond, w_mod, b_mod.reshape(n_layers, 1, n6))


def _modulated_norm(x, g, ml, mc, is_ctx, shift_row, scale_row):
    y = x * lax.rsqrt(jnp.mean(x * x, axis=-1, keepdims=True) + NORM_EPS) * g
    shift = jnp.where(is_ctx, mc[shift_row:shift_row + 1], ml[shift_row:shift_row + 1])
    scale = jnp.where(is_ctx, mc[scale_row:scale_row + 1], ml[scale_row:scale_row + 1])
    return y * (1.0 + scale) + shift


def _ctx_rows(tile_index, tm, seg, ctx_len):
    row = (tile_index * tm) % seg + lax.broadcasted_iota(jnp.int32, (tm, 1), 0)
    return row < ctx_len


def _inproj_kernel(x_ref, g_ref, ml_ref, mc_ref, w_ref, o_ref, s_ref, h_scr,
                   *, tm, seg, ctx_len, n_main):
    n = pl.program_id(1)

    @pl.when(n == 0)
    def _():
        is_ctx = _ctx_rows(pl.program_id(0), tm, seg, ctx_len)
        rows = _tile(tm, 384)
        for r0 in range(0, tm, rows):
            h = _modulated_norm(x_ref[r0:r0 + rows, :], g_ref[...], ml_ref[0], mc_ref[0],
                                is_ctx[r0:r0 + rows], 0, 1)
            h_scr[r0:r0 + rows, :] = h.astype(BF16)

    def project(dst_ref):
        sub = _tile(tm, IN_SUB_ROWS, 16)
        for r0 in range(0, tm, sub):
            dst_ref[r0:r0 + sub, :] = jnp.dot(h_scr[r0:r0 + sub, :], w_ref[...],
                                              preferred_element_type=F32).astype(dst_ref.dtype)

    @pl.when(n < n_main)
    def _():
        project(o_ref)

    @pl.when(n == n_main)
    def _():
        project(s_ref)


def _in_projection(x, g, mod, w_packed, *, n_batch, seg, ctx_len):
    ntok, d = x.shape
    tn = IN_TILE_N
    n_main = w_packed.shape[1] // tn - 1
    tm = _tile(seg, 1152, 16)
    per = seg // tm
    kern = functools.partial(_inproj_kernel, tm=tm, seg=seg, ctx_len=ctx_len, n_main=n_main)
    return pl.pallas_call(
        kern,
        grid=(ntok // tm, n_main + 1),
        in_specs=[pl.BlockSpec((tm, d), lambda i, n: (i, 0)),
                  pl.BlockSpec((1, d), lambda i, n: (0, 0)),
                  pl.BlockSpec((1, 6, d), lambda i, n: (i // per, 0, 0)),
                  pl.BlockSpec((1, 6, d), lambda i, n: (n_batch, 0, 0)),
                  pl.BlockSpec((d, tn), lambda i, n: (0, n))],
        out_specs=[pl.BlockSpec((tm, tn), lambda i, n: (i, jnp.minimum(n, n_main - 1))),
                   pl.BlockSpec((tm, tn), lambda i, n: (i, 0))],
        out_shape=[jax.ShapeDtypeStruct((ntok, n_main * tn), BF16),
                   jax.ShapeDtypeStruct((ntok, tn), F32)],
        scratch_shapes=[pltpu.VMEM((tm, d), BF16)],
        compiler_params=_cparams(("arbitrary", "arbitrary")),
        name="in_projection",
    )(x, g.reshape(1, d), mod, mod, w_packed)


def _pack_w_in(w_in_l, d_model):
    gate_cols = 3 * d_model
    dn0 = gate_cols
    dn_cols = 2 * DN_QK + 2 * DN_V + 4 * DN_HEADS
    rw0 = dn0 + dn_cols
    rw_cols = 3 * RW_WIDTH + 2 * RW_DECAY_LORA + 2 * RW_A_LORA + RW_GATE_LORA
    ss0 = rw0 + rw_cols
    z = lambda n: jnp.zeros((d_model, n), w_in_l.dtype)
    c = lambda a, n: w_in_l[:, a:a + n]
    dn_small = dn0 + 2 * DN_QK + 2 * DN_V
    rw_l = rw0 + 3 * RW_WIDTH
    parts = [
        c(0, gate_cols),
        c(dn0, 2 * DN_QK + DN_V),
        c(dn0 + 2 * DN_QK + DN_V, DN_V),
        c(rw0, 3 * RW_WIDTH),
        c(rw_l, 2 * RW_DECAY_LORA), z(64),
        c(rw_l + 2 * RW_DECAY_LORA, 2 * RW_A_LORA), z(64),
        c(rw_l + 2 * RW_DECAY_LORA + 2 * RW_A_LORA, RW_GATE_LORA), z(256),
        c(ss0, SSM_INNER),
        c(ss0 + SSM_INNER, SSM_XBC),
        c(dn_small, 4 * DN_HEADS), c(ss0 + SSM_INNER + SSM_XBC, 2 * SSM_HEADS), z(512 - 64),
    ]
    return jnp.concatenate(parts, axis=1).astype(BF16)


CONV_PAD = 72


def _conv_rows(seg, ctx_len):
    ctx0 = CONV_PAD
    lat0 = ctx0 + ctx_len + CONV_PAD
    total = lat0 + (seg - ctx_len) + CONV_PAD
    return ctx0, lat0, total


def _fill_padded(scr, x, seg, ctx_len):
    c = x.shape[1]
    n_lat = seg - ctx_len
    ctx0, lat0, total = _conv_rows(seg, ctx_len)
    xc = x[:ctx_len]
    xl = x[ctx_len:]
    col = lax.broadcasted_iota(jnp.int32, (n_lat, c), 0) % GRID_W
    zpad = jnp.zeros((CONV_PAD, c), F32)
    for i, lat in enumerate((xl, jnp.where(col == GRID_W - 1, 0.0, xl), jnp.where(col == 0, 0.0, xl))):
        scr[i, pl.ds(0, CONV_PAD), :] = zpad
        scr[i, pl.ds(ctx0, ctx_len), :] = xc
        scr[i, pl.ds(ctx0 + ctx_len, CONV_PAD), :] = zpad
        scr[i, pl.ds(lat0, n_lat), :] = lat
        scr[i, pl.ds(lat0 + n_lat, CONV_PAD), :] = zpad


def _dwconv_block(scr, w9, seg, ctx_len):
    n_lat = seg - ctx_len
    ctx0, lat0, _ = _conv_rows(seg, ctx_len)
    acc_c = None
    for dc in (-1, 0, 1):
        k = 3 + (dc + 1)
        term = scr[0, pl.ds(ctx0 + dc, ctx_len), :] * w9[k:k + 1]
        acc_c = term if acc_c is None else acc_c + term
    cols = [scr[{-1: 1, 0: 0, 1: 2}[dc], pl.ds(lat0 + dc, n_lat), :] for dc in (-1, 0, 1)]
    rows = []
    for dr in (-1, 0, 1):
        k = (dr + 1) * 3
        rows.append(cols[0] * w9[k:k + 1] + cols[1] * w9[k + 1:k + 2] + cols[2] * w9[k + 2:k + 3])
    zeros = jnp.zeros((GRID_W, cols[0].shape[1]), F32)
    acc_l = (rows[1] + jnp.concatenate([zeros, rows[0][:n_lat - GRID_W]], axis=0)
             + jnp.concatenate([rows[2][GRID_W:], zeros], axis=0))
    return acc_c, acc_l


def _conv_kernel(x_ref, w_ref, b_ref, o_ref, scr, *, seg, ctx_len, n_norm_blocks):
    _fill_padded(scr, x_ref[...].astype(F32), seg, ctx_len)
    do_norm = pl.program_id(1) < n_norm_blocks
    for part, rows in zip(_dwconv_block(scr, w_ref[...], seg, ctx_len),
                          (pl.ds(0, ctx_len), pl.ds(ctx_len, seg - ctx_len))):
        y = _silu(part + b_ref[...])
        if n_norm_blocks:
            yn = y * lax.rsqrt(jnp.sum(y * y, axis=-1, keepdims=True) + L2_EPS)
            y = jnp.where(do_norm, yn, y)
        o_ref[rows, :] = y.astype(o_ref.dtype)


def _conv_silu(p, conv_w, conv_b, *, n_batch, seg, ctx_len, col0, n_norm_blocks, name):
    ntok = p.shape[0]
    cw = conv_w.shape[-1]
    kern = functools.partial(_conv_kernel, seg=seg, ctx_len=ctx_len, n_norm_blocks=n_norm_blocks)
    return pl.pallas_call(
        kern,
        grid=(n_batch, cw // LANES),
        in_specs=[pl.BlockSpec((seg, LANES), lambda b, j: (b, col0 // LANES + j)),
                  pl.BlockSpec((9, LANES), lambda b, j: (0, j)),
                  pl.BlockSpec((1, LANES), lambda b, j: (0, j))],
        out_specs=pl.BlockSpec((seg, LANES), lambda b, j: (b, j)),
        out_shape=jax.ShapeDtypeStruct((ntok, cw), BF16),
        scratch_shapes=[pltpu.VMEM((3, _conv_rows(seg, ctx_len)[2], LANES), F32)],
        compiler_params=_cparams(("arbitrary", "arbitrary")),
        name=name,
    )(p, conv_w.reshape(9, cw), conv_b.reshape(1, cw))


def _dn_conv(p, conv_w, *, n_batch, seg, ctx_len, col0):
    return _conv_silu(p, conv_w, jnp.zeros((conv_w.shape[-1],), F32), n_batch=n_batch, seg=seg,
                      ctx_len=ctx_len, col0=col0, n_norm_blocks=2 * DN_QK // LANES, name="dn_conv")


SO_BETA, SO_G, SO_DT, SO_A = 0, 16, 32, 64


def _small_kernel(x_ref, par_ref, o_ref):
    x = x_ref[...]
    par = par_ref[...]
    neg_a = -jnp.exp(par[1:2])
    tm = x.shape[0]
    beta = _sigmoid(x[:, SM_BETA:SM_BETA + 16])
    sp = _softplus(x[:, SM_A:SM_A + 48] + par[0:1, SM_A:SM_A + 48])
    g = sp[:, 0:16] * neg_a[:, SM_A:SM_A + 16]
    dt = sp[:, 16:48]
    a = dt * neg_a[:, SM_DT:SM_DT + 32]
    o_ref[...] = jnp.concatenate([beta, g, dt, a, jnp.zeros((tm, LANES - 96), F32)], axis=1)


def _small_prep(ps, dn_a_log, dn_dt_bias, ssm_a_log, ssm_dt_bias):
    ntok = ps.shape[0]
    tm = _tile(ntok, 1024)
    zeros16 = jnp.zeros((16,), F32)
    bias = jnp.concatenate([zeros16, dn_dt_bias.reshape(-1), ssm_dt_bias.reshape(-1),
                            jnp.zeros((LANES - 64,), F32)])
    a_log = jnp.concatenate([zeros16, dn_a_log.reshape(-1), ssm_a_log.reshape(-1),
                             jnp.zeros((LANES - 64,), F32)])
    par = jnp.concatenate([bias[None], a_log[None], jnp.zeros((6, LANES), F32)], axis=0)
    return pl.pallas_call(
        _small_kernel,
        grid=(ntok // tm,),
        in_specs=[pl.BlockSpec((tm, LANES), lambda i: (i, 0)),
                  pl.BlockSpec((8, LANES), lambda i: (0, 0))],
        out_specs=pl.BlockSpec((tm, LANES), lambda i: (i, 0)),
        out_shape=jax.ShapeDtypeStruct((ntok, LANES), F32),
        compiler_params=_cparams(("arbitrary",)),
        name="small_prep",
    )(ps, par)


def _chunk_of_step(d, s, n_ctx, n_all):
    bwd = jnp.where(s < n_ctx, n_ctx - 1 - s, n_all + n_ctx - 1 - s)
    return jnp.where(d == 0, s, bwd)


def _order_masks(d, size):
    ri = lax.broadcasted_iota(jnp.int32, (size, size), 0)
    ci = lax.broadcasted_iota(jnp.int32, (size, size), 1)
    delta = (ri - ci) * (1 - 2 * d)
    return delta >= 0, delta > 0, ri == ci


def _col_to_row(col, eye):
    return jnp.sum(jnp.where(eye, col, 0.0), axis=0, keepdims=True)


def _dn_scan_kernel(qf_ref, kf_ref, vf_ref, smf_ref, qb_ref, kb_ref, vb_ref, smb_ref,
                    of_ref, ob_ref, s_scr):
    c = DN_CHUNK
    h_n = DN_HEADS

    @pl.when(pl.program_id(1) == 0)
    def _():
        s_scr[...] = jnp.zeros(s_scr.shape, F32)

    scale = DN_DK ** -0.5
    refs = ((qf_ref, kf_ref, vf_ref, smf_ref, of_ref), (qb_ref, kb_ref, vb_ref, smb_ref, ob_ref))
    masks = [_order_masks(d, c) for d in range(2)]
    gcs = []
    gtots = []
    betas = []
    for d in range(2):
        sm = refs[d][3][...]
        betas.append(sm[:, SO_BETA + d * h_n:SO_BETA + (d + 1) * h_n])
        g = sm[:, SO_G + d * h_n:SO_G + (d + 1) * h_n]
        gcs.append(_dot_hilo_lhs(masks[d][0].astype(BF16), g))
        gtots.append(jnp.sum(g, axis=0, keepdims=True))

    items = [(d, h) for d in range(2) for h in range(h_n)]
    n = range(len(items))
    lanes = [slice(h * DN_DK, (h + 1) * DN_DK) for _, h in items]
    incl = [masks[d][0] for d, _ in items]
    strict = [masks[d][1] for d, _ in items]
    eye = masks[0][2]
    qs = [refs[d][0][:, lanes[i]].astype(F32) * scale for i, (d, _) in enumerate(items)]
    ks = [refs[d][1][:, lanes[i]].astype(F32) for i, (d, _) in enumerate(items)]
    vs = [refs[d][2][:, lanes[i]].astype(F32) for i, (d, _) in enumerate(items)]
    bcols = [betas[d][:, h:h + 1] for d, h in items]
    gcols = [gcs[d][:, h:h + 1] for d, h in items]
    gts = [gtots[d][:, h:h + 1] for d, h in items]
    decs = [jnp.exp(jnp.where(incl[i], gcols[i] - _col_to_row(gcols[i], eye), 0.0)) for i in n]
    kbs = [ks[i] * bcols[i] for i in n]
    a_s = [_dot_nt(jnp.concatenate([kbs[i], qs[i]], axis=0), ks[i]) for i in n]
    atts = [jnp.where(incl[i], a_s[i][c:] * decs[i], 0.0) for i in n]
    tinvs = _inv_unit_lower([-jnp.where(strict[i], a_s[i][:c] * decs[i], 0.0) for i in n], c)
    egs = [jnp.exp(gcols[i]) for i in n]
    uws = [_dot(tinvs[i], jnp.concatenate([vs[i] * bcols[i], kbs[i] * egs[i]], axis=1))
           for i in n]
    s_olds = [s_scr[d, h] for d, h in items]
    wss = [_dot(jnp.concatenate([uws[i][:, DN_DK:], qs[i] * egs[i]], axis=0), s_olds[i])
           for i in n]
    v_news = [uws[i][:, :DN_DK] - wss[i][:c] for i in n]
    outs = [wss[i][c:] + _dot(atts[i], v_news[i]) for i in n]
    upds = [_dot_tn(ks[i] * jnp.exp(gts[i] - gcols[i]), v_news[i]) for i in n]
    for i, (d, h) in enumerate(items):
        s_scr[d, h] = s_olds[i] * jnp.exp(gts[i]) + upds[i]
        refs[d][4][:, lanes[i]] = outs[i].astype(refs[d][4].dtype)


def _scan_specs(c, widths, n_all, n_ctx):
    def spec(d, width, col):
        return pl.BlockSpec(
            (c, width), lambda b, s: (b * n_all + _chunk_of_step(d, s, n_ctx, n_all), col))
    return [spec(d, w, col) for d in range(2) for w, col in widths]


def _dn_scan(qkv, small, *, n_batch, seg, ctx_len):
    ntok = qkv.shape[0]
    c = DN_CHUNK
    n_all = seg // c
    n_ctx = ctx_len // c
    ins = _scan_specs(c, [(DN_QK, 0), (DN_QK, 1), (DN_V, 2), (LANES, 0)], n_all, n_ctx)
    outs = _scan_specs(c, [(DN_V, 0)], n_all, n_ctx)
    return pl.pallas_call(
        _dn_scan_kernel,
        grid=(n_batch, n_all),
        in_specs=ins,
        out_specs=outs,
        out_shape=[jax.ShapeDtypeStruct((ntok, DN_V), SCAN_OUT_DTYPE)] * 2,
        scratch_shapes=[pltpu.VMEM((2, DN_HEADS, DN_DK, DN_DK), F32)],
        compiler_params=_cparams(("arbitrary", "arbitrary")),
        name="dn_scan",
    )(qkv, qkv, qkv, small, qkv, qkv, qkv, small)


def _dn_readout_kernel(of_ref, ob_ref, gate_ref, g_ref, y_ref):
    o = of_ref[...].astype(F32) + ob_ref[...].astype(F32)
    tm = o.shape[0]
    gate = gate_ref[...].astype(F32)
    outs = []
    for h in range(DN_HEADS):
        lanes = slice(h * DN_DK, (h + 1) * DN_DK)
        oh = o[:, lanes]
        yh = oh * lax.rsqrt(jnp.mean(oh * oh, axis=-1, keepdims=True) + NORM_EPS) * g_ref[...]
        outs.append(yh * _silu(gate[:, lanes]))
    y_ref[...] = jnp.concatenate(outs, axis=1).astype(y_ref.dtype)


def _dn_readout(o_f, o_b, p, norm_g, *, col_gate):
    ntok = p.shape[0]
    tm = _tile(ntok, 512)
    tok = pl.BlockSpec((tm, DN_V), lambda i: (i, 0))
    return pl.pallas_call(
        _dn_readout_kernel,
        grid=(ntok // tm,),
        in_specs=[tok, tok,
                  pl.BlockSpec((tm, DN_V), lambda i: (i, col_gate // DN_V)),
                  pl.BlockSpec((1, DN_DK), lambda i: (0, 0))],
        out_specs=tok,
        out_shape=jax.ShapeDtypeStruct((ntok, DN_V), BF16),
        compiler_params=_cparams(("arbitrary",)),
        name="dn_readout",
    )(o_f, o_b, p, norm_g.reshape(1, DN_DK))


def _head_expander(n_heads, head_dim):
    hi = lax.broadcasted_iota(jnp.int32, (n_heads, n_heads * head_dim), 0)
    li = lax.broadcasted_iota(jnp.int32, (n_heads, n_heads * head_dim), 1)
    lo = hi * head_dim
    return jnp.where((li >= lo) & (li < lo + head_dim), 1.0, 0.0).astype(BF16)


def _ssd_scan_kernel(xf_ref, bf_ref, cf_ref, smf_ref, xb_ref, bb_ref, cb_ref, smb_ref,
                     yf_ref, yb_ref, s_scr):
    c = SSM_CHUNK
    hn = SSM_HEADS
    hd = SSM_HEADDIM
    per_group = hn // SSM_GROUPS
    gw = per_group * hd

    @pl.when(pl.program_id(1) == 0)
    def _():
        s_scr[...] = jnp.zeros(s_scr.shape, F32)

    refs = ((xf_ref, bf_ref, cf_ref, smf_ref, yf_ref), (xb_ref, bb_ref, cb_ref, smb_ref, yb_ref))
    expand = _head_expander(hn, hd)
    lane_lo = lax.broadcasted_iota(jnp.int32, (c, 2 * hd), 1) < hd
    pre = []
    for d in range(2):
        incl, _, eye = _order_masks(d, c)
        sm = refs[d][3][...]
        dt = sm[:, SO_DT + d * hn:SO_DT + (d + 1) * hn]
        a = sm[:, SO_A + d * hn:SO_A + (d + 1) * hn]
        acum = _dot_hilo_lhs(incl.astype(BF16), a)
        acum_x = _dot_exact_rhs(acum, expand)
        atot_x = jnp.sum(_dot_exact_rhs(a, expand), axis=0, keepdims=True)
        xdt = refs[d][0][...].astype(F32) * _dot_exact_rhs(dt, expand)
        pre.append(dict(incl=incl, eye=eye, acum=acum, xdt=xdt,
                        xdec=xdt * jnp.exp(atot_x - acum_x),
                        eacum=jnp.exp(acum_x), etot=jnp.exp(atot_x)))

    dgs = [(d, g) for d in range(2) for g in range(SSM_GROUPS)]
    bgs = [refs[d][1][:, g * SSM_STATE:(g + 1) * SSM_STATE] for d, g in dgs]
    cgs = [refs[d][2][:, g * SSM_STATE:(g + 1) * SSM_STATE] for d, g in dgs]
    cbs = [_dot_nt(cgs[i], bgs[i]) for i in range(len(dgs))]
    ms = {}
    for d in range(2):
        p = pre[d]
        for h in range(hn):
            acol = p["acum"][:, h:h + 1]
            lmat = jnp.exp(jnp.where(p["incl"], acol - _col_to_row(acol, p["eye"]), 0.0))
            ms[d, h] = jnp.where(p["incl"], cbs[d * SSM_GROUPS + h // per_group] * lmat, 0.0)
    ydiag = {}
    for d in range(2):
        for pair in range(hn // 2):
            xp = pre[d]["xdt"][:, pair * 2 * hd:(pair + 1) * 2 * hd]
            ydiag[d, pair] = jnp.where(lane_lo, _dot(ms[d, 2 * pair], xp),
                                       _dot(ms[d, 2 * pair + 1], xp))
    s_olds = [s_scr[d, g] for d, g in dgs]
    yoffs = [_dot(cgs[i], s_olds[i]) for i in range(len(dgs))]
    sts = [_dot_tn(bgs[i], pre[d]["xdec"][:, g * gw:(g + 1) * gw])
           for i, (d, g) in enumerate(dgs)]
    for i, (d, g) in enumerate(dgs):
        lanes = slice(g * gw, (g + 1) * gw)
        s_scr[d, g] = s_olds[i] * pre[d]["etot"][:, lanes] + sts[i]
        yd = jnp.concatenate(
            [ydiag[d, pr] for pr in range(g * per_group // 2, (g + 1) * per_group // 2)], axis=1)
        refs[d][4][:, lanes] = (yd + yoffs[i] * pre[d]["eacum"][:, lanes]).astype(refs[d][4].dtype)


def _ssd_scan(xbc, small, *, n_batch, seg, ctx_len):
    ntok = xbc.shape[0]
    c = SSM_CHUNK
    n_all = seg // c
    n_ctx = ctx_len // c
    gn = SSM_GROUPS * SSM_STATE
    ins = _scan_specs(c, [(SSM_INNER, 0), (gn, SSM_INNER // gn), (gn, SSM_INNER // gn + 1),
                          (LANES, 0)], n_all, n_ctx)
    outs = _scan_specs(c, [(SSM_INNER, 0)], n_all, n_ctx)
    return pl.pallas_call(
        _ssd_scan_kernel,
        grid=(n_batch, n_all),
        in_specs=ins,
        out_specs=outs,
        out_shape=[jax.ShapeDtypeStruct((ntok, SSM_INNER), SCAN_OUT_DTYPE)] * 2,
        scratch_shapes=[pltpu.VMEM((2, SSM_GROUPS, SSM_STATE, SSM_INNER // SSM_GROUPS), F32)],
        compiler_params=_cparams(("arbitrary", "arbitrary")),
        name="ssd_scan",
    )(xbc, xbc, xbc, small, xbc, xbc, xbc, small)


def _ssd_readout_kernel(yf_ref, yb_ref, xs_ref, z_ref, d_ref, g_ref, o_ref):
    y = ((yf_ref[...].astype(F32) + yb_ref[...].astype(F32)
          + d_ref[...] * xs_ref[...].astype(F32)) * _silu(z_ref[...].astype(F32)))
    gw = SSM_INNER // SSM_GROUPS
    outs = []
    for g in range(SSM_GROUPS):
        lanes = slice(g * gw, (g + 1) * gw)
        yg = y[:, lanes]
        outs.append(yg * lax.rsqrt(jnp.mean(yg * yg, axis=-1, keepdims=True) + SSM_NORM_EPS)
                    * g_ref[:, lanes])
    o_ref[...] = jnp.concatenate(outs, axis=1).astype(o_ref.dtype)


def _ssd_readout(y_f, y_b, xbc, p, d_skip, norm_g, *, col_z):
    ntok = p.shape[0]
    tm = _tile(ntok, 512)
    w = SSM_INNER
    return pl.pallas_call(
        _ssd_readout_kernel,
        grid=(ntok // tm,),
        in_specs=[pl.BlockSpec((tm, w), lambda i: (i, 0)),
                  pl.BlockSpec((tm, w), lambda i: (i, 0)),
                  pl.BlockSpec((tm, w), lambda i: (i, 0)),
                  pl.BlockSpec((tm, w), lambda i: (i, col_z // w)),
                  pl.BlockSpec((1, w), lambda i: (0, 0)),
                  pl.BlockSpec((1, w), lambda i: (0, 0))],
        out_specs=pl.BlockSpec((tm, w), lambda i: (i, 0)),
        out_shape=jax.ShapeDtypeStruct((ntok, w), BF16),
        compiler_params=_cparams(("arbitrary",)),
        name="ssd_readout",
    )(y_f, y_b, xbc, p, jnp.repeat(d_skip, SSM_HEADDIM).reshape(1, w), norm_g.reshape(1, w))


def _rw_shift_kernel(x_ref, mu_ref, o_ref, scr, *, seg, ctx_len):
    x = x_ref[...].astype(F32)
    n_lat = seg - ctx_len
    _fill_padded(scr, x, seg, ctx_len)
    ctx0, lat0, _ = _conv_rows(seg, ctx_len)
    mu = mu_ref[...]
    lane_c = lax.broadcasted_iota(jnp.int32, (ctx_len, LANES), 1)
    prev = scr[0, pl.ds(ctx0 - 1, ctx_len), :]
    nxt = scr[0, pl.ds(ctx0 + 1, ctx_len), :]
    xc = x[:ctx_len]
    o_ref[pl.ds(0, ctx_len), :] = (xc + (jnp.where(lane_c % 2 == 0, prev, nxt) - xc) * mu
                                   ).astype(o_ref.dtype)
    sel = lax.broadcasted_iota(jnp.int32, (n_lat, LANES), 1) % 4
    left = scr[1, pl.ds(lat0 - 1, n_lat), :]
    right = scr[2, pl.ds(lat0 + 1, n_lat), :]
    up = scr[0, pl.ds(lat0 - GRID_W, n_lat), :]
    down = scr[0, pl.ds(lat0 + GRID_W, n_lat), :]
    shifted = jnp.where(sel == 0, left, jnp.where(sel == 1, right, jnp.where(sel == 2, up, down)))
    xl = x[ctx_len:]
    o_ref[pl.ds(ctx_len, n_lat), :] = (xl + (shifted - xl) * mu).astype(o_ref.dtype)


def _rw_shift(p, mu_packed, *, n_batch, seg, ctx_len, col0):
    ntok = p.shape[0]
    cw = mu_packed.shape[-1]
    kern = functools.partial(_rw_shift_kernel, seg=seg, ctx_len=ctx_len)
    return pl.pallas_call(
        kern,
        grid=(n_batch, cw // LANES),
        in_specs=[pl.BlockSpec((seg, LANES), lambda b, j: (b, col0 // LANES + j)),
                  pl.BlockSpec((1, LANES), lambda b, j: (0, j))],
        out_specs=pl.BlockSpec((seg, LANES), lambda b, j: (b, j)),
        out_shape=jax.ShapeDtypeStruct((ntok, cw), BF16),
        scratch_shapes=[pltpu.VMEM((3, _conv_rows(seg, ctx_len)[2], LANES), F32)],
        compiler_params=_cparams(("arbitrary", "arbitrary")),
        name="rw_shift",
    )(p, mu_packed.reshape(1, cw))


def _head_sum(x, n_heads, head_dim):
    expand = _head_expander(n_heads, head_dim)
    hi = lax.broadcasted_iota(jnp.int32, (n_heads * head_dim, n_heads), 1)
    li = lax.broadcasted_iota(jnp.int32, (n_heads * head_dim, n_heads), 0)
    lo = hi * head_dim
    reduce = jnp.where((li >= lo) & (li < lo + head_dim), 1.0, 0.0).astype(BF16)
    return _dot(_dot(x, reduce), expand)


def _rw_prep_kernel(x_ref, w0_ref, w2_ref, a0_ref, a2_ref, g2_ref, kk_ref, ka_ref, rk_ref,
                    r_ref, v_ref, nk_ref, gate_ref, bonus_ref, lw_ref, kd_ref, b_ref):
    w = RW_WIDTH
    r = x_ref[:, 0:w].astype(F32)
    k = x_ref[:, w:2 * w].astype(F32)
    v = x_ref[:, 2 * w:3 * w].astype(F32)
    lora = 3 * w
    wl = x_ref[:, lora + RWL_W:lora + RWL_W + 256].astype(F32)
    al = x_ref[:, lora + RWL_A:lora + RWL_A + 256]
    gl = x_ref[:, lora + RWL_G:lora + RWL_G + RW_GATE_LORA].astype(F32)
    w_raw = w0_ref[...] + _dot(jnp.tanh(wl), w2_ref[...])
    log_decay = -math.exp(-0.5) * _sigmoid(w_raw)
    icl = _sigmoid(a0_ref[...] + _dot(al, a2_ref[...]))
    gate_ref[...] = _dot(_sigmoid(gl), g2_ref[...]).astype(gate_ref.dtype)
    kx = k * kk_ref[...]
    nk = kx * lax.rsqrt(_head_sum(kx * kx, RW_HEADS, RW_HEAD) + L2_EPS)
    r_ref[...] = r.astype(r_ref.dtype)
    v_ref[...] = v.astype(v_ref.dtype)
    nk_ref[...] = nk.astype(nk_ref.dtype)
    ksum = None
    for d in range(2):
        icl_d = icl[:, d * w:(d + 1) * w]
        k_d = k * (1.0 + (icl_d - 1.0) * ka_ref[...])
        lw_ref[d] = log_decay[:, d * w:(d + 1) * w]
        kd_ref[d] = k_d.astype(kd_ref.dtype)
        b_ref[d] = (nk * icl_d).astype(b_ref.dtype)
        ksum = k_d if ksum is None else ksum + k_d
    bonus_ref[...] = (_head_sum(r * ksum * rk_ref[...], RW_HEADS, RW_HEAD) * v
                      ).astype(bonus_ref.dtype)


def _pack_rw_vec(vec):
    w3 = 3 * RW_WIDTH
    z = lambda n: jnp.zeros((n,), vec.dtype)
    l0 = w3 + 2 * RW_DECAY_LORA
    l1 = l0 + 2 * RW_A_LORA
    return jnp.concatenate([vec[:w3], vec[w3:l0], z(64), vec[l0:l1], z(64), vec[l1:], z(256)])


def _pack_lora(m2):
    lora, w = m2.shape[1], m2.shape[2]
    out = jnp.zeros((256, 2 * w), F32)
    out = out.at[0:lora, 0:w].set(m2[0])
    out = out.at[lora:2 * lora, w:2 * w].set(m2[1])
    return out.astype(BF16)


def _rw_prep(xs, w0, w2, a0, a2, g2, k_k, k_a, r_k):
    ntok = xs.shape[0]
    w = RW_WIDTH
    tm = _tile(ntok, 256)
    row = lambda a: a.reshape(1, -1)
    full = lambda shape: pl.BlockSpec(shape, lambda i: (0,) * len(shape))
    tok = pl.BlockSpec((tm, w), lambda i: (i, 0))
    tok2 = pl.BlockSpec((2, tm, w), lambda i: (0, i, 0))
    sd = jax.ShapeDtypeStruct
    return pl.pallas_call(
        _rw_prep_kernel,
        grid=(ntok // tm,),
        in_specs=[pl.BlockSpec((tm, 4 * w), lambda i: (i, 0)),
                  full((1, 2 * w)), full((256, 2 * w)), full((1, 2 * w)), full((256, 2 * w)),
                  full((RW_GATE_LORA, w)), full((1, w)), full((1, w)), full((1, w))],
        out_specs=[tok, tok, tok, tok, tok, tok2, tok2, tok2],
        out_shape=[sd((ntok, w), BF16)] * 5
        + [sd((2, ntok, w), F32), sd((2, ntok, w), BF16), sd((2, ntok, w), BF16)],
        compiler_params=_cparams(("arbitrary",)),
        name="rw_prep",
    )(xs, row(w0), _pack_lora(w2), row(a0), _pack_lora(a2), g2.astype(BF16), row(k_k), row(k_a),
      row(r_k))


def _stack_heads(x, lane_lo):
    return jnp.concatenate([jnp.where(lane_lo, x, 0.0), jnp.where(lane_lo, 0.0, x)], axis=0)


def _rw_scan_kernel(rf_ref, vf_ref, nkf_ref, lwf_ref, kdf_ref, bf_ref,
                    rb_ref, vb_ref, nkb_ref, lwb_ref, kdb_ref, bb_ref, yf_ref, yb_ref, h_scr):
    c = RW_CHUNK
    pw = 2 * RW_HEAD
    n_pairs = RW_HEADS // 2

    @pl.when(pl.program_id(1) == 0)
    def _():
        h_scr[...] = jnp.zeros(h_scr.shape, F32)

    refs = ((rf_ref, vf_ref, nkf_ref, lwf_ref, kdf_ref, bf_ref, yf_ref),
            (rb_ref, vb_ref, nkb_ref, lwb_ref, kdb_ref, bb_ref, yb_ref))
    rj = lax.broadcasted_iota(jnp.int32, (pw, pw), 0)
    cj = lax.broadcasted_iota(jnp.int32, (pw, pw), 1)
    eye_p = rj == cj
    lane_lo = lax.broadcasted_iota(jnp.int32, (c, pw), 1) < RW_HEAD
    ri = lax.broadcasted_iota(jnp.int32, (2 * c, 2 * c), 0) % c
    ci = lax.broadcasted_iota(jnp.int32, (2 * c, 2 * c), 1) % c

    per_dir = []
    for d in range(2):
        r_ref, v_ref, nk_ref, lw_ref, kd_ref, b_ref, _ = refs[d]
        incl, _, _ = _order_masks(d, c)
        lw = lw_ref[0]
        cw = _dot_hilo_lhs(incl.astype(BF16), lw)
        tot = jnp.sum(lw, axis=0, keepdims=True)
        e_neg = jnp.exp(-cw)
        e_rem = jnp.exp(tot - cw)
        b = b_ref[0].astype(F32)
        kd = kd_ref[0].astype(F32)
        delta = (ri - ci) * (1 - 2 * d)
        incl2 = delta >= 0
        per_dir.append(dict(
            a_t=-nk_ref[...].astype(F32) * jnp.exp(cw - lw), r_t=r_ref[...].astype(F32) * jnp.exp(cw),
            b_t=b * e_neg, k_t=kd * e_neg, b_h=b * e_rem, k_h=kd * e_rem,
            v=v_ref[...].astype(F32), e_tot=jnp.exp(tot), strict2=delta > 0,
            mask_y=jnp.concatenate([incl2, incl2], axis=1)))

    items = [(d, p) for d in range(2) for p in range(n_pairs)]
    n = range(len(items))
    ln = [slice(p * pw, (p + 1) * pw) for _, p in items]
    dd = [per_dir[d] for d, _ in items]

    def stacked(i, top, bottom):
        return jnp.concatenate([_stack_heads(dd[i][top][:, ln[i]], lane_lo),
                                _stack_heads(dd[i][bottom][:, ln[i]], lane_lo)], axis=0)

    xs = [stacked(i, "a_t", "r_t") for i in n]
    ys = [stacked(i, "b_t", "k_t") for i in n]
    vst = [_stack_heads(dd[i]["v"][:, ln[i]], lane_lo) for i in n]
    gs = [_dot_nt(xs[i], ys[i]) for i in n]
    tinvs = _inv_unit_lower(
        [jnp.where(dd[i]["strict2"], gs[i][:2 * c, :2 * c], 0.0) for i in n], c)
    h_olds = [h_scr[d, p] for d, p in items]
    xh = [_dot(xs[i], h_olds[i]) for i in n]
    rhs = [xh[i][:2 * c] + _dot(jnp.where(dd[i]["strict2"], gs[i][:2 * c, 2 * c:], 0.0), vst[i])
           for i in n]
    us = [_dot(tinvs[i], rhs[i]) for i in n]
    uv = [jnp.concatenate([us[i], vst[i]], axis=0) for i in n]
    yst = [xh[i][2 * c:] + _dot(jnp.where(dd[i]["mask_y"], gs[i][2 * c:, :], 0.0), uv[i])
           for i in n]
    upd = [_dot_tn(stacked(i, "b_h", "k_h"), uv[i]) for i in n]
    for i, (d, p) in enumerate(items):
        e_col = jnp.sum(jnp.where(eye_p, dd[i]["e_tot"][:, ln[i]], 0.0), axis=1, keepdims=True)
        h_scr[d, p] = h_olds[i] * e_col + upd[i]
        refs[d][6][:, ln[i]] = (yst[i][:c] + yst[i][c:]).astype(refs[d][6].dtype)


def _rw_scan(r, v, nk, lw, kd, b, *, n_batch, seg, ctx_len):
    ntok = r.shape[0]
    c = RW_CHUNK
    w = RW_WIDTH
    n_all = seg // c
    n_ctx = ctx_len // c
    def row(d):
        return lambda bb, s: bb * n_all + _chunk_of_step(d, s, n_ctx, n_all)

    def tok(d):
        return pl.BlockSpec((c, w), lambda bb, s: (row(d)(bb, s), 0))

    def tok2(d):
        return pl.BlockSpec((1, c, w), lambda bb, s: (d, row(d)(bb, s), 0))

    return pl.pallas_call(
        _rw_scan_kernel,
        grid=(n_batch, n_all),
        in_specs=[spec(d) for d in range(2) for spec in (tok, tok, tok, tok2, tok2, tok2)],
        out_specs=[tok(0), tok(1)],
        out_shape=[jax.ShapeDtypeStruct((ntok, w), SCAN_OUT_DTYPE)] * 2,
        scratch_shapes=[pltpu.VMEM((2, RW_HEADS // 2, 2 * RW_HEAD, 2 * RW_HEAD), F32)],
        compiler_params=_cparams(("arbitrary", "arbitrary")),
        name="rw_scan",
    )(r, v, nk, lw, kd, b, r, v, nk, lw, kd, b)


def _rw_readout_kernel(yf_ref, yb_ref, bonus_ref, gate_ref, g_ref, b_ref, o_ref):
    inv_n = 1.0 / RW_HEAD
    tm = o_ref.shape[0]
    halves = [slice(0, tm // 2), slice(tm // 2, tm)]
    ys = [yf_ref[h, :].astype(F32) + yb_ref[h, :].astype(F32) for h in halves]
    means = [_head_sum(y, RW_HEADS, RW_HEAD) * inv_n for y in ys]
    ycs = [y - m for y, m in zip(ys, means)]
    vs = [_head_sum(yc * yc, RW_HEADS, RW_HEAD) * inv_n for yc in ycs]
    for h, yc, var in zip(halves, ycs, vs):
        yn = yc * lax.rsqrt(var + RW_GN_EPS) * g_ref[...] + b_ref[...]
        o_ref[h, :] = ((yn + bonus_ref[h, :].astype(F32)) * gate_ref[h, :].astype(F32)
                       ).astype(o_ref.dtype)


def _rw_readout(y_f, y_b, bonus, gate, ln_g, ln_b):
    ntok = bonus.shape[0]
    w = RW_WIDTH
    tm = _tile(ntok, 512)
    tok = pl.BlockSpec((tm, w), lambda i: (i, 0))
    vec = pl.BlockSpec((1, w), lambda i: (0, 0))
    return pl.pallas_call(
        _rw_readout_kernel,
        grid=(ntok // tm,),
        in_specs=[tok, tok, tok, tok, vec, vec],
        out_specs=tok,
        out_shape=jax.ShapeDtypeStruct((ntok, w), BF16),
        compiler_params=_cparams(("arbitrary",)),
        name="rw_readout",
    )(y_f, y_b, bonus, gate, ln_g.reshape(1, w), ln_b.reshape(1, w))


def _branch_mix_kernel(oa_ref, ob_ref, oc_ref, ga_ref, gb_ref, gc_ref, w_ref, y_ref):
    acc = None
    for i, (o_ref, g_ref) in enumerate(((oa_ref, ga_ref), (ob_ref, gb_ref), (oc_ref, gc_ref))):
        term = (_sigmoid(g_ref[...].astype(F32))
                * jnp.dot(o_ref[...], w_ref[i], preferred_element_type=F32))
        acc = term if acc is None else acc + term
    y_ref[...] = acc.astype(y_ref.dtype)


def _branch_mix(oa, ob, oc, p, w_branch_bf16, *, d_model):
    ntok = p.shape[0]
    bw = oa.shape[1]
    tm = _tile(ntok, 512)
    tn = 512
    nb = d_model // tn
    o_spec = pl.BlockSpec((tm, bw), lambda i, n: (i, 0))
    g_spec = lambda br: pl.BlockSpec((tm, tn), lambda i, n: (i, br * nb + n))
    return pl.pallas_call(
        _branch_mix_kernel,
        grid=(ntok // tm, nb),
        in_specs=[o_spec, o_spec, o_spec, g_spec(0), g_spec(1), g_spec(2),
                  pl.BlockSpec((3, bw, tn), lambda i, n: (0, 0, n))],
        out_specs=pl.BlockSpec((tm, tn), lambda i, n: (i, n)),
        out_shape=jax.ShapeDtypeStruct((ntok, d_model), BF16),
        compiler_params=_cparams(("arbitrary", "arbitrary")),
        name="branch_mix",
    )(oa, ob, oc, p, p, p, w_branch_bf16)


ROUTE_TILE = 256


def _dot3_nt(a, b):
    ah, al = _split2(a)
    bh, bl = _split2(b)
    d = functools.partial(lax.dot_general, dimension_numbers=(((1,), (1,)), ((), ())),
                          preferred_element_type=F32)
    return d(ah, bh) + d(ah, bl) + d(al, bh)


def _outproj_router_kernel(y_ref, x_ref, ml_ref, mc_ref, wo_ref, g_ref, wr_ref, br_ref,
                           xo_ref, f_ref, idx_ref, wgt_ref, rank_ref, cnt_ref, run_scr,
                           *, tm, seg, ctx_len):
    i = pl.program_id(0)

    @pl.when(i == 0)
    def _():
        run_scr[...] = jnp.zeros(run_scr.shape, F32)

    is_ctx = _ctx_rows(i, tm, seg, ctx_len)
    ml = ml_ref[0]
    mc = mc_ref[0]
    gate = jnp.where(is_ctx, mc[2:3], ml[2:3])
    halves = [slice(0, tm // 2), slice(tm // 2, tm)]
    x_news = [x_ref[h, :] + gate[h] * jnp.dot(y_ref[h, :], wo_ref[...], preferred_element_type=F32)
              for h in halves]
    fs = [_modulated_norm(x_news[j], g_ref[...], ml, mc, is_ctx[h], 3, 4)
          for j, h in enumerate(halves)]
    dh = x_ref.shape[1] // 2
    for j, h in enumerate(halves):
        xo_ref[h, :] = x_news[j]
        f_ref[h, :] = _pack_bf16_pair(fs[j][:, :dh], fs[j][:, dh:])

    ne = N_EXPERTS
    logits = jnp.concatenate([_dot3_nt(wr_ref[...], f) for f in fs], axis=1) + br_ref[...]
    eidx = lax.broadcasted_iota(jnp.int32, (ne, tm), 0)
    vals = logits
    picks = []
    tops = []
    hots = []
    for _ in range(TOP_K):
        m = jnp.max(vals, axis=0, keepdims=True)
        pick = jnp.min(jnp.where(vals == m, eidx, ne), axis=0, keepdims=True)
        hot = eidx == pick
        vals = jnp.where(hot, -jnp.inf, vals)
        tops.append(m)
        picks.append(pick)
        hots.append(hot)
    es = [jnp.exp(t - tops[0]) for t in tops]
    denom = es[0] + es[1] + es[2] + es[3]
    hot_all = jnp.where(hots[0] | hots[1] | hots[2] | hots[3], 1.0, 0.0)
    si = lax.broadcasted_iota(jnp.int32, (tm, tm), 0)
    ti = lax.broadcasted_iota(jnp.int32, (tm, tm), 1)
    earlier = jnp.where(si < ti, 1.0, 0.0).astype(BF16)
    before = jnp.dot(hot_all.astype(BF16), earlier, preferred_element_type=F32)
    rank = run_scr[:, 0:1] + before
    run_new = run_scr[...] + jnp.sum(hot_all, axis=1, keepdims=True)
    run_scr[...] = run_new
    cnt_ref[...] = run_new
    idx_ref[...] = jnp.concatenate(picks, axis=0)
    wgt_ref[...] = jnp.concatenate([e / denom for e in es], axis=0)
    rank_ref[...] = jnp.concatenate(
        [jnp.sum(jnp.where(h, rank, 0.0), axis=0, keepdims=True) for h in hots],
        axis=0).astype(jnp.int32)


def _outproj_router(y, x, mod, w_out_bf16, g_ffn, w_router, b_router, *, n_batch, seg, ctx_len):
    ntok, d = x.shape
    tm = _tile(seg, ROUTE_TILE, LANES)
    per = seg // tm
    ne = N_EXPERTS
    kern = functools.partial(_outproj_router_kernel, tm=tm, seg=seg, ctx_len=ctx_len)
    tok = pl.BlockSpec((tm, d), lambda i: (i, 0))
    sel = pl.BlockSpec((TOP_K, tm), lambda i: (0, i))
    sd = jax.ShapeDtypeStruct
    return pl.pallas_call(
        kern,
        grid=(ntok // tm,),
        in_specs=[tok, tok,
                  pl.BlockSpec((1, 6, d), lambda i: (i // per, 0, 0)),
                  pl.BlockSpec((1, 6, d), lambda i: (n_batch, 0, 0)),
                  pl.BlockSpec((d, d), lambda i: (0, 0)),
                  pl.BlockSpec((1, d), lambda i: (0, 0)),
                  pl.BlockSpec((ne, d), lambda i: (0, 0)),
                  pl.BlockSpec((ne, 1), lambda i: (0, 0))],
        out_specs=[tok, pl.BlockSpec((tm, d // 2), lambda i: (i, 0)), sel, sel, sel,
                   pl.BlockSpec((ne, LANES), lambda i: (0, 0))],
        out_shape=[sd((ntok, d), F32), sd((ntok, d // 2), jnp.uint32), sd((TOP_K, ntok), jnp.int32),
                   sd((TOP_K, ntok), F32), sd((TOP_K, ntok), jnp.int32), sd((ne, LANES), F32)],
        scratch_shapes=[pltpu.VMEM((ne, LANES), F32)],
        compiler_params=_cparams(("arbitrary",)),
        name="outproj_router",
    )(y, x, mod, mod, w_out_bf16, g_ffn.reshape(1, d), w_router.T, b_router.reshape(ne, 1))


EXPERT_TILE = 256
DMA_UNROLL = 4


def _dispatch_kernel(dest_ref, f_ref, xs_in_ref, xs_ref, sem, *, tm):
    del xs_in_ref

    def row_copy(j, k):
        dst = dest_ref[0, 0, k * tm + j]
        return pltpu.make_async_copy(f_ref.at[pl.ds(j, 1), :], xs_ref.at[pl.ds(dst, 1), :], sem)

    def issue(j, carry):
        for k in range(TOP_K):
            row_copy(j, k).start()
        return carry

    lax.fori_loop(0, tm, issue, 0, unroll=DMA_UNROLL)

    def drain(j, carry):
        for k in range(TOP_K):
            row_copy(j, k).wait()
        return carry

    lax.fori_loop(0, tm, drain, 0, unroll=DMA_UNROLL)


def _dispatch(f, dest_tiles, n_rows):
    ntok, d = f.shape
    tm = dest_tiles.shape[-1] // TOP_K
    kern = functools.partial(_dispatch_kernel, tm=tm)
    return pl.pallas_call(
        kern,
        grid=(ntok // tm,),
        in_specs=[pl.BlockSpec((1, 1, TOP_K * tm), lambda i: (i, 0, 0), memory_space=pltpu.SMEM),
                  pl.BlockSpec((tm, d), lambda i: (i, 0)),
                  pl.BlockSpec(memory_space=pl.ANY)],
        out_specs=pl.BlockSpec(memory_space=pl.ANY),
        out_shape=jax.ShapeDtypeStruct((n_rows, d), f.dtype),
        scratch_shapes=[pltpu.SemaphoreType.DMA(())],
        input_output_aliases={2: 0},
        compiler_params=_cparams(("arbitrary",)),
        name="moe_dispatch",
    )(dest_tiles, f, jnp.zeros((n_rows, d), f.dtype))


def _experts_kernel(te_ref, xs_ref, w1_ref, b1_ref, w2_ref, b2_ref, ys_ref, w1_scr, w2_scr):
    i = pl.program_id(0)

    @pl.when((i == 0) | (te_ref[i] != te_ref[jnp.maximum(i - 1, 0)]))
    def _():
        w1_scr[...] = w1_ref[0, 0].astype(BF16)
        w2_scr[...] = w2_ref[0, 0].astype(BF16)

    dh = xs_ref.shape[1]
    x_lo, x_hi = _unpack_bf16_pair(xs_ref[...])
    hu = (jnp.dot(x_lo.astype(BF16), w1_scr[0:dh, :], preferred_element_type=F32)
          + jnp.dot(x_hi.astype(BF16), w1_scr[dh:2 * dh, :], preferred_element_type=F32)
          + b1_ref[0])
    gt = jnp.minimum(hu[:, :D_EXPERT], SWIGLU_LIMIT)
    up = jnp.clip(hu[:, D_EXPERT:], -SWIGLU_LIMIT, SWIGLU_LIMIT)
    act = gt * _sigmoid(SWIGLU_ALPHA * gt) * (up + 1.0)
    y = jnp.dot(act.astype(BF16), w2_scr[...], preferred_element_type=F32) + b2_ref[0]
    ys_ref[...] = _pack_bf16_pair(y[:, :dh], y[:, dh:])


def _experts(xs, tile_expert, w1_all, b1, w2_all, b2, *, layer):
    n_rows, dh = xs.shape
    _, ne, d, de2 = w1_all.shape
    tmx = EXPERT_TILE
    grid_spec = pltpu.PrefetchScalarGridSpec(
        num_scalar_prefetch=1,
        grid=(n_rows // tmx,),
        in_specs=[pl.BlockSpec((tmx, dh), lambda i, te: (i, 0)),
                  pl.BlockSpec((1, 1, d, de2), lambda i, te: (layer, te[i], 0, 0)),
                  pl.BlockSpec((1, 1, de2), lambda i, te: (te[i], 0, 0)),
                  pl.BlockSpec((1, 1, de2 // 2, d), lambda i, te: (layer, te[i], 0, 0)),
                  pl.BlockSpec((1, 1, d), lambda i, te: (te[i], 0, 0))],
        out_specs=pl.BlockSpec((tmx, dh), lambda i, te: (i, 0)),
        scratch_shapes=[pltpu.VMEM((d, de2), BF16), pltpu.VMEM((de2 // 2, d), BF16)],
    )
    return pl.pallas_call(
        _experts_kernel,
        grid_spec=grid_spec,
        out_shape=jax.ShapeDtypeStruct((n_rows, dh), jnp.uint32),
        compiler_params=_cparams(("arbitrary",)),
        name="moe_experts",
    )(tile_expert, xs, w1_all, b1.reshape(ne, 1, de2), w2_all, b2.reshape(ne, 1, d))


def _combine_kernel(dest_ref, x_ref, wgt_ref, ml_ref, mc_ref, ys_ref, xo_ref, rows_scr, sem,
                    *, tm, seg, ctx_len):
    def row_copy(j, k):
        src = dest_ref[0, 0, k * tm + j]
        return pltpu.make_async_copy(ys_ref.at[pl.ds(src, 1), :],
                                     rows_scr.at[k, pl.ds(j, 1), :], sem)

    def issue(j, carry):
        for k in range(TOP_K):
            row_copy(j, k).start()
        return carry

    lax.fori_loop(0, tm, issue, 0, unroll=DMA_UNROLL)

    def drain(j, carry):
        for k in range(TOP_K):
            row_copy(j, k).wait()
        return carry

    lax.fori_loop(0, tm, drain, 0, unroll=DMA_UNROLL)

    wgt = wgt_ref[...]
    y_lo = None
    y_hi = None
    for k in range(TOP_K):
        lo, hi = _unpack_bf16_pair(rows_scr[k])
        wk = wgt[:, k:k + 1]
        y_lo = wk * lo if y_lo is None else y_lo + wk * lo
        y_hi = wk * hi if y_hi is None else y_hi + wk * hi
    is_ctx = _ctx_rows(pl.program_id(0), tm, seg, ctx_len)
    gate = jnp.where(is_ctx, mc_ref[0][5:6], ml_ref[0][5:6])
    xo_ref[...] = x_ref[...] + gate * jnp.concatenate([y_lo, y_hi], axis=1)


def _combine(ys, dest_tiles, x, wgt_t, mod, *, n_batch, seg, ctx_len):
    ntok, d = x.shape
    tm = dest_tiles.shape[-1] // TOP_K
    per = seg // tm
    kern = functools.partial(_combine_kernel, tm=tm, seg=seg, ctx_len=ctx_len)
    tok = pl.BlockSpec((tm, d), lambda i: (i, 0))
    return pl.pallas_call(
        kern,
        grid=(ntok // tm,),
        in_specs=[pl.BlockSpec((1, 1, TOP_K * tm), lambda i: (i, 0, 0), memory_space=pltpu.SMEM),
                  tok,
                  pl.BlockSpec((tm, TOP_K), lambda i: (i, 0)),
                  pl.BlockSpec((1, 6, d), lambda i: (i // per, 0, 0)),
                  pl.BlockSpec((1, 6, d), lambda i: (n_batch, 0, 0)),
                  pl.BlockSpec(memory_space=pl.ANY)],
        out_specs=tok,
        out_shape=jax.ShapeDtypeStruct((ntok, d), F32),
        scratch_shapes=[pltpu.VMEM((TOP_K, tm, ys.shape[1]), ys.dtype), pltpu.SemaphoreType.DMA(())],
        compiler_params=_cparams(("arbitrary",)),
        name="moe_combine",
    )(dest_tiles, x, wgt_t, mod, mod, ys)


def _route_plan(idx, rank, counts, tm):
    tmx = EXPERT_TILE
    ntok = idx.shape[1]
    cnt = counts[:, 0].astype(jnp.int32)
    padded = ((cnt + tmx - 1) // tmx) * tmx
    ends = jnp.cumsum(padded)
    starts = ends - padded
    experts = jnp.arange(N_EXPERTS, dtype=jnp.int32)
    dest = rank + jnp.sum(jnp.where(idx[..., None] == experts, starts, 0), axis=-1)
    n_tiles = (TOP_K * ntok) // tmx + N_EXPERTS
    tile_start = jnp.arange(n_tiles, dtype=jnp.int32) * tmx
    tile_expert = jnp.minimum(jnp.sum((tile_start[:, None] >= ends[None, :]).astype(jnp.int32), axis=1),
                              N_EXPERTS - 1)
    nt = ntok // tm
    dest_tiles = dest.reshape(TOP_K, nt, tm).transpose(1, 0, 2).reshape(nt, 1, TOP_K * tm)
    return dest_tiles, tile_expert, n_tiles * tmx


def _final_norm_kernel(x_ref, g_ref, o_ref):
    x = x_ref[...]
    o_ref[0] = x * lax.rsqrt(jnp.mean(x * x, axis=-1, keepdims=True) + NORM_EPS) * g_ref[...]


def _final_norm(x, g, *, n_batch, seg, ctx_len):
    d = x.shape[1]
    n_lat = seg - ctx_len
    tm = _tile(math.gcd(ctx_len, n_lat), 512)
    per = n_lat // tm
    return pl.pallas_call(
        _final_norm_kernel,
        grid=(n_batch, per),
        in_specs=[pl.BlockSpec((tm, d), lambda b, j: (b * (seg // tm) + ctx_len // tm + j, 0)),
                  pl.BlockSpec((1, d), lambda b, j: (0, 0))],
        out_specs=pl.BlockSpec((1, tm, d), lambda b, j: (b, j, 0)),
        out_shape=jax.ShapeDtypeStruct((n_batch, n_lat, d), F32),
        compiler_params=_cparams(("arbitrary", "arbitrary")),
        name="final_norm",
    )(x, g.reshape(1, d))


def kernel(x, c, ctx, c_ctx, norm_mix_g, norm_ffn_g, w_mod, b_mod, w_in, dn_conv, dn_a_log, dn_dt_bias, dn_norm_g, rw_mu, rw_w0, rw_w2, rw_a0, rw_a2, rw_g2, rw_k_k, rw_k_a, rw_r_k, rw_ln_g, rw_ln_b, ssm_conv, ssm_conv_b, ssm_a_log, ssm_dt_bias, ssm_d, ssm_norm_g, w_branch, w_out, w_router, b_router, w_e1, b_e1, w_e2, b_e2, final_norm_g):
    n_batch, seq, d = x.shape
    ctx_len = ctx.shape[1]
    seg = seq + ctx_len
    n_layers = w_in.shape[0]
    assert seq % GRID_W == 0 and ctx_len % SSM_CHUNK == 0 and seq % SSM_CHUNK == 0
    off = _p_layout(d)
    dims = dict(n_batch=n_batch, seg=seg, ctx_len=ctx_len)

    mod_rows = -(-(n_batch + 1) // 8) * 8
    cond = jnp.concatenate([c, c_ctx[None], jnp.zeros((mod_rows - n_batch - 1, d), F32)], axis=0)
    mod_all = _modulation(cond, w_mod, b_mod).reshape(n_layers, mod_rows, 6, d)

    xa = jnp.concatenate([ctx, x], axis=1).reshape(n_batch * seg, d)
    route_tile = _tile(seg, ROUTE_TILE, LANES)
    for i in range(n_layers):
        mod = mod_all[i]
        p, p_small = _in_projection(xa, norm_mix_g[i], mod, _pack_w_in(w_in[i], d), **dims)
        small = _small_prep(p_small, dn_a_log[i], dn_dt_bias[i], ssm_a_log[i], ssm_dt_bias[i])

        qkv = _dn_conv(p, dn_conv[i], col0=off["dn_qkv"], **dims)
        oa = _dn_readout(*_dn_scan(qkv, small, **dims), p, dn_norm_g[i], col_gate=off["dn_gate"])

        shifted = _rw_shift(p, _pack_rw_vec(rw_mu[i]), col0=off["rw_rkv"], **dims)
        r, v, nk, gate, bonus, lw, kd, b = _rw_prep(
            shifted, rw_w0[i], rw_w2[i], rw_a0[i], rw_a2[i], rw_g2[i], rw_k_k[i], rw_k_a[i],
            rw_r_k[i])
        ob = _rw_readout(*_rw_scan(r, v, nk, lw, kd, b, **dims), bonus, gate, rw_ln_g[i],
                         rw_ln_b[i])

        xbc = _conv_silu(p, ssm_conv[i], ssm_conv_b[i], col0=off["ssm_xbc"], n_norm_blocks=0,
                         name="ssm_conv", **dims)
        oc = _ssd_readout(*_ssd_scan(xbc, small, **dims), xbc, p, ssm_d[i], ssm_norm_g[i],
                          col_z=off["ssm_z"])

        y = _branch_mix(oa, ob, oc, p, w_branch[i].astype(BF16), d_model=d)
        xa, f, idx, wgt, rank, counts = _outproj_router(
            y, xa, mod, w_out[i].astype(BF16), norm_ffn_g[i], w_router[i], b_router[i], **dims)
        dest_tiles, tile_expert, n_rows = _route_plan(idx, rank, counts, route_tile)
        xs = _dispatch(f, dest_tiles, n_rows)
        ys = _experts(xs, tile_expert, w_e1, b_e1[i], w_e2, b_e2[i], layer=i)
        xa = _combine(ys, dest_tiles, xa, wgt.T, mod, **dims)
    return _final_norm(xa, final_norm_g, **dims)
```

```python
import functools
import math

import jax
import jax.numpy as jnp
from jax import lax
from jax.experimental import pallas as pl
from jax.experimental.pallas import tpu as pltpu

F32 = jnp.float32
BF16 = jnp.bfloat16

GRID_W = 64
NORM_EPS = 1e-6
L2_EPS = 1e-6

DN_HEADS = 8
DN_DK = 128
DN_CHUNK = 64
DN_SAMPLES_PER_STEP = 2
DN_QK = DN_HEADS * DN_DK
DN_V = DN_QK

RW_HEADS = 16
RW_HEAD = 64
RW_WIDTH = RW_HEADS * RW_HEAD
RW_DECAY_LORA = 96
RW_A_LORA = 96
RW_GATE_LORA = 256
RW_GN_EPS = 64e-5
RW_CHUNK = 64

SSM_HEADS = 16
SSM_HEADDIM = 64
SSM_INNER = SSM_HEADS * SSM_HEADDIM
SSM_GROUPS = 4
SSM_STATE = 128
SSM_CHUNK = 128
SSM_XBC = SSM_INNER + 2 * SSM_GROUPS * SSM_STATE
SSM_NORM_EPS = 1e-5

N_EXPERTS = 32
TOP_K = 4
D_EXPERT = 512
SWIGLU_LIMIT = 7.0
SWIGLU_ALPHA = 1.702

V7X_VMEM_LIMIT = 56 * 1024 * 1024
LANES = 128

SCAN_OUT_DTYPE = jnp.bfloat16

IN_TILE_N = 512
IN_SUB_ROWS = 384


def _p_layout(d_model):
    off = {}
    pos = 0
    for name, width in (("gate", 3 * d_model), ("dn_qkv", 2 * DN_QK + DN_V), ("dn_gate", DN_V),
                        ("rw_rkv", 3 * RW_WIDTH), ("rw_lora", 1024), ("ssm_z", SSM_INNER),
                        ("ssm_xbc", SSM_XBC), ("small", IN_TILE_N)):
        off[name] = pos
        pos += width
    off["main"] = off["small"]
    off["total"] = pos
    return off


RWL_W, RWL_A, RWL_G = 0, 256, 512
SM_BETA, SM_A, SM_DT = 0, 16, 32


def _cparams(sem):
    return pltpu.CompilerParams(dimension_semantics=sem, vmem_limit_bytes=V7X_VMEM_LIMIT)


def _tile(n, cap, mult=8):
    best = None
    for t in range(mult, min(n, cap) + 1, mult):
        if n % t == 0:
            best = t
    assert best is not None, (n, cap)
    return best


def _dot(a, b):
    return jnp.dot(a.astype(BF16), b.astype(BF16), preferred_element_type=F32)


def _dot_nt(a, b):
    return lax.dot_general(a.astype(BF16), b.astype(BF16), (((1,), (1,)), ((), ())),
                           preferred_element_type=F32)


def _dot_tn(a, b):
    return lax.dot_general(a.astype(BF16), b.astype(BF16), (((0,), (0,)), ((), ())),
                           preferred_element_type=F32)


def _split2(a):
    hi = a.astype(BF16)
    lo = (a - hi.astype(F32)).astype(BF16)
    return hi, lo


def _dot_hilo_lhs(m_bf16, x):
    hi, lo = _split2(x)
    d = functools.partial(jnp.dot, preferred_element_type=F32)
    return d(m_bf16, hi) + d(m_bf16, lo)


def _dot_exact_rhs(x, m_bf16):
    x1 = x.astype(BF16)
    r1 = x - x1.astype(F32)
    x2 = r1.astype(BF16)
    x3 = (r1 - x2.astype(F32)).astype(BF16)
    d = functools.partial(jnp.dot, preferred_element_type=F32)
    return d(x1, m_bf16) + d(x2, m_bf16) + d(x3, m_bf16)


def _pack_bf16_pair(a, b):
    def rounded(x):
        u = lax.bitcast_convert_type(x, jnp.uint32)
        return u + jnp.uint32(0x7FFF) + ((u >> 16) & jnp.uint32(1))
    return (rounded(a) >> 16) | (rounded(b) & jnp.uint32(0xFFFF0000))


def _unpack_bf16_pair(p):
    lo = lax.bitcast_convert_type(p << 16, F32)
    hi = lax.bitcast_convert_type(p & jnp.uint32(0xFFFF0000), F32)
    return lo, hi


def _sigmoid(x):
    return 1.0 / (1.0 + jnp.exp(-x))


def _silu(x):
    return x * _sigmoid(x)


def _softplus(x):
    return jnp.maximum(x, 0.0) + jnp.log(1.0 + jnp.exp(-jnp.abs(x)))


def _inv_unit_lower(neg_lowers, size):
    shape = neg_lowers[0].shape
    ri = lax.broadcasted_iota(jnp.int32, shape, 0)
    ci = lax.broadcasted_iota(jnp.int32, shape, 1)
    eye = jnp.where(ri == ci, 1.0, 0.0)
    ps = [eye + n for n in neg_lowers]
    qs = list(neg_lowers)
    power = 2
    while power < size:
        qs = [_dot(q, q) for q in qs]
        ps = [p + _dot(p, q) for p, q in zip(ps, qs)]
        power *= 2
    return ps


def _mod_kernel(c_ref, w_ref, b_ref, o_ref):
    c = c_ref[...]
    s = _silu(c)
    o_ref[0] = jnp.dot(s, w_ref[0], preferred_element_type=F32,
                       precision=lax.Precision.HIGHEST) + b_ref[0]


def _modulation(cond, w_mod, b_mod):
    n_layers, d, n6 = w_mod.shape
    rows = cond.shape[0]
    tn = _tile(n6, 1024, LANES)
    return pl.pallas_call(
        _mod_kernel,
        grid=(n_layers, n6 // tn),
        in_specs=[pl.BlockSpec((rows, d), lambda l, n: (0, 0)),
                  pl.BlockSpec((1, d, tn), lambda l, n: (l, 0, n)),
                  pl.BlockSpec((1, 1, tn), lambda l, n: (l, 0, n))],
        out_specs=pl.BlockSpec((1, rows, tn), lambda l, n: (l, 0, n)),
        out_shape=jax.ShapeDtypeStruct((n_layers, rows, n6), F32),
        compiler_params=_cparams(("arbitrary", "arbitrary")),
        name="modulation",
    )(cond, w_mod, b_mod.reshape(n_layers, 1, n6))


def _modulated_norm(x, g, ml, mc, is_ctx, shift_row, scale_row):
    y = x * lax.rsqrt(jnp.mean(x * x, axis=-1, keepdims=True) + NORM_EPS) * g
    shift = jnp.where(is_ctx, mc[shift_row:shift_row + 1], ml[shift_row:shift_row + 1])
    scale = jnp.where(is_ctx, mc[scale_row:scale_row + 1], ml[scale_row:scale_row + 1])
    return y * (1.0 + scale) + shift


def _ctx_rows(tile_index, tm, seg, ctx_len):
    row = (tile_index * tm) % seg + lax.broadcasted_iota(jnp.int32, (tm, 1), 0)
    return row < ctx_len


def _inproj_kernel(x_ref, g_ref, ml_ref, mc_ref, w_ref, o_ref, s_ref, h_scr,
                   *, tm, seg, ctx_len, n_main):
    n = pl.program_id(1)

    @pl.when(n == 0)
    def _():
        is_ctx = _ctx_rows(pl.program_id(0), tm, seg, ctx_len)
        rows = _tile(tm, 384)
        for r0 in range(0, tm, rows):
            h = _modulated_norm(x_ref[r0:r0 + rows, :], g_ref[...], ml_ref[0], mc_ref[0],
                                is_ctx[r0:r0 + rows], 0, 1)
            h_scr[r0:r0 + rows, :] = h.astype(BF16)

    def project(dst_ref):
        sub = _tile(tm, IN_SUB_ROWS, 16)
        for r0 in range(0, tm, sub):
            dst_ref[r0:r0 + sub, :] = jnp.dot(h_scr[r0:r0 + sub, :], w_ref[...],
                                              preferred_element_type=F32).astype(dst_ref.dtype)

    @pl.when(n < n_main)
    def _():
        project(o_ref)

    @pl.when(n == n_main)
    def _():
        project(s_ref)


def _in_projection(x, g, mod, w_packed, *, n_batch, seg, ctx_len):
    ntok, d = x.shape
    tn = IN_TILE_N
    n_main = w_packed.shape[1] // tn - 1
    tm = _tile(seg, 1152, 16)
    per = seg // tm
    kern = functools.partial(_inproj_kernel, tm=tm, seg=seg, ctx_len=ctx_len, n_main=n_main)
    return pl.pallas_call(
        kern,
        grid=(ntok // tm, n_main + 1),
        in_specs=[pl.BlockSpec((tm, d), lambda i, n: (i, 0)),
                  pl.BlockSpec((1, d), lambda i, n: (0, 0)),
                  pl.BlockSpec((1, 6, d), lambda i, n: (i // per, 0, 0)),
                  pl.BlockSpec((1, 6, d), lambda i, n: (n_batch, 0, 0)),
                  pl.BlockSpec((d, tn), lambda i, n: (0, n))],
        out_specs=[pl.BlockSpec((tm, tn), lambda i, n: (i, jnp.minimum(n, n_main - 1))),
                   pl.BlockSpec((tm, tn), lambda i, n: (i, 0))],
        out_shape=[jax.ShapeDtypeStruct((ntok, n_main * tn), BF16),
                   jax.ShapeDtypeStruct((ntok, tn), F32)],
        scratch_shapes=[pltpu.VMEM((tm, d), BF16)],
        compiler_params=_cparams(("arbitrary", "arbitrary")),
        name="in_projection",
    )(x, g.reshape(1, d), mod, mod, w_packed)


def _pack_w_in(w_in_l, d_model):
    gate_cols = 3 * d_model
    dn0 = gate_cols
    dn_cols = 2 * DN_QK + 2 * DN_V + 4 * DN_HEADS
    rw0 = dn0 + dn_cols
    rw_cols = 3 * RW_WIDTH + 2 * RW_DECAY_LORA + 2 * RW_A_LORA + RW_GATE_LORA
    ss0 = rw0 + rw_cols
    z = lambda n: jnp.zeros((d_model, n), w_in_l.dtype)
    c = lambda a, n: w_in_l[:, a:a + n]
    dn_small = dn0 + 2 * DN_QK + 2 * DN_V
    rw_l = rw0 + 3 * RW_WIDTH
    parts = [
        c(0, gate_cols),
        c(dn0, 2 * DN_QK + DN_V),
        c(dn0 + 2 * DN_QK + DN_V, DN_V),
        c(rw0, 3 * RW_WIDTH),
        c(rw_l, 2 * RW_DECAY_LORA), z(64),
        c(rw_l + 2 * RW_DECAY_LORA, 2 * RW_A_LORA), z(64),
        c(rw_l + 2 * RW_DECAY_LORA + 2 * RW_A_LORA, RW_GATE_LORA), z(256),
        c(ss0, SSM_INNER),
        c(ss0 + SSM_INNER, SSM_XBC),
        c(dn_small, 4 * DN_HEADS), c(ss0 + SSM_INNER + SSM_XBC, 2 * SSM_HEADS), z(512 - 64),
    ]
    return jnp.concatenate(parts, axis=1).astype(BF16)


CONV_PAD = 72


def _conv_rows(seg, ctx_len):
    ctx0 = CONV_PAD
    lat0 = ctx0 + ctx_len + CONV_PAD
    total = lat0 + (seg - ctx_len) + CONV_PAD
    return ctx0, lat0, total


def _fill_padded(scr, x, seg, ctx_len):
    c = x.shape[1]
    n_lat = seg - ctx_len
    ctx0, lat0, total = _conv_rows(seg, ctx_len)
    xc = x[:ctx_len]
    xl = x[ctx_len:]
    col = lax.broadcasted_iota(jnp.int32, (n_lat, c), 0) % GRID_W
    zpad = jnp.zeros((CONV_PAD, c), F32)
    for i, lat in enumerate((xl, jnp.where(col == GRID_W - 1, 0.0, xl), jnp.where(col == 0, 0.0, xl))):
        scr[i, pl.ds(0, CONV_PAD), :] = zpad
        scr[i, pl.ds(ctx0, ctx_len), :] = xc
        scr[i, pl.ds(ctx0 + ctx_len, CONV_PAD), :] = zpad
        scr[i, pl.ds(lat0, n_lat), :] = lat
        scr[i, pl.ds(lat0 + n_lat, CONV_PAD), :] = zpad


def _dwconv_block(scr, w9, seg, ctx_len):
    n_lat = seg - ctx_len
    ctx0, lat0, _ = _conv_rows(seg, ctx_len)
    acc_c = None
    for dc in (-1, 0, 1):
        k = 3 + (dc + 1)
        term = scr[0, pl.ds(ctx0 + dc, ctx_len), :] * w9[k:k + 1]
        acc_c = term if acc_c is None else acc_c + term
    cols = [scr[{-1: 1, 0: 0, 1: 2}[dc], pl.ds(lat0 + dc, n_lat), :] for dc in (-1, 0, 1)]
    rows = []
    for dr in (-1, 0, 1):
        k = (dr + 1) * 3
        rows.append(cols[0] * w9[k:k + 1] + cols[1] * w9[k + 1:k + 2] + cols[2] * w9[k + 2:k + 3])
    zeros = jnp.zeros((GRID_W, cols[0].shape[1]), F32)
    acc_l = (rows[1] + jnp.concatenate([zeros, rows[0][:n_lat - GRID_W]], axis=0)
             + jnp.concatenate([rows[2][GRID_W:], zeros], axis=0))
    return acc_c, acc_l


def _conv_kernel(x_ref, w_ref, b_ref, o_ref, scr, *, seg, ctx_len, n_norm_blocks):
    _fill_padded(scr, x_ref[...].astype(F32), seg, ctx_len)
    do_norm = pl.program_id(1) < n_norm_blocks
    for part, rows in zip(_dwconv_block(scr, w_ref[...], seg, ctx_len),
                          (pl.ds(0, ctx_len), pl.ds(ctx_len, seg - ctx_len))):
        y = _silu(part + b_ref[...])
        if n_norm_blocks:
            yn = y * lax.rsqrt(jnp.sum(y * y, axis=-1, keepdims=True) + L2_EPS)
            y = jnp.where(do_norm, yn, y)
        o_ref[rows, :] = y.astype(o_ref.dtype)


def _conv_silu(p, conv_w, conv_b, *, n_batch, seg, ctx_len, col0, n_norm_blocks, name):
    ntok = p.shape[0]
    cw = conv_w.shape[-1]
    kern = functools.partial(_conv_kernel, seg=seg, ctx_len=ctx_len, n_norm_blocks=n_norm_blocks)
    return pl.pallas_call(
        kern,
        grid=(n_batch, cw // LANES),
        in_specs=[pl.BlockSpec((seg, LANES), lambda b, j: (b, col0 // LANES + j)),
                  pl.BlockSpec((9, LANES), lambda b, j: (0, j)),
                  pl.BlockSpec((1, LANES), lambda b, j: (0, j))],
        out_specs=pl.BlockSpec((seg, LANES), lambda b, j: (b, j)),
        out_shape=jax.ShapeDtypeStruct((ntok, cw), BF16),
        scratch_shapes=[pltpu.VMEM((3, _conv_rows(seg, ctx_len)[2], LANES), F32)],
        compiler_params=_cparams(("arbitrary", "arbitrary")),
        name=name,
    )(p, conv_w.reshape(9, cw), conv_b.reshape(1, cw))


def _dn_conv(p, conv_w, *, n_batch, seg, ctx_len, col0):
    return _conv_silu(p, conv_w, jnp.zeros((conv_w.shape[-1],), F32), n_batch=n_batch, seg=seg,
                      ctx_len=ctx_len, col0=col0, n_norm_blocks=2 * DN_QK // LANES, name="dn_conv")


SO_BETA, SO_G, SO_DT, SO_A = 0, 16, 32, 64


def _small_kernel(x_ref, par_ref, o_ref):
    x = x_ref[...]
    par = par_ref[...]
    neg_a = -jnp.exp(par[1:2])
    tm = x.shape[0]
    beta = _sigmoid(x[:, SM_BETA:SM_BETA + 16])
    sp = _softplus(x[:, SM_A:SM_A + 48] + par[0:1, SM_A:SM_A + 48])
    g = sp[:, 0:16] * neg_a[:, SM_A:SM_A + 16]
    dt = sp[:, 16:48]
    a = dt * neg_a[:, SM_DT:SM_DT + 32]
    o_ref[...] = jnp.concatenate([beta, g, dt, a, jnp.zeros((tm, LANES - 96), F32)], axis=1)


def _small_prep(ps, dn_a_log, dn_dt_bias, ssm_a_log, ssm_dt_bias):
    ntok = ps.shape[0]
    tm = _tile(ntok, 1024)
    zeros16 = jnp.zeros((16,), F32)
    bias = jnp.concatenate([zeros16, dn_dt_bias.reshape(-1), ssm_dt_bias.reshape(-1),
                            jnp.zeros((LANES - 64,), F32)])
    a_log = jnp.concatenate([zeros16, dn_a_log.reshape(-1), ssm_a_log.reshape(-1),
                             jnp.zeros((LANES - 64,), F32)])
    par = jnp.concatenate([bias[None], a_log[None], jnp.zeros((6, LANES), F32)], axis=0)
    return pl.pallas_call(
        _small_kernel,
        grid=(ntok // tm,),
        in_specs=[pl.BlockSpec((tm, LANES), lambda i: (i, 0)),
                  pl.BlockSpec((8, LANES), lambda i: (0, 0))],
        out_specs=pl.BlockSpec((tm, LANES), lambda i: (i, 0)),
        out_shape=jax.ShapeDtypeStruct((ntok, LANES), F32),
        compiler_params=_cparams(("arbitrary",)),
        name="small_prep",
    )(ps, par)


def _chunk_of_step(d, s, n_ctx, n_all):
    bwd = jnp.where(s < n_ctx, n_ctx - 1 - s, n_all + n_ctx - 1 - s)
    return jnp.where(d == 0, s, bwd)


def _order_masks(d, size):
    ri = lax.broadcasted_iota(jnp.int32, (size, size), 0)
    ci = lax.broadcasted_iota(jnp.int32, (size, size), 1)
    delta = (ri - ci) * (1 - 2 * d)
    return delta >= 0, delta > 0, ri == ci


def _col_to_row(col, eye):
    return jnp.sum(jnp.where(eye, col, 0.0), axis=0, keepdims=True)


def _dn_scan_kernel(qf_ref, kf_ref, vf_ref, smf_ref, qb_ref, kb_ref, vb_ref, smb_ref,
                    of_ref, ob_ref, s_scr):
    c = DN_CHUNK
    h_n = DN_HEADS

    @pl.when(pl.program_id(1) == 0)
    def _():
        s_scr[...] = jnp.zeros(s_scr.shape, F32)

    scale = DN_DK ** -0.5
    n_s = qf_ref.shape[1]
    refs = ((qf_ref, kf_ref, vf_ref, smf_ref, of_ref), (qb_ref, kb_ref, vb_ref, smb_ref, ob_ref))
    masks = [_order_masks(d, c) for d in range(2)]
    gcs = {}
    gtots = {}
    betas = {}
    for s in range(n_s):
        for d in range(2):
            sm = refs[d][3][0, s]
            betas[s, d] = sm[:, SO_BETA + d * h_n:SO_BETA + (d + 1) * h_n]
            g = sm[:, SO_G + d * h_n:SO_G + (d + 1) * h_n]
            gcs[s, d] = _dot_hilo_lhs(masks[d][0].astype(BF16), g)
            gtots[s, d] = jnp.sum(g, axis=0, keepdims=True)

    items = [(s, d, h) for s in range(n_s) for d in range(2) for h in range(h_n)]
    n = range(len(items))
    lanes = [slice(h * DN_DK, (h + 1) * DN_DK) for _, _, h in items]
    incl = [masks[d][0] for _, d, _ in items]
    strict = [masks[d][1] for _, d, _ in items]
    eye = masks[0][2]
    qs = [refs[d][0][0, s, :, lanes[i]].astype(F32) * scale for i, (s, d, _) in enumerate(items)]
    ks = [refs[d][1][0, s, :, lanes[i]].astype(F32) for i, (s, d, _) in enumerate(items)]
    vs = [refs[d][2][0, s, :, lanes[i]].astype(F32) for i, (s, d, _) in enumerate(items)]
    bcols = [betas[s, d][:, h:h + 1] for s, d, h in items]
    gcols = [gcs[s, d][:, h:h + 1] for s, d, h in items]
    gts = [gtots[s, d][:, h:h + 1] for s, d, h in items]
    decs = [jnp.exp(jnp.where(incl[i], gcols[i] - _col_to_row(gcols[i], eye), 0.0)) for i in n]
    kbs = [ks[i] * bcols[i] for i in n]
    a_s = [_dot_nt(jnp.concatenate([kbs[i], qs[i]], axis=0), ks[i]) for i in n]
    atts = [jnp.where(incl[i], a_s[i][c:] * decs[i], 0.0) for i in n]
    tinvs = _inv_unit_lower([-jnp.where(strict[i], a_s[i][:c] * decs[i], 0.0) for i in n], c)
    egs = [jnp.exp(gcols[i]) for i in n]
    uws = [_dot(tinvs[i], jnp.concatenate([vs[i] * bcols[i], kbs[i] * egs[i]], axis=1))
           for i in n]
    s_olds = [s_scr[s, d, h] for s, d, h in items]
    wss = [_dot(jnp.concatenate([uws[i][:, DN_DK:], qs[i] * egs[i]], axis=0), s_olds[i])
           for i in n]
    v_news = [uws[i][:, :DN_DK] - wss[i][:c] for i in n]
    outs = [wss[i][c:] + _dot(atts[i], v_news[i]) for i in n]
    upds = [_dot_tn(ks[i] * jnp.exp(gts[i] - gcols[i]), v_news[i]) for i in n]
    for i, (s, d, h) in enumerate(items):
        s_scr[s, d, h] = s_olds[i] * jnp.exp(gts[i]) + upds[i]
        refs[d][4][0, s, :, lanes[i]] = outs[i].astype(refs[d][4].dtype)


def _scan_specs(c, widths, n_all, n_ctx):
    def spec(d, width, col):
        return pl.BlockSpec(
            (c, width), lambda b, s: (b * n_all + _chunk_of_step(d, s, n_ctx, n_all), col))
    return [spec(d, w, col) for d in range(2) for w, col in widths]


def _dn_scan(qkv, small, *, n_batch, seg, ctx_len):
    ntok = qkv.shape[0]
    c = DN_CHUNK
    n_all = seg // c
    n_ctx = ctx_len // c
    n_s = DN_SAMPLES_PER_STEP if n_batch % DN_SAMPLES_PER_STEP == 0 else 1
    groups = n_batch // n_s

    def spec(d, width, col):
        return pl.BlockSpec((1, n_s, c, width),
                            lambda b, s: (b, 0, _chunk_of_step(d, s, n_ctx, n_all), col))

    ins = [spec(d, w, col) for d in range(2)
           for w, col in ((DN_QK, 0), (DN_QK, 1), (DN_V, 2), (LANES, 0))]
    qkv4 = qkv.reshape(groups, n_s, seg, qkv.shape[1])
    small4 = small.reshape(groups, n_s, seg, LANES)
    o_f, o_b = pl.pallas_call(
        _dn_scan_kernel,
        grid=(groups, n_all),
        in_specs=ins,
        out_specs=[spec(0, DN_V, 0), spec(1, DN_V, 0)],
        out_shape=[jax.ShapeDtypeStruct((groups, n_s, seg, DN_V), SCAN_OUT_DTYPE)] * 2,
        scratch_shapes=[pltpu.VMEM((n_s, 2, DN_HEADS, DN_DK, DN_DK), F32)],
        compiler_params=_cparams(("arbitrary", "arbitrary")),
        name="dn_scan",
    )(qkv4, qkv4, qkv4, small4, qkv4, qkv4, qkv4, small4)
    return o_f.reshape(ntok, DN_V), o_b.reshape(ntok, DN_V)


def _dn_readout_kernel(of_ref, ob_ref, gate_ref, g_ref, y_ref):
    o = of_ref[...].astype(F32) + ob_ref[...].astype(F32)
    tm = o.shape[0]
    gate = gate_ref[...].astype(F32)
    outs = []
    for h in range(DN_HEADS):
        lanes = slice(h * DN_DK, (h + 1) * DN_DK)
        oh = o[:, lanes]
        yh = oh * lax.rsqrt(jnp.mean(oh * oh, axis=-1, keepdims=True) + NORM_EPS) * g_ref[...]
        outs.append(yh * _silu(gate[:, lanes]))
    y_ref[...] = jnp.concatenate(outs, axis=1).astype(y_ref.dtype)


def _dn_readout(o_f, o_b, p, norm_g, *, col_gate):
    ntok = p.shape[0]
    tm = _tile(ntok, 512)
    tok = pl.BlockSpec((tm, DN_V), lambda i: (i, 0))
    return pl.pallas_call(
        _dn_readout_kernel,
        grid=(ntok // tm,),
        in_specs=[tok, tok,
                  pl.BlockSpec((tm, DN_V), lambda i: (i, col_gate // DN_V)),
                  pl.BlockSpec((1, DN_DK), lambda i: (0, 0))],
        out_specs=tok,
        out_shape=jax.ShapeDtypeStruct((ntok, DN_V), BF16),
        compiler_params=_cparams(("arbitrary",)),
        name="dn_readout",
    )(o_f, o_b, p, norm_g.reshape(1, DN_DK))


def _head_expander(n_heads, head_dim):
    hi = lax.broadcasted_iota(jnp.int32, (n_heads, n_heads * head_dim), 0)
    li = lax.broadcasted_iota(jnp.int32, (n_heads, n_heads * head_dim), 1)
    lo = hi * head_dim
    return jnp.where((li >= lo) & (li < lo + head_dim), 1.0, 0.0).astype(BF16)


def _ssd_scan_kernel(xf_ref, bf_ref, cf_ref, smf_ref, xb_ref, bb_ref, cb_ref, smb_ref,
                     yf_ref, yb_ref, s_scr):
    c = SSM_CHUNK
    hn = SSM_HEADS
    hd = SSM_HEADDIM
    per_group = hn // SSM_GROUPS
    gw = per_group * hd

    @pl.when(pl.program_id(1) == 0)
    def _():
        s_scr[...] = jnp.zeros(s_scr.shape, F32)

    refs = ((xf_ref, bf_ref, cf_ref, smf_ref, yf_ref), (xb_ref, bb_ref, cb_ref, smb_ref, yb_ref))
    expand = _head_expander(hn, hd)
    lane_lo = lax.broadcasted_iota(jnp.int32, (c, 2 * hd), 1) < hd
    pre = []
    for d in range(2):
        incl, _, eye = _order_masks(d, c)
        sm = refs[d][3][...]
        dt = sm[:, SO_DT + d * hn:SO_DT + (d + 1) * hn]
        a = sm[:, SO_A + d * hn:SO_A + (d + 1) * hn]
        acum = _dot_hilo_lhs(incl.astype(BF16), a)
        acum_x = _dot_exact_rhs(acum, expand)
        atot_x = jnp.sum(_dot_exact_rhs(a, expand), axis=0, keepdims=True)
        xdt = refs[d][0][...].astype(F32) * _dot_exact_rhs(dt, expand)
        pre.append(dict(incl=incl, eye=eye, acum=acum, xdt=xdt,
                        xdec=xdt * jnp.exp(atot_x - acum_x),
                        eacum=jnp.exp(acum_x), etot=jnp.exp(atot_x)))

    dgs = [(d, g) for d in range(2) for g in range(SSM_GROUPS)]
    bgs = [refs[d][1][:, g * SSM_STATE:(g + 1) * SSM_STATE] for d, g in dgs]
    cgs = [refs[d][2][:, g * SSM_STATE:(g + 1) * SSM_STATE] for d, g in dgs]
    cbs = [_dot_nt(cgs[i], bgs[i]) for i in range(len(dgs))]
    ms = {}
    for d in range(2):
        p = pre[d]
        for h in range(hn):
            acol = p["acum"][:, h:h + 1]
            lmat = jnp.exp(jnp.where(p["incl"], acol - _col_to_row(acol, p["eye"]), 0.0))
            ms[d, h] = jnp.where(p["incl"], cbs[d * SSM_GROUPS + h // per_group] * lmat, 0.0)
    ydiag = {}
    for d in range(2):
        for pair in range(hn // 2):
            xp = pre[d]["xdt"][:, pair * 2 * hd:(pair + 1) * 2 * hd]
            ydiag[d, pair] = jnp.where(lane_lo, _dot(ms[d, 2 * pair], xp),
                                       _dot(ms[d, 2 * pair + 1], xp))
    s_olds = [s_scr[d, g] for d, g in dgs]
    yoffs = [_dot(cgs[i], s_olds[i]) for i in range(len(dgs))]
    sts = [_dot_tn(bgs[i], pre[d]["xdec"][:, g * gw:(g + 1) * gw])
           for i, (d, g) in enumerate(dgs)]
    for i, (d, g) in enumerate(dgs):
        lanes = slice(g * gw, (g + 1) * gw)
        s_scr[d, g] = s_olds[i] * pre[d]["etot"][:, lanes] + sts[i]
        yd = jnp.concatenate(
            [ydiag[d, pr] for pr in range(g * per_group // 2, (g + 1) * per_group // 2)], axis=1)
        refs[d][4][:, lanes] = (yd + yoffs[i] * pre[d]["eacum"][:, lanes]).astype(refs[d][4].dtype)


def _ssd_scan(xbc, small, *, n_batch, seg, ctx_len):
    ntok = xbc.shape[0]
    c = SSM_CHUNK
    n_all = seg // c
    n_ctx = ctx_len // c
    gn = SSM_GROUPS * SSM_STATE
    ins = _scan_specs(c, [(SSM_INNER, 0), (gn, SSM_INNER // gn), (gn, SSM_INNER // gn + 1),
                          (LANES, 0)], n_all, n_ctx)
    outs = _scan_specs(c, [(SSM_INNER, 0)], n_all, n_ctx)
    return pl.pallas_call(
        _ssd_scan_kernel,
        grid=(n_batch, n_all),
        in_specs=ins,
        out_specs=outs,
        out_shape=[jax.ShapeDtypeStruct((ntok, SSM_INNER), SCAN_OUT_DTYPE)] * 2,
        scratch_shapes=[pltpu.VMEM((2, SSM_GROUPS, SSM_STATE, SSM_INNER // SSM_GROUPS), F32)],
        compiler_params=_cparams(("arbitrary", "arbitrary")),
        name="ssd_scan",
    )(xbc, xbc, xbc, small, xbc, xbc, xbc, small)


def _ssd_readout_kernel(yf_ref, yb_ref, xs_ref, z_ref, d_ref, g_ref, o_ref):
    y = ((yf_ref[...].astype(F32) + yb_ref[...].astype(F32)
          + d_ref[...] * xs_ref[...].astype(F32)) * _silu(z_ref[...].astype(F32)))
    gw = SSM_INNER // SSM_GROUPS
    outs = []
    for g in range(SSM_GROUPS):
        lanes = slice(g * gw, (g + 1) * gw)
        yg = y[:, lanes]
        outs.append(yg * lax.rsqrt(jnp.mean(yg * yg, axis=-1, keepdims=True) + SSM_NORM_EPS)
                    * g_ref[:, lanes])
    o_ref[...] = jnp.concatenate(outs, axis=1).astype(o_ref.dtype)


def _ssd_readout(y_f, y_b, xbc, p, d_skip, norm_g, *, col_z):
    ntok = p.shape[0]
    tm = _tile(ntok, 512)
    w = SSM_INNER
    return pl.pallas_call(
        _ssd_readout_kernel,
        grid=(ntok // tm,),
        in_specs=[pl.BlockSpec((tm, w), lambda i: (i, 0)),
                  pl.BlockSpec((tm, w), lambda i: (i, 0)),
                  pl.BlockSpec((tm, w), lambda i: (i, 0)),
                  pl.BlockSpec((tm, w), lambda i: (i, col_z // w)),
                  pl.BlockSpec((1, w), lambda i: (0, 0)),
                  pl.BlockSpec((1, w), lambda i: (0, 0))],
        out_specs=pl.BlockSpec((tm, w), lambda i: (i, 0)),
        out_shape=jax.ShapeDtypeStruct((ntok, w), BF16),
        compiler_params=_cparams(("arbitrary",)),
        name="ssd_readout",
    )(y_f, y_b, xbc, p, jnp.repeat(d_skip, SSM_HEADDIM).reshape(1, w), norm_g.reshape(1, w))


def _rw_shift_kernel(x_ref, mu_ref, o_ref, scr, *, seg, ctx_len):
    x = x_ref[...].astype(F32)
    n_lat = seg - ctx_len
    _fill_padded(scr, x, seg, ctx_len)
    ctx0, lat0, _ = _conv_rows(seg, ctx_len)
    mu = mu_ref[...]
    lane_c = lax.broadcasted_iota(jnp.int32, (ctx_len, LANES), 1)
    prev = scr[0, pl.ds(ctx0 - 1, ctx_len), :]
    nxt = scr[0, pl.ds(ctx0 + 1, ctx_len), :]
    xc = x[:ctx_len]
    o_ref[pl.ds(0, ctx_len), :] = (xc + (jnp.where(lane_c % 2 == 0, prev, nxt) - xc) * mu
                                   ).astype(o_ref.dtype)
    sel = lax.broadcasted_iota(jnp.int32, (n_lat, LANES), 1) % 4
    left = scr[1, pl.ds(lat0 - 1, n_lat), :]
    right = scr[2, pl.ds(lat0 + 1, n_lat), :]
    up = scr[0, pl.ds(lat0 - GRID_W, n_lat), :]
    down = scr[0, pl.ds(lat0 + GRID_W, n_lat), :]
    shifted = jnp.where(sel == 0, left, jnp.where(sel == 1, right, jnp.where(sel == 2, up, down)))
    xl = x[ctx_len:]
    o_ref[pl.ds(ctx_len, n_lat), :] = (xl + (shifted - xl) * mu).astype(o_ref.dtype)


def _rw_shift(p, mu_packed, *, n_batch, seg, ctx_len, col0):
    ntok = p.shape[0]
    cw = mu_packed.shape[-1]
    kern = functools.partial(_rw_shift_kernel, seg=seg, ctx_len=ctx_len)
    return pl.pallas_call(
        kern,
        grid=(n_batch, cw // LANES),
        in_specs=[pl.BlockSpec((seg, LANES), lambda b, j: (b, col0 // LANES + j)),
                  pl.BlockSpec((1, LANES), lambda b, j: (0, j))],
        out_specs=pl.BlockSpec((seg, LANES), lambda b, j: (b, j)),
        out_shape=jax.ShapeDtypeStruct((ntok, cw), BF16),
        scratch_shapes=[pltpu.VMEM((3, _conv_rows(seg, ctx_len)[2], LANES), F32)],
        compiler_params=_cparams(("arbitrary", "arbitrary")),
        name="rw_shift",
    )(p, mu_packed.reshape(1, cw))


def _head_sum(x, n_heads, head_dim):
    expand = _head_expander(n_heads, head_dim)
    hi = lax.broadcasted_iota(jnp.int32, (n_heads * head_dim, n_heads), 1)
    li = lax.broadcasted_iota(jnp.int32, (n_heads * head_dim, n_heads), 0)
    lo = hi * head_dim
    reduce = jnp.where((li >= lo) & (li < lo + head_dim), 1.0, 0.0).astype(BF16)
    return _dot(_dot(x, reduce), expand)


def _rw_prep_kernel(x_ref, w0_ref, w2_ref, a0_ref, a2_ref, g2_ref, kk_ref, ka_ref, rk_ref,
                    r_ref, v_ref, nk_ref, gate_ref, bonus_ref, lw_ref, kd_ref, b_ref):
    w = RW_WIDTH
    r = x_ref[:, 0:w].astype(F32)
    k = x_ref[:, w:2 * w].astype(F32)
    v = x_ref[:, 2 * w:3 * w].astype(F32)
    lora = 3 * w
    wl = x_ref[:, lora + RWL_W:lora + RWL_W + 256].astype(F32)
    al = x_ref[:, lora + RWL_A:lora + RWL_A + 256]
    gl = x_ref[:, lora + RWL_G:lora + RWL_G + RW_GATE_LORA].astype(F32)
    w_raw = w0_ref[...] + _dot(jnp.tanh(wl), w2_ref[...])
    log_decay = -math.exp(-0.5) * _sigmoid(w_raw)
    icl = _sigmoid(a0_ref[...] + _dot(al, a2_ref[...]))
    gate_ref[...] = _dot(_sigmoid(gl), g2_ref[...]).astype(gate_ref.dtype)
    kx = k * kk_ref[...]
    nk = kx * lax.rsqrt(_head_sum(kx * kx, RW_HEADS, RW_HEAD) + L2_EPS)
    r_ref[...] = r.astype(r_ref.dtype)
    v_ref[...] = v.astype(v_ref.dtype)
    nk_ref[...] = nk.astype(nk_ref.dtype)
    ksum = None
    for d in range(2):
        icl_d = icl[:, d * w:(d + 1) * w]
        k_d = k * (1.0 + (icl_d - 1.0) * ka_ref[...])
        lw_ref[d] = log_decay[:, d * w:(d + 1) * w]
        kd_ref[d] = k_d.astype(kd_ref.dtype)
        b_ref[d] = (nk * icl_d).astype(b_ref.dtype)
        ksum = k_d if ksum is None else ksum + k_d
    bonus_ref[...] = (_head_sum(r * ksum * rk_ref[...], RW_HEADS, RW_HEAD) * v
                      ).astype(bonus_ref.dtype)


def _pack_rw_vec(vec):
    w3 = 3 * RW_WIDTH
    z = lambda n: jnp.zeros((n,), vec.dtype)
    l0 = w3 + 2 * RW_DECAY_LORA
    l1 = l0 + 2 * RW_A_LORA
    return jnp.concatenate([vec[:w3], vec[w3:l0], z(64), vec[l0:l1], z(64), vec[l1:], z(256)])


def _pack_lora(m2):
    lora, w = m2.shape[1], m2.shape[2]
    out = jnp.zeros((256, 2 * w), F32)
    out = out.at[0:lora, 0:w].set(m2[0])
    out = out.at[lora:2 * lora, w:2 * w].set(m2[1])
    return out.astype(BF16)


def _rw_prep(xs, w0, w2, a0, a2, g2, k_k, k_a, r_k):
    ntok = xs.shape[0]
    w = RW_WIDTH
    tm = _tile(ntok, 256)
    row = lambda a: a.reshape(1, -1)
    full = lambda shape: pl.BlockSpec(shape, lambda i: (0,) * len(shape))
    tok = pl.BlockSpec((tm, w), lambda i: (i, 0))
    tok2 = pl.BlockSpec((2, tm, w), lambda i: (0, i, 0))
    sd = jax.ShapeDtypeStruct
    return pl.pallas_call(
        _rw_prep_kernel,
        grid=(ntok // tm,),
        in_specs=[pl.BlockSpec((tm, 4 * w), lambda i: (i, 0)),
                  full((1, 2 * w)), full((256, 2 * w)), full((1, 2 * w)), full((256, 2 * w)),
                  full((RW_GATE_LORA, w)), full((1, w)), full((1, w)), full((1, w))],
        out_specs=[tok, tok, tok, tok, tok, tok2, tok2, tok2],
        out_shape=[sd((ntok, w), BF16)] * 5
        + [sd((2, ntok, w), F32), sd((2, ntok, w), BF16), sd((2, ntok, w), BF16)],
        compiler_params=_cparams(("arbitrary",)),
        name="rw_prep",
    )(xs, row(w0), _pack_lora(w2), row(a0), _pack_lora(a2), g2.astype(BF16), row(k_k), row(k_a),
      row(r_k))


def _stack_heads(x, lane_lo):
    return jnp.concatenate([jnp.where(lane_lo, x, 0.0), jnp.where(lane_lo, 0.0, x)], axis=0)


def _rw_scan_kernel(rf_ref, vf_ref, nkf_ref, lwf_ref, kdf_ref, bf_ref,
                    rb_ref, vb_ref, nkb_ref, lwb_ref, kdb_ref, bb_ref, yf_ref, yb_ref, h_scr):
    c = RW_CHUNK
    pw = 2 * RW_HEAD
    n_pairs = RW_HEADS // 2

    @pl.when(pl.program_id(1) == 0)
    def _():
        h_scr[...] = jnp.zeros(h_scr.shape, F32)

    refs = ((rf_ref, vf_ref, nkf_ref, lwf_ref, kdf_ref, bf_ref, yf_ref),
            (rb_ref, vb_ref, nkb_ref, lwb_ref, kdb_ref, bb_ref, yb_ref))
    rj = lax.broadcasted_iota(jnp.int32, (pw, pw), 0)
    cj = lax.broadcasted_iota(jnp.int32, (pw, pw), 1)
    eye_p = rj == cj
    lane_lo = lax.broadcasted_iota(jnp.int32, (c, pw), 1) < RW_HEAD
    ri = lax.broadcasted_iota(jnp.int32, (2 * c, 2 * c), 0) % c
    ci = lax.broadcasted_iota(jnp.int32, (2 * c, 2 * c), 1) % c

    per_dir = []
    for d in range(2):
        r_ref, v_ref, nk_ref, lw_ref, kd_ref, b_ref, _ = refs[d]
        incl, _, _ = _order_masks(d, c)
        lw = lw_ref[0]
        cw = _dot_hilo_lhs(incl.astype(BF16), lw)
        tot = jnp.sum(lw, axis=0, keepdims=True)
        e_neg = jnp.exp(-cw)
        e_rem = jnp.exp(tot - cw)
        b = b_ref[0].astype(F32)
        kd = kd_ref[0].astype(F32)
        delta = (ri - ci) * (1 - 2 * d)
        incl2 = delta >= 0
        per_dir.append(dict(
            a_t=-nk_ref[...].astype(F32) * jnp.exp(cw - lw), r_t=r_ref[...].astype(F32) * jnp.exp(cw),
            b_t=b * e_neg, k_t=kd * e_neg, b_h=b * e_rem, k_h=kd * e_rem,
            v=v_ref[...].astype(F32), e_tot=jnp.exp(tot), strict2=delta > 0,
            mask_y=jnp.concatenate([incl2, incl2], axis=1)))

    items = [(d, p) for d in range(2) for p in range(n_pairs)]
    n = range(len(items))
    ln = [slice(p * pw, (p + 1) * pw) for _, p in items]
    dd = [per_dir[d] for d, _ in items]

    def stacked(i, top, bottom):
        return jnp.concatenate([_stack_heads(dd[i][top][:, ln[i]], lane_lo),
                                _stack_heads(dd[i][bottom][:, ln[i]], lane_lo)], axis=0)

    xs = [stacked(i, "a_t", "r_t") for i in n]
    ys = [stacked(i, "b_t", "k_t") for i in n]
    vst = [_stack_heads(dd[i]["v"][:, ln[i]], lane_lo) for i in n]
    gs = [_dot_nt(xs[i], ys[i]) for i in n]
    tinvs = _inv_unit_lower(
        [jnp.where(dd[i]["strict2"], gs[i][:2 * c, :2 * c], 0.0) for i in n], c)
    h_olds = [h_scr[d, p] for d, p in items]
    xh = [_dot(xs[i], h_olds[i]) for i in n]
    rhs = [xh[i][:2 * c] + _dot(jnp.where(dd[i]["strict2"], gs[i][:2 * c, 2 * c:], 0.0), vst[i])
           for i in n]
    us = [_dot(tinvs[i], rhs[i]) for i in n]
    uv = [jnp.concatenate([us[i], vst[i]], axis=0) for i in n]
    yst = [xh[i][2 * c:] + _dot(jnp.where(dd[i]["mask_y"], gs[i][2 * c:, :], 0.0), uv[i])
           for i in n]
    upd = [_dot_tn(stacked(i, "b_h", "k_h"), uv[i]) for i in n]
    for i, (d, p) in enumerate(items):
        e_col = jnp.sum(jnp.where(eye_p, dd[i]["e_tot"][:, ln[i]], 0.0), axis=1, keepdims=True)
        h_scr[d, p] = h_olds[i] * e_col + upd[i]
        refs[d][6][:, ln[i]] = (yst[i][:c] + yst[i][c:]).astype(refs[d][6].dtype)


def _rw_scan(r, v, nk, lw, kd, b, *, n_batch, seg, ctx_len):
    ntok = r.shape[0]
    c = RW_CHUNK
    w = RW_WIDTH
    n_all = seg // c
    n_ctx = ctx_len // c
    def row(d):
        return lambda bb, s: bb * n_all + _chunk_of_step(d, s, n_ctx, n_all)

    def tok(d):
        return pl.BlockSpec((c, w), lambda bb, s: (row(d)(bb, s), 0))

    def tok2(d):
        return pl.BlockSpec((1, c, w), lambda bb, s: (d, row(d)(bb, s), 0))

    return pl.pallas_call(
        _rw_scan_kernel,
        grid=(n_batch, n_all),
        in_specs=[spec(d) for d in range(2) for spec in (tok, tok, tok, tok2, tok2, tok2)],
        out_specs=[tok(0), tok(1)],
        out_shape=[jax.ShapeDtypeStruct((ntok, w), SCAN_OUT_DTYPE)] * 2,
        scratch_shapes=[pltpu.VMEM((2, RW_HEADS // 2, 2 * RW_HEAD, 2 * RW_HEAD), F32)],
        compiler_params=_cparams(("arbitrary", "arbitrary")),
        name="rw_scan",
    )(r, v, nk, lw, kd, b, r, v, nk, lw, kd, b)


def _rw_readout_kernel(yf_ref, yb_ref, bonus_ref, gate_ref, g_ref, b_ref, o_ref):
    inv_n = 1.0 / RW_HEAD
    tm = o_ref.shape[0]
    halves = [slice(0, tm // 2), slice(tm // 2, tm)]
    ys = [yf_ref[h, :].astype(F32) + yb_ref[h, :].astype(F32) for h in halves]
    means = [_head_sum(y, RW_HEADS, RW_HEAD) * inv_n for y in ys]
    ycs = [y - m for y, m in zip(ys, means)]
    vs = [_head_sum(yc * yc, RW_HEADS, RW_HEAD) * inv_n for yc in ycs]
    for h, yc, var in zip(halves, ycs, vs):
        yn = yc * lax.rsqrt(var + RW_GN_EPS) * g_ref[...] + b_ref[...]
        o_ref[h, :] = ((yn + bonus_ref[h, :].astype(F32)) * gate_ref[h, :].astype(F32)
                       ).astype(o_ref.dtype)


def _rw_readout(y_f, y_b, bonus, gate, ln_g, ln_b):
    ntok = bonus.shape[0]
    w = RW_WIDTH
    tm = _tile(ntok, 512)
    tok = pl.BlockSpec((tm, w), lambda i: (i, 0))
    vec = pl.BlockSpec((1, w), lambda i: (0, 0))
    return pl.pallas_call(
        _rw_readout_kernel,
        grid=(ntok // tm,),
        in_specs=[tok, tok, tok, tok, vec, vec],
        out_specs=tok,
        out_shape=jax.ShapeDtypeStruct((ntok, w), BF16),
        compiler_params=_cparams(("arbitrary",)),
        name="rw_readout",
    )(y_f, y_b, bonus, gate, ln_g.reshape(1, w), ln_b.reshape(1, w))


def _branch_mix_kernel(oa_ref, ob_ref, oc_ref, ga_ref, gb_ref, gc_ref, w_ref, y_ref):
    acc = None
    for i, (o_ref, g_ref) in enumerate(((oa_ref, ga_ref), (ob_ref, gb_ref), (oc_ref, gc_ref))):
        term = (_sigmoid(g_ref[...].astype(F32))
                * jnp.dot(o_ref[...], w_ref[i], preferred_element_type=F32))
        acc = term if acc is None else acc + term
    y_ref[...] = acc.astype(y_ref.dtype)


def _branch_mix(oa, ob, oc, p, w_branch_bf16, *, d_model):
    ntok = p.shape[0]
    bw = oa.shape[1]
    tm = _tile(ntok, 512)
    tn = 512
    nb = d_model // tn
    o_spec = pl.BlockSpec((tm, bw), lambda i, n: (i, 0))
    g_spec = lambda br: pl.BlockSpec((tm, tn), lambda i, n: (i, br * nb + n))
    return pl.pallas_call(
        _branch_mix_kernel,
        grid=(ntok // tm, nb),
        in_specs=[o_spec, o_spec, o_spec, g_spec(0), g_spec(1), g_spec(2),
                  pl.BlockSpec((3, bw, tn), lambda i, n: (0, 0, n))],
        out_specs=pl.BlockSpec((tm, tn), lambda i, n: (i, n)),
        out_shape=jax.ShapeDtypeStruct((ntok, d_model), BF16),
        compiler_params=_cparams(("arbitrary", "arbitrary")),
        name="branch_mix",
    )(oa, ob, oc, p, p, p, w_branch_bf16)


ROUTE_TILE = 256


def _dot3_nt(a, b):
    ah, al = _split2(a)
    bh, bl = _split2(b)
    d = functools.partial(lax.dot_general, dimension_numbers=(((1,), (1,)), ((), ())),
                          preferred_element_type=F32)
    return d(ah, bh) + d(ah, bl) + d(al, bh)


def _outproj_router_kernel(y_ref, x_ref, ml_ref, mc_ref, wo_ref, g_ref, wr_ref, br_ref,
                           xo_ref, f_ref, idx_ref, wgt_ref, rank_ref, cnt_ref, run_scr,
                           *, tm, seg, ctx_len):
    i = pl.program_id(0)

    @pl.when(i == 0)
    def _():
        run_scr[...] = jnp.zeros(run_scr.shape, F32)

    is_ctx = _ctx_rows(i, tm, seg, ctx_len)
    ml = ml_ref[0]
    mc = mc_ref[0]
    gate = jnp.where(is_ctx, mc[2:3], ml[2:3])
    halves = [slice(0, tm // 2), slice(tm // 2, tm)]
    x_news = [x_ref[h, :] + gate[h] * jnp.dot(y_ref[h, :], wo_ref[...], preferred_element_type=F32)
              for h in halves]
    fs = [_modulated_norm(x_news[j], g_ref[...], ml, mc, is_ctx[h], 3, 4)
          for j, h in enumerate(halves)]
    dh = x_ref.shape[1] // 2
    for j, h in enumerate(halves):
        xo_ref[h, :] = x_news[j]
        f_ref[h, :] = _pack_bf16_pair(fs[j][:, :dh], fs[j][:, dh:])

    ne = N_EXPERTS
    logits = jnp.concatenate([_dot3_nt(wr_ref[...], f) for f in fs], axis=1) + br_ref[...]
    eidx = lax.broadcasted_iota(jnp.int32, (ne, tm), 0)
    vals = logits
    picks = []
    tops = []
    hots = []
    for _ in range(TOP_K):
        m = jnp.max(vals, axis=0, keepdims=True)
        pick = jnp.min(jnp.where(vals == m, eidx, ne), axis=0, keepdims=True)
        hot = eidx == pick
        vals = jnp.where(hot, -jnp.inf, vals)
        tops.append(m)
        picks.append(pick)
        hots.append(hot)
    es = [jnp.exp(t - tops[0]) for t in tops]
    denom = es[0] + es[1] + es[2] + es[3]
    hot_all = jnp.where(hots[0] | hots[1] | hots[2] | hots[3], 1.0, 0.0)
    si = lax.broadcasted_iota(jnp.int32, (tm, tm), 0)
    ti = lax.broadcasted_iota(jnp.int32, (tm, tm), 1)
    earlier = jnp.where(si < ti, 1.0, 0.0).astype(BF16)
    before = jnp.dot(hot_all.astype(BF16), earlier, preferred_element_type=F32)
    rank = run_scr[:, 0:1] + before
    run_new = run_scr[...] + jnp.sum(hot_all, axis=1, keepdims=True)
    run_scr[...] = run_new
    cnt_ref[...] = run_new
    idx_ref[...] = jnp.concatenate(picks, axis=0)
    wgt_ref[...] = jnp.concatenate([e / denom for e in es], axis=0)
    rank_ref[...] = jnp.concatenate(
        [jnp.sum(jnp.where(h, rank, 0.0), axis=0, keepdims=True) for h in hots],
        axis=0).astype(jnp.int32)


def _outproj_router(y, x, mod, w_out_bf16, g_ffn, w_router, b_router, *, n_batch, seg, ctx_len):
    ntok, d = x.shape
    tm = _tile(seg, ROUTE_TILE, LANES)
    per = seg // tm
    ne = N_EXPERTS
    kern = functools.partial(_outproj_router_kernel, tm=tm, seg=seg, ctx_len=ctx_len)
    tok = pl.BlockSpec((tm, d), lambda i: (i, 0))
    sel = pl.BlockSpec((TOP_K, tm), lambda i: (0, i))
    sd = jax.ShapeDtypeStruct
    return pl.pallas_call(
        kern,
        grid=(ntok // tm,),
        in_specs=[tok, tok,
                  pl.BlockSpec((1, 6, d), lambda i: (i // per, 0, 0)),
                  pl.BlockSpec((1, 6, d), lambda i: (n_batch, 0, 0)),
                  pl.BlockSpec((d, d), lambda i: (0, 0)),
                  pl.BlockSpec((1, d), lambda i: (0, 0)),
                  pl.BlockSpec((ne, d), lambda i: (0, 0)),
                  pl.BlockSpec((ne, 1), lambda i: (0, 0))],
        out_specs=[tok, pl.BlockSpec((tm, d // 2), lambda i: (i, 0)), sel, sel, sel,
                   pl.BlockSpec((ne, LANES), lambda i: (0, 0))],
        out_shape=[sd((ntok, d), F32), sd((ntok, d // 2), jnp.uint32), sd((TOP_K, ntok), jnp.int32),
                   sd((TOP_K, ntok), F32), sd((TOP_K, ntok), jnp.int32), sd((ne, LANES), F32)],
        scratch_shapes=[pltpu.VMEM((ne, LANES), F32)],
        compiler_params=_cparams(("arbitrary",)),
        name="outproj_router",
    )(y, x, mod, mod, w_out_bf16, g_ffn.reshape(1, d), w_router.T, b_router.reshape(ne, 1))


EXPERT_TILE = 256
DMA_UNROLL = 4


def _dispatch_kernel(dest_ref, f_ref, xs_in_ref, xs_ref, sem, *, tm):
    del xs_in_ref

    def row_copy(j, k):
        dst = dest_ref[0, 0, k * tm + j]
        return pltpu.make_async_copy(f_ref.at[pl.ds(j, 1), :], xs_ref.at[pl.ds(dst, 1), :], sem)

    def issue(j, carry):
        for k in range(TOP_K):
            row_copy(j, k).start()
        return carry

    lax.fori_loop(0, tm, issue, 0, unroll=DMA_UNROLL)

    def drain(j, carry):
        for k in range(TOP_K):
            row_copy(j, k).wait()
        return carry

    lax.fori_loop(0, tm, drain, 0, unroll=DMA_UNROLL)


def _dispatch(f, dest_tiles, n_rows):
    ntok, d = f.shape
    tm = dest_tiles.shape[-1] // TOP_K
    kern = functools.partial(_dispatch_kernel, tm=tm)
    return pl.pallas_call(
        kern,
        grid=(ntok // tm,),
        in_specs=[pl.BlockSpec((1, 1, TOP_K * tm), lambda i: (i, 0, 0), memory_space=pltpu.SMEM),
                  pl.BlockSpec((tm, d), lambda i: (i, 0)),
                  pl.BlockSpec(memory_space=pl.ANY)],
        out_specs=pl.BlockSpec(memory_space=pl.ANY),
        out_shape=jax.ShapeDtypeStruct((n_rows, d), f.dtype),
        scratch_shapes=[pltpu.SemaphoreType.DMA(())],
        input_output_aliases={2: 0},
        compiler_params=_cparams(("arbitrary",)),
        name="moe_dispatch",
    )(dest_tiles, f, jnp.zeros((n_rows, d), f.dtype))


def _experts_kernel(te_ref, xs_ref, w1_ref, b1_ref, w2_ref, b2_ref, ys_ref, w1_scr, w2_scr):
    i = pl.program_id(0)

    @pl.when((i == 0) | (te_ref[i] != te_ref[jnp.maximum(i - 1, 0)]))
    def _():
        w1_scr[...] = w1_ref[0, 0].astype(BF16)
        w2_scr[...] = w2_ref[0, 0].astype(BF16)

    dh = xs_ref.shape[1]
    x_lo, x_hi = _unpack_bf16_pair(xs_ref[...])
    hu = (jnp.dot(x_lo.astype(BF16), w1_scr[0:dh, :], preferred_element_type=F32)
          + jnp.dot(x_hi.astype(BF16), w1_scr[dh:2 * dh, :], preferred_element_type=F32)
          + b1_ref[0])
    gt = jnp.minimum(hu[:, :D_EXPERT], SWIGLU_LIMIT)
    up = jnp.clip(hu[:, D_EXPERT:], -SWIGLU_LIMIT, SWIGLU_LIMIT)
    act = gt * _sigmoid(SWIGLU_ALPHA * gt) * (up + 1.0)
    y = jnp.dot(act.astype(BF16), w2_scr[...], preferred_element_type=F32) + b2_ref[0]
    ys_ref[...] = _pack_bf16_pair(y[:, :dh], y[:, dh:])


def _experts(xs, tile_expert, w1_all, b1, w2_all, b2, *, layer):
    n_rows, dh = xs.shape
    _, ne, d, de2 = w1_all.shape
    tmx = EXPERT_TILE
    grid_spec = pltpu.PrefetchScalarGridSpec(
        num_scalar_prefetch=1,
        grid=(n_rows // tmx,),
        in_specs=[pl.BlockSpec((tmx, dh), lambda i, te: (i, 0)),
                  pl.BlockSpec((1, 1, d, de2), lambda i, te: (layer, te[i], 0, 0)),
                  pl.BlockSpec((1, 1, de2), lambda i, te: (te[i], 0, 0)),
                  pl.BlockSpec((1, 1, de2 // 2, d), lambda i, te: (layer, te[i], 0, 0)),
                  pl.BlockSpec((1, 1, d), lambda i, te: (te[i], 0, 0))],
        out_specs=pl.BlockSpec((tmx, dh), lambda i, te: (i, 0)),
        scratch_shapes=[pltpu.VMEM((d, de2), BF16), pltpu.VMEM((de2 // 2, d), BF16)],
    )
    return pl.pallas_call(
        _experts_kernel,
        grid_spec=grid_spec,
        out_shape=jax.ShapeDtypeStruct((n_rows, dh), jnp.uint32),
        compiler_params=_cparams(("arbitrary",)),
        name="moe_experts",
    )(tile_expert, xs, w1_all, b1.reshape(ne, 1, de2), w2_all, b2.reshape(ne, 1, d))


def _combine_kernel(dest_ref, x_ref, wgt_ref, ml_ref, mc_ref, ys_ref, xo_ref, rows_scr, sem,
                    *, tm, seg, ctx_len):
    def row_copy(j, k):
        src = dest_ref[0, 0, k * tm + j]
        return pltpu.make_async_copy(ys_ref.at[pl.ds(src, 1), :],
                                     rows_scr.at[k, pl.ds(j, 1), :], sem)

    def issue(j, carry):
        for k in range(TOP_K):
            row_copy(j, k).start()
        return carry

    lax.fori_loop(0, tm, issue, 0, unroll=DMA_UNROLL)

    def drain(j, carry):
        for k in range(TOP_K):
            row_copy(j, k).wait()
        return carry

    lax.fori_loop(0, tm, drain, 0, unroll=DMA_UNROLL)

    wgt = wgt_ref[...]
    y_lo = None
    y_hi = None
    for k in range(TOP_K):
        lo, hi = _unpack_bf16_pair(rows_scr[k])
        wk = wgt[:, k:k + 1]
        y_lo = wk * lo if y_lo is None else y_lo + wk * lo
        y_hi = wk * hi if y_hi is None else y_hi + wk * hi
    is_ctx = _ctx_rows(pl.program_id(0), tm, seg, ctx_len)
    gate = jnp.where(is_ctx, mc_ref[0][5:6], ml_ref[0][5:6])
    xo_ref[...] = x_ref[...] + gate * jnp.concatenate([y_lo, y_hi], axis=1)


def _combine(ys, dest_tiles, x, wgt_t, mod, *, n_batch, seg, ctx_len):
    ntok, d = x.shape
    tm = dest_tiles.shape[-1] // TOP_K
    per = seg // tm
    kern = functools.partial(_combine_kernel, tm=tm, seg=seg, ctx_len=ctx_len)
    tok = pl.BlockSpec((tm, d), lambda i: (i, 0))
    return pl.pallas_call(
        kern,
        grid=(ntok // tm,),
        in_specs=[pl.BlockSpec((1, 1, TOP_K * tm), lambda i: (i, 0, 0), memory_space=pltpu.SMEM),
                  tok,
                  pl.BlockSpec((tm, TOP_K), lambda i: (i, 0)),
                  pl.BlockSpec((1, 6, d), lambda i: (i // per, 0, 0)),
                  pl.BlockSpec((1, 6, d), lambda i: (n_batch, 0, 0)),
                  pl.BlockSpec(memory_space=pl.ANY)],
        out_specs=tok,
        out_shape=jax.ShapeDtypeStruct((ntok, d), F32),
        scratch_shapes=[pltpu.VMEM((TOP_K, tm, ys.shape[1]), ys.dtype), pltpu.SemaphoreType.DMA(())],
        compiler_params=_cparams(("arbitrary",)),
        name="moe_combine",
    )(dest_tiles, x, wgt_t, mod, mod, ys)


def _route_plan(idx, rank, counts, tm):
    tmx = EXPERT_TILE
    ntok = idx.shape[1]
    cnt = counts[:, 0].astype(jnp.int32)
    padded = ((cnt + tmx - 1) // tmx) * tmx
    ends = jnp.cumsum(padded)
    starts = ends - padded
    experts = jnp.arange(N_EXPERTS, dtype=jnp.int32)
    dest = rank + jnp.sum(jnp.where(idx[..., None] == experts, starts, 0), axis=-1)
    n_tiles = (TOP_K * ntok) // tmx + N_EXPERTS
    tile_start = jnp.arange(n_tiles, dtype=jnp.int32) * tmx
    tile_expert = jnp.minimum(jnp.sum((tile_start[:, None] >= ends[None, :]).astype(jnp.int32), axis=1),
                              N_EXPERTS - 1)
    nt = ntok // tm
    dest_tiles = dest.reshape(TOP_K, nt, tm).transpose(1, 0, 2).reshape(nt, 1, TOP_K * tm)
    return dest_tiles, tile_expert, n_tiles * tmx


def _final_norm_kernel(x_ref, g_ref, o_ref):
    x = x_ref[...]
    o_ref[0] = x * lax.rsqrt(jnp.mean(x * x, axis=-1, keepdims=True) + NORM_EPS) * g_ref[...]


def _final_norm(x, g, *, n_batch, seg, ctx_len):
    d = x.shape[1]
    n_lat = seg - ctx_len
    tm = _tile(math.gcd(ctx_len, n_lat), 512)
    per = n_lat // tm
    return pl.pallas_call(
        _final_norm_kernel,
        grid=(n_batch, per),
        in_specs=[pl.BlockSpec((tm, d), lambda b, j: (b * (seg // tm) + ctx_len // tm + j, 0)),
                  pl.BlockSpec((1, d), lambda b, j: (0, 0))],
        out_specs=pl.BlockSpec((1, tm, d), lambda b, j: (b, j, 0)),
        out_shape=jax.ShapeDtypeStruct((n_batch, n_lat, d), F32),
        compiler_params=_cparams(("arbitrary", "arbitrary")),
        name="final_norm",
    )(x, g.reshape(1, d))


def kernel(x, c, ctx, c_ctx, norm_mix_g, norm_ffn_g, w_mod, b_mod, w_in, dn_conv, dn_a_log, dn_dt_bias, dn_norm_g, rw_mu, rw_w0, rw_w2, rw_a0, rw_a2, rw_g2, rw_k_k, rw_k_a, rw_r_k, rw_ln_g, rw_ln_b, ssm_conv, ssm_conv_b, ssm_a_log, ssm_dt_bias, ssm_d, ssm_norm_g, w_branch, w_out, w_router, b_router, w_e1, b_e1, w_e2, b_e2, final_norm_g):
    n_batch, seq, d = x.shape
    ctx_len = ctx.shape[1]
    seg = seq + ctx_len
    n_layers = w_in.shape[0]
    assert seq % GRID_W == 0 and ctx_len % SSM_CHUNK == 0 and seq % SSM_CHUNK == 0
    off = _p_layout(d)
    dims = dict(n_batch=n_batch, seg=seg, ctx_len=ctx_len)

    mod_rows = -(-(n_batch + 1) // 8) * 8
    cond = jnp.concatenate([c, c_ctx[None], jnp.zeros((mod_rows - n_batch - 1, d), F32)], axis=0)
    mod_all = _modulation(cond, w_mod, b_mod).reshape(n_layers, mod_rows, 6, d)

    xa = jnp.concatenate([ctx, x], axis=1).reshape(n_batch * seg, d)
    route_tile = _tile(seg, ROUTE_TILE, LANES)
    for i in range(n_layers):
        mod = mod_all[i]
        p, p_small = _in_projection(xa, norm_mix_g[i], mod, _pack_w_in(w_in[i], d), **dims)
        small = _small_prep(p_small, dn_a_log[i], dn_dt_bias[i], ssm_a_log[i], ssm_dt_bias[i])

        qkv = _dn_conv(p, dn_conv[i], col0=off["dn_qkv"], **dims)
        oa = _dn_readout(*_dn_scan(qkv, small, **dims), p, dn_norm_g[i], col_gate=off["dn_gate"])

        shifted = _rw_shift(p, _pack_rw_vec(rw_mu[i]), col0=off["rw_rkv"], **dims)
        r, v, nk, gate, bonus, lw, kd, b = _rw_prep(
            shifted, rw_w0[i], rw_w2[i], rw_a0[i], rw_a2[i], rw_g2[i], rw_k_k[i], rw_k_a[i],
            rw_r_k[i])
        ob = _rw_readout(*_rw_scan(r, v, nk, lw, kd, b, **dims), bonus, gate, rw_ln_g[i],
                         rw_ln_b[i])

        xbc = _conv_silu(p, ssm_conv[i], ssm_conv_b[i], col0=off["ssm_xbc"], n_norm_blocks=0,
                         name="ssm_conv", **dims)
        oc = _ssd_readout(*_ssd_scan(xbc, small, **dims), xbc, p, ssm_d[i], ssm_norm_g[i],
                          col_z=off["ssm_z"])

        y = _branch_mix(oa, ob, oc, p, w_branch[i].astype(BF16), d_model=d)
        xa, f, idx, wgt, rank, counts = _outproj_router(
            y, xa, mod, w_out[i].astype(BF16), norm_ffn_g[i], w_router[i], b_router[i], **dims)
        dest_tiles, tile_expert, n_rows = _route_plan(idx, rank, counts, route_tile)
        xs = _dispatch(f, dest_tiles, n_rows)
        ys = _experts(xs, tile_expert, w_e1, b_e1[i], w_e2, b_e2[i], layer=i)
        xa = _combine(ys, dest_tiles, xa, wgt.T, mod, **dims)
    return _final_norm(xa, final_norm_g, **dims)
```
